```python
import math
import jax
import jax.numpy as jnp
from jax import lax
import numpy as np

D_MODEL = 2048
BATCH = 1
SEQ = 8192
DEPTH = 1
DEC_BATCH = 32
DEC_SEQ = 32
PAST_LEN = 1024

CHUNK = 64
Q_BLOCK = 128
DENSE_KEY_LIMIT = 2048
N_ATT_HEADS = 8
ATT_HEAD_DIM = 64
ATT_WIDTH = N_ATT_HEADS * 2 * ATT_HEAD_DIM
ATT_SCALE = ATT_HEAD_DIM ** -0.5
N_BUCKETS = 32
MAX_DISTANCE = 128
NEG_INF = -1e30
SSM_D_INNER = 2048
SSM_HEAD_DIM = 64
N_SSM_HEADS = SSM_D_INNER // SSM_HEAD_DIM
N_SSM_GROUPS = 4
HEADS_PER_GROUP = N_SSM_HEADS // N_SSM_GROUPS
D_STATE = 128
CONV_WIDTH = 4
CONV_DIM = SSM_D_INNER + 2 * N_SSM_GROUPS * D_STATE
N_EXPERTS = 32
TOP_K = 4
D_FF = 2048
SWIGLU_LIMIT = 7.0
SWIGLU_ALPHA = 1.702
MOE_BLOCK = 256
PLE_DIM = 256
EPS = 1e-6
IN_SPLITS = (ATT_WIDTH, ATT_WIDTH, ATT_WIDTH, SSM_D_INNER, CONV_DIM, N_SSM_HEADS, D_MODEL, D_MODEL)
IN_WIDTH = sum(IN_SPLITS)

kernel_name = 'hybrid_diffattn_ssd_moe_stream_step'


def _rms(x):
    xf = x.astype(jnp.float32)
    return (xf * lax.rsqrt(jnp.mean(xf * xf, axis=-1, keepdims=True) + EPS)).astype(x.dtype)


def rmsnorm(x, g):
    return _rms(x) * g


def t5_bucket(rel):
    nb = N_BUCKETS // 2
    max_exact = nb // 2
    ret = jnp.where(rel > 0, nb, 0)
    n = jnp.abs(rel)
    large = max_exact + (jnp.log(jnp.maximum(n, 1).astype(jnp.float32) / max_exact)
                         / math.log(MAX_DISTANCE / max_exact) * (nb - max_exact)).astype(jnp.int32)
    large = jnp.minimum(large, nb - 1)
    return ret + jnp.where(n < max_exact, n, large)


def diff_attention(q, k, v, q_pos, k_pos, rel_bias, lam):
    bsz, lq = q.shape[0], q.shape[1]
    lk = k.shape[1]

    def attend(qb, qp):
        s = jnp.einsum('bqhcd,bkhcd->bhcqk', qb, k).astype(jnp.float32) * ATT_SCALE
        bias = jnp.transpose(rel_bias[t5_bucket(k_pos[None, :] - qp[:, None])], (2, 0, 1)).astype(jnp.float32)
        visible = (k_pos[None, :] // CHUNK) <= (qp[:, None] // CHUNK)
        s = jnp.where(visible, s + bias[None, :, None], NEG_INF)
        pr = jax.nn.softmax(s, axis=-1)
        wts = pr[:, :, 0] - lam * pr[:, :, 1]
        return jnp.einsum('bhqk,bkhe->bqhe', wts.astype(v.dtype), v)

    if lk >= DENSE_KEY_LIMIT and lq > Q_BLOCK and lq % Q_BLOCK == 0:
        nb = lq // Q_BLOCK
        qb = jnp.moveaxis(q.reshape(bsz, nb, Q_BLOCK, *q.shape[2:]), 1, 0)
        out = lax.map(lambda a: attend(a[0], a[1]), (qb, q_pos.reshape(nb, Q_BLOCK)))
        return jnp.moveaxis(out, 0, 1).reshape(bsz, lq, *out.shape[3:])
    return attend(q, q_pos)


def causal_depthwise_conv(u, hist, w, b):
    full = jnp.concatenate([hist.astype(u.dtype), u], axis=1)
    out = lax.conv_general_dilated(full, w[:, None, :].astype(u.dtype), window_strides=(1,), padding='VALID',
                                   dimension_numbers=('NWC', 'WIO', 'NWC'), feature_group_count=u.shape[-1])
    return out + b, full[:, full.shape[1] - (CONV_WIDTH - 1):]


def ssd_chunk(state, xs, dt, a, bm, cm):
    seq_len = xs.shape[1]
    a_cum = jnp.cumsum(dt * a, axis=1)
    seg = a_cum[:, :, None] - a_cum[:, None, :]
    causal = jnp.tril(jnp.ones((seq_len, seq_len), bool))[None, :, :, None, None]
    decay = jnp.exp(jnp.where(causal, seg, -jnp.inf))
    xdt = xs * dt[..., None]
    cb = jnp.einsum('blgn,bsgn->blsg', cm, bm)
    y = jnp.einsum('blsg,blsgr,bsgrp->blgrp', cb, decay, xdt)
    y = y + jnp.einsum('blgn,bgrpn->blgrp', cm, state) * jnp.exp(a_cum)[..., None]
    tail = jnp.exp(a_cum[:, -1:] - a_cum)
    new_state = state * jnp.exp(a_cum[:, -1])[..., None, None] + jnp.einsum('bsgn,bsgr,bsgrp->bgrpn', bm, tail, xdt)
    return y, new_state


def ssd(state, xs, dt, a, bm, cm):
    seq_len = xs.shape[1]
    if seq_len <= CHUNK:
        return ssd_chunk(state, xs, dt, a, bm, cm)
    n = seq_len // CHUNK

    def to_chunks(t):
        return jnp.moveaxis(t.reshape(t.shape[0], n, CHUNK, *t.shape[2:]), 1, 0)

    def step(s, inp):
        y, s_new = ssd_chunk(s, inp[0], inp[1], a, inp[2], inp[3])
        return s_new, y

    s_fin, ys = lax.scan(step, state, (to_chunks(xs), to_chunks(dt), to_chunks(bm), to_chunks(cm)))
    ys = jnp.moveaxis(ys, 0, 1)
    return ys.reshape(ys.shape[0], seq_len, *ys.shape[3:]), s_fin


def moe(xn, w_router, b_router, w_gu, b_gu, w_dn, b_dn):
    bsz, seq_len, dm = xn.shape
    x2 = xn.reshape(-1, dm)
    n_tok = x2.shape[0]
    logits = (x2 @ w_router + b_router).astype(jnp.float32)
    top_v, top_i = lax.top_k(logits, TOP_K)
    gates = jax.nn.softmax(top_v, axis=-1).astype(xn.dtype)
    n_slots = n_tok * TOP_K
    flat_e = top_i.reshape(-1)
    order = jnp.argsort(flat_e)
    sorted_e = flat_e[order]
    counts = jnp.bincount(flat_e, length=N_EXPERTS)
    padded = (counts + MOE_BLOCK - 1) // MOE_BLOCK * MOE_BLOCK
    pad_end = jnp.cumsum(padded)
    pad_start = pad_end - padded
    start = jnp.cumsum(counts) - counts
    dest = pad_start[sorted_e] + jnp.arange(n_slots) - start[sorted_e]
    n_pad = -(-n_slots // MOE_BLOCK) * MOE_BLOCK + N_EXPERTS * MOE_BLOCK
    n_blocks = n_pad // MOE_BLOCK
    slot_tok = jnp.full((n_pad,), n_tok, jnp.int32).at[dest].set((order // TOP_K).astype(jnp.int32))
    slot_gate = jnp.zeros((n_pad,), xn.dtype).at[dest].set(gates.reshape(-1)[order])
    block_e = jnp.minimum(jnp.searchsorted(pad_end, jnp.arange(n_blocks) * MOE_BLOCK, side='right'), N_EXPERTS - 1)
    x_pad = jnp.concatenate([x2, jnp.zeros((1, dm), x2.dtype)], axis=0)

    def run_block(args):
        tok, e = args
        hg = x_pad[tok] @ w_gu[e] + b_gu[e]
        g, u = jnp.split(hg, 2, axis=-1)
        g = jnp.minimum(g, SWIGLU_LIMIT)
        u = jnp.clip(u, -SWIGLU_LIMIT, SWIGLU_LIMIT)
        act = (u + 1.0) * g * jax.nn.sigmoid(SWIGLU_ALPHA * g)
        return act @ w_dn[e] + b_dn[e]

    y_slots = lax.map(run_block, (slot_tok.reshape(n_blocks, MOE_BLOCK), block_e))
    y_slots = y_slots.reshape(n_pad, dm) * slot_gate[:, None]
    out = jnp.zeros((n_tok + 1, dm), y_slots.dtype).at[slot_tok].add(y_slots)[:n_tok]
    return out.reshape(bsz, seq_len, dm)


def _trunk(x, p, k_past, v_past, ssm0, conv0, rel_bias, w_in, lambda_q1, lambda_k1, lambda_q2, lambda_k2,
           attn_subln, w_attn_out, conv_w, conv_b, dt_bias, a_log, d_skip, ssm_norm, w_ssm_out, w_o,
           g_mix, g_ffn, w_router, b_router, w_gate_up, b_gate_up, w_down, b_down, g_ple, w_ple_gate,
           w_ple_proj, g_final):
    bsz, seq_len, _ = x.shape
    past = k_past.shape[2]
    q_pos = past + jnp.arange(seq_len, dtype=jnp.int32)
    k_pos = jnp.arange(past + seq_len, dtype=jnp.int32)
    split_points = np.cumsum(IN_SPLITS)[:-1].tolist()
    h = x
    ks, vs, ssms, convs = [], [], [], []
    for i in range(DEPTH):
        lam_init = 0.8 - 0.6 * math.exp(-0.3 * i)
        hn = rmsnorm(h, g_mix[i])
        q, k, v, z, xbc, dt_raw, g_att, g_ssm = jnp.split(hn @ w_in[i], split_points, axis=-1)
        k_rows = k.reshape(bsz, seq_len, N_ATT_HEADS, 2 * ATT_HEAD_DIM)
        v_rows = v.reshape(bsz, seq_len, N_ATT_HEADS, 2 * ATT_HEAD_DIM)
        k_all = jnp.concatenate([k_past[i].astype(k_rows.dtype), k_rows], axis=1)
        v_all = jnp.concatenate([v_past[i].astype(v_rows.dtype), v_rows], axis=1)
        lam = (jnp.exp(jnp.sum(lambda_q1[i] * lambda_k1[i]).astype(jnp.float32))
               - jnp.exp(jnp.sum(lambda_q2[i] * lambda_k2[i]).astype(jnp.float32)) + lam_init)
        o = diff_attention(q.reshape(bsz, seq_len, N_ATT_HEADS, 2, ATT_HEAD_DIM),
                           k_all.reshape(bsz, past + seq_len, N_ATT_HEADS, 2, ATT_HEAD_DIM),
                           v_all, q_pos, k_pos, rel_bias, lam)
        o = rmsnorm(o, attn_subln[i]) * (1.0 - lam_init)
        att_branch = o.reshape(bsz, seq_len, ATT_WIDTH) @ w_attn_out[i]
        xbc_c, conv_new = causal_depthwise_conv(xbc, conv0[i], conv_w[i], conv_b[i])
        xbc_c = jax.nn.silu(xbc_c)
        xs, b_in, c_in = jnp.split(xbc_c, [SSM_D_INNER, SSM_D_INNER + N_SSM_GROUPS * D_STATE], axis=-1)
        xs5 = xs.reshape(bsz, seq_len, N_SSM_GROUPS, HEADS_PER_GROUP, SSM_HEAD_DIM).astype(jnp.float32)
        bm = b_in.reshape(bsz, seq_len, N_SSM_GROUPS, D_STATE).astype(jnp.float32)
        cm = c_in.reshape(bsz, seq_len, N_SSM_GROUPS, D_STATE).astype(jnp.float32)
        dt = jax.nn.softplus((dt_raw + dt_bias[i]).astype(jnp.float32)).reshape(bsz, seq_len, N_SSM_GROUPS, HEADS_PER_GROUP)
        a = -jnp.exp(a_log[i].astype(jnp.float32)).reshape(N_SSM_GROUPS, HEADS_PER_GROUP)
        s0 = ssm0[i].astype(jnp.float32).reshape(bsz, N_SSM_GROUPS, HEADS_PER_GROUP, SSM_HEAD_DIM, D_STATE)
        y, s_new = ssd(s0, xs5, dt, a, bm, cm)
        y = y + d_skip[i].astype(jnp.float32).reshape(N_SSM_GROUPS, HEADS_PER_GROUP)[..., None] * xs5
        yz = y.reshape(bsz, seq_len, SSM_D_INNER).astype(h.dtype) * jax.nn.silu(z)
        yz = _rms(yz.reshape(bsz, seq_len, N_SSM_GROUPS, SSM_D_INNER // N_SSM_GROUPS)).reshape(bsz, seq_len, SSM_D_INNER) * ssm_norm[i]
        ssm_branch = yz @ w_ssm_out[i]
        mixed = jax.nn.sigmoid(g_att) * att_branch + jax.nn.sigmoid(g_ssm) * ssm_branch
        h = h + mixed @ w_o[i]
        h = h + moe(rmsnorm(h, g_ffn[i]), w_router[i], b_router[i], w_gate_up[i], b_gate_up[i], w_down[i], b_down[i])
        h = h + jax.nn.sigmoid(rmsnorm(h, g_ple[i]) @ w_ple_gate[i]) * (p[i] @ w_ple_proj[i])
        ks.append(k_rows)
        vs.append(v_rows)
        ssms.append(s_new.reshape(bsz, N_SSM_HEADS, SSM_HEAD_DIM, D_STATE).astype(ssm0.dtype))
        convs.append(conv_new.astype(conv0.dtype))
    return rmsnorm(h, g_final), jnp.stack(ks), jnp.stack(vs), jnp.stack(ssms), jnp.stack(convs)


def setup_inputs(seed: int = 0) -> dict:
    key = jax.random.key(seed)
    keys = iter(jax.random.split(key, 40))

    def nrm(shape, scale):
        return jax.random.normal(next(keys), shape, jnp.float32) * scale

    def gain(shape):
        return 1.0 + nrm(shape, 0.02)

    dt0 = jnp.exp(jax.random.uniform(next(keys), (DEPTH, N_SSM_HEADS), jnp.float32, math.log(1e-3), math.log(0.1)))
    return {
        'x_prompt': nrm((BATCH, SEQ, D_MODEL), 1.0),
        'x_sample': nrm((DEC_BATCH, DEC_SEQ, D_MODEL), 1.0),
        'cache_k': nrm((DEPTH, DEC_BATCH, PAST_LEN, N_ATT_HEADS, 2 * ATT_HEAD_DIM), 1.0),
        'cache_v': nrm((DEPTH, DEC_BATCH, PAST_LEN, N_ATT_HEADS, 2 * ATT_HEAD_DIM), 1.0),
        'state_ssm': nrm((DEPTH, DEC_BATCH, N_SSM_HEADS, SSM_HEAD_DIM, D_STATE), 0.1),
        'state_conv': nrm((DEPTH, DEC_BATCH, CONV_WIDTH - 1, CONV_DIM), 1.0),
        'p_prompt': nrm((DEPTH, BATCH, SEQ, PLE_DIM), 1.0),
        'p_sample': nrm((DEPTH, DEC_BATCH, DEC_SEQ, PLE_DIM), 1.0),
        'rel_bias': nrm((N_BUCKETS, N_ATT_HEADS), 0.1),
        'w_in': nrm((DEPTH, D_MODEL, IN_WIDTH), D_MODEL ** -0.5),
        'lambda_q1': nrm((DEPTH, ATT_HEAD_DIM), 0.1),
        'lambda_k1': nrm((DEPTH, ATT_HEAD_DIM), 0.1),
        'lambda_q2': nrm((DEPTH, ATT_HEAD_DIM), 0.1),
        'lambda_k2': nrm((DEPTH, ATT_HEAD_DIM), 0.1),
        'attn_subln': gain((DEPTH, 2 * ATT_HEAD_DIM)),
        'w_attn_out': nrm((DEPTH, ATT_WIDTH, D_MODEL), ATT_WIDTH ** -0.5),
        'conv_w': nrm((DEPTH, CONV_WIDTH, CONV_DIM), CONV_WIDTH ** -0.5),
        'conv_b': nrm((DEPTH, CONV_DIM), 0.02),
        'dt_bias': dt0 + jnp.log(-jnp.expm1(-dt0)),
        'a_log': jnp.log(jax.random.uniform(next(keys), (DEPTH, N_SSM_HEADS), jnp.float32, 1.0, 16.0)),
        'd_skip': gain((DEPTH, N_SSM_HEADS)),
        'ssm_norm': gain((DEPTH, SSM_D_INNER)),
        'w_ssm_out': nrm((DEPTH, SSM_D_INNER, D_MODEL), SSM_D_INNER ** -0.5),
        'w_o': nrm((DEPTH, D_MODEL, D_MODEL), D_MODEL ** -0.5),
        'g_mix': gain((DEPTH, D_MODEL)),
        'g_ffn': gain((DEPTH, D_MODEL)),
        'w_router': nrm((DEPTH, D_MODEL, N_EXPERTS), D_MODEL ** -0.5),
        'b_router': nrm((DEPTH, N_EXPERTS), 0.01),
        'w_gate_up': nrm((DEPTH, N_EXPERTS, D_MODEL, 2 * D_FF), D_MODEL ** -0.5),
        'b_gate_up': nrm((DEPTH, N_EXPERTS, 2 * D_FF), 0.02),
        'w_down': nrm((DEPTH, N_EXPERTS, D_FF, D_MODEL), D_FF ** -0.5),
        'b_down': nrm((DEPTH, N_EXPERTS, D_MODEL), 0.02),
        'g_ple': gain((DEPTH, D_MODEL)),
        'w_ple_gate': nrm((DEPTH, D_MODEL, D_MODEL), D_MODEL ** -0.5),
        'w_ple_proj': nrm((DEPTH, PLE_DIM, D_MODEL), PLE_DIM ** -0.5),
        'g_final': gain((D_MODEL,)),
    }


def reference(x_prompt, x_sample, cache_k, cache_v, state_ssm, state_conv, p_prompt, p_sample, rel_bias, w_in,
              lambda_q1, lambda_k1, lambda_q2, lambda_k2, attn_subln, w_attn_out, conv_w, conv_b, dt_bias, a_log,
              d_skip, ssm_norm, w_ssm_out, w_o, g_mix, g_ffn, w_router, b_router, w_gate_up, b_gate_up, w_down,
              b_down, g_ple, w_ple_gate, w_ple_proj, g_final):
    weights = (rel_bias, w_in, lambda_q1, lambda_k1, lambda_q2, lambda_k2, attn_subln, w_attn_out, conv_w, conv_b,
               dt_bias, a_log, d_skip, ssm_norm, w_ssm_out, w_o, g_mix, g_ffn, w_router, b_router, w_gate_up,
               b_gate_up, w_down, b_down, g_ple, w_ple_gate, w_ple_proj, g_final)
    bsz = x_prompt.shape[0]
    k0 = jnp.zeros((DEPTH, bsz, 0, N_ATT_HEADS, 2 * ATT_HEAD_DIM), x_prompt.dtype)
    s0 = jnp.zeros((DEPTH, bsz, N_SSM_HEADS, SSM_HEAD_DIM, D_STATE), x_prompt.dtype)
    c0 = jnp.zeros((DEPTH, bsz, CONV_WIDTH - 1, CONV_DIM), x_prompt.dtype)
    y_prompt, k_prompt, v_prompt, ssm_prompt, conv_prompt = _trunk(x_prompt, p_prompt, k0, k0, s0, c0, *weights)
    y_sample, k_sample, v_sample, ssm_sample, conv_sample = _trunk(x_sample, p_sample, cache_k, cache_v, state_ssm,
                                                                   state_conv, *weights)
    return (y_prompt, y_sample, k_prompt, v_prompt, ssm_prompt, conv_prompt, k_sample, v_sample, ssm_sample, conv_sample)
```

```python
import functools
import math

import numpy as np
import jax
import jax.numpy as jnp
from jax import lax
from jax.experimental import pallas as pl
from jax.experimental.pallas import tpu as pltpu

F32 = jnp.float32
BF16 = jnp.bfloat16
U32 = jnp.uint32

D_MODEL = 2048
CHUNK = 64
N_ATT_HEADS = 8
ATT_HEAD_DIM = 64
HEAD_W = 2 * ATT_HEAD_DIM
ATT_WIDTH = N_ATT_HEADS * HEAD_W
ATT_SCALE = ATT_HEAD_DIM ** -0.5
N_BUCKETS = 32
MAX_DISTANCE = 128
NEG_INF = -1e30
SSM_D_INNER = 2048
SSM_HEAD_DIM = 64
N_SSM_HEADS = SSM_D_INNER // SSM_HEAD_DIM
N_SSM_GROUPS = 4
HEADS_PER_GROUP = N_SSM_HEADS // N_SSM_GROUPS
GROUP_W = HEADS_PER_GROUP * SSM_HEAD_DIM
D_STATE = 128
CONV_WIDTH = 4
CONV_DIM = SSM_D_INNER + 2 * N_SSM_GROUPS * D_STATE
N_EXPERTS = 32
TOP_K = 4
D_FF = 2048
SWIGLU_LIMIT = 7.0
SWIGLU_ALPHA = 1.702
PLE_DIM = 256
EPS = 1e-6
LAM_INIT = 0.8 - 0.6 * math.exp(-0.3 * 0)

LANES = 128
SUBLANES = 8
MIB = 1024 * 1024

INPROJ_TM = 512
INPROJ_TN = 512
ATT_TQ = 256
SSD_L_PROMPT = 128
SSD_LP = 128
TOK_TM = 256
ROW_BLK = 256
SUPER_ROWS = 1280
FF_TILE = 256


def _dot(a, b):
    return jnp.dot(a, b, preferred_element_type=F32)


def _dot_nt(a, b):
    return lax.dot_general(a, b, (((1,), (1,)), ((), ())), preferred_element_type=F32)


def _rms(x):
    return x * lax.rsqrt(jnp.mean(x * x, axis=-1, keepdims=True) + EPS)


def _sigmoid(x):
    return 1.0 / (1.0 + jnp.exp(-x))


def _split3(x):
    hi = x.astype(BF16)
    r = x - hi.astype(F32)
    mid = r.astype(BF16)
    lo = (r - mid.astype(F32)).astype(BF16)
    return hi, mid, lo


def _xdot_r(x, c):
    hi, mid, lo = _split3(x)
    return (_dot(hi, c) + _dot(mid, c)) + _dot(lo, c)


def _xdot_l(c, x):
    hi, mid, lo = _split3(x)
    return (_dot(c, hi) + _dot(c, mid)) + _dot(c, lo)


_SEG_WIDTHS = (("q", ATT_WIDTH), ("k", ATT_WIDTH), ("v", ATT_WIDTH), ("z", SSM_D_INNER),
               ("xbc", CONV_DIM), ("ga", D_MODEL), ("gs", D_MODEL))


def _segments():
    segs, first = {}, 0
    for name, width in _SEG_WIDTHS:
        assert width % INPROJ_TN == 0
        segs[name] = (first, width // INPROJ_TN)
        first += width // INPROJ_TN
    return segs, first


def _inproj_kernel(x_ref, g_ref, w_ref, wdt_ref, q_ref, k_ref, v_ref, kb_ref, vb_ref, z_ref, xbc_ref,
                   ga_ref, gs_ref, dt_ref, hn_ref):
    j = pl.program_id(1)
    segs, _ = _segments()

    @pl.when(j == 0)
    def _():
        hn = _rms(x_ref[...]) * g_ref[...]
        hn_ref[...] = hn.astype(BF16)
        dt_ref[...] = _dot(hn_ref[...], wdt_ref[...])

    acc = _dot(hn_ref[...], w_ref[...])

    def in_seg(name):
        lo, n = segs[name]
        return (j >= lo) & (j < lo + n)

    @pl.when(in_seg("q"))
    def _():
        q_ref[...] = acc.astype(BF16)

    @pl.when(in_seg("k"))
    def _():
        k_ref[...] = acc
        kb_ref[...] = acc.astype(BF16)

    @pl.when(in_seg("v"))
    def _():
        v_ref[...] = acc
        vb_ref[...] = acc.astype(BF16)

    @pl.when(in_seg("z"))
    def _():
        z_ref[...] = acc

    @pl.when(in_seg("xbc"))
    def _():
        xbc_ref[...] = acc

    @pl.when(in_seg("ga"))
    def _():
        ga_ref[...] = acc

    @pl.when(in_seg("gs"))
    def _():
        gs_ref[...] = acc


def _inproj(x, g_mix, w_main, w_dt):
    t = x.shape[0]
    tm, tn = INPROJ_TM, INPROJ_TN
    segs, n_tiles = _segments()
    assert t % tm == 0 and w_main.shape[1] == n_tiles * tn

    def seg_spec(name):
        lo, n = segs[name]
        return pl.BlockSpec((tm, tn), lambda i, j: (i, jnp.clip(j - lo, 0, n - 1)))

    def seg_shape(name, dtype):
        return jax.ShapeDtypeStruct((t, segs[name][1] * tn), dtype)

    out_shape = (seg_shape("q", BF16), seg_shape("k", F32), seg_shape("v", F32), seg_shape("k", BF16),
                 seg_shape("v", BF16), seg_shape("z", F32), seg_shape("xbc", F32), seg_shape("ga", F32),
                 seg_shape("gs", F32), jax.ShapeDtypeStruct((t, LANES), F32))
    out_specs = (seg_spec("q"), seg_spec("k"), seg_spec("v"), seg_spec("k"), seg_spec("v"), seg_spec("z"),
                 seg_spec("xbc"), seg_spec("ga"), seg_spec("gs"), pl.BlockSpec((tm, LANES), lambda i, j: (i, 0)))
    return pl.pallas_call(
        _inproj_kernel,
        out_shape=out_shape,
        grid=(t // tm, n_tiles),
        in_specs=[pl.BlockSpec((tm, D_MODEL), lambda i, j: (i, 0)),
                  pl.BlockSpec((1, D_MODEL), lambda i, j: (0, 0)),
                  pl.BlockSpec((D_MODEL, tn), lambda i, j: (0, j)),
                  pl.BlockSpec((D_MODEL, LANES), lambda i, j: (0, 0))],
        out_specs=out_specs,
        scratch_shapes=[pltpu.VMEM((tm, D_MODEL), BF16)],
        compiler_params=pltpu.CompilerParams(dimension_semantics=("arbitrary", "arbitrary"),
                                             vmem_limit_bytes=48 * MIB),
        name="inproj",
    )(x, g_mix, w_main, w_dt)


def _t5_bucket(rel):
    nb = N_BUCKETS // 2
    max_exact = nb // 2
    ret = jnp.where(rel > 0, nb, 0)
    n = jnp.abs(rel)
    large = max_exact + (jnp.log(jnp.maximum(n, 1).astype(jnp.float32) / max_exact)
                         / math.log(MAX_DISTANCE / max_exact) * (nb - max_exact)).astype(jnp.int32)
    large = jnp.minimum(large, nb - 1)
    return ret + jnp.where(n < max_exact, n, large)


def _rel_bias_table(rel_bias, q_pos, k_pos):
    bias = rel_bias[_t5_bucket(k_pos[None, :] - q_pos[:, None])]
    return jnp.transpose(bias, (2, 0, 1)).astype(F32)


def _split_maps(qh):
    lane = lax.broadcasted_iota(jnp.int32, qh.shape, 1)
    q1 = jnp.where(lane < ATT_HEAD_DIM, qh, 0.0) * ATT_SCALE
    q2 = jnp.where(lane >= ATT_HEAD_DIM, qh, 0.0) * ATT_SCALE
    return q1.astype(BF16), q2.astype(BF16)


def _subln(o, lam_unused, sub):
    return (_rms(o) * sub) * (1.0 - LAM_INIT)


def _attn_prompt_kernel(qi_ref, kj_ref, q_ref, k_ref, v_ref, bias_ref, lam_ref, sub_ref, o_ref,
                        qs_ref, m_ref, l_ref, acc_ref):
    s = pl.program_id(0)
    qi = qi_ref[s]
    kj = kj_ref[s]
    tq = q_ref.shape[0]
    tk = k_ref.shape[0]

    @pl.when(kj == 0)
    def _():
        for h in range(N_ATT_HEADS):
            q1, q2 = _split_maps(q_ref[:, h * HEAD_W:(h + 1) * HEAD_W].astype(F32))
            qs_ref[2 * h] = q1
            qs_ref[2 * h + 1] = q2
        m_ref[...] = jnp.full(m_ref.shape, NEG_INF, F32)
        l_ref[...] = jnp.zeros(l_ref.shape, F32)
        acc_ref[...] = jnp.zeros(acc_ref.shape, F32)

    def step(masked):
        if masked:
            row = lax.broadcasted_iota(jnp.int32, (tq, tk), 0)
            col = lax.broadcasted_iota(jnp.int32, (tq, tk), 1)
            shift = CHUNK.bit_length() - 1
            visible = jnp.right_shift(col, shift) <= jnp.right_shift(row, shift)
        for h in range(N_ATT_HEADS):
            kh = k_ref[:, h * HEAD_W:(h + 1) * HEAD_W]
            vh = v_ref[:, h * HEAD_W:(h + 1) * HEAD_W]
            bias = bias_ref[0, h]
            for c in range(2):
                idx = 2 * h + c
                sc = _dot_nt(qs_ref[idx], kh) + bias
                if masked:
                    sc = jnp.where(visible, sc, NEG_INF)
                m_old = m_ref[idx]
                m_new = jnp.maximum(m_old, jnp.max(sc, axis=-1, keepdims=True))
                alpha = jnp.exp(m_old - m_new)
                p = jnp.exp(sc - m_new)
                l_ref[idx] = alpha * l_ref[idx] + jnp.sum(p, axis=-1, keepdims=True)
                acc_ref[idx] = alpha * acc_ref[idx] + _dot(p.astype(BF16), vh)
                m_ref[idx] = m_new

    @pl.when(kj == qi)
    def _():
        step(True)

    @pl.when(kj != qi)
    def _():
        step(False)

    @pl.when(kj == qi)
    def _():
        lam = lam_ref[...]
        sub = sub_ref[...]
        for h in range(N_ATT_HEADS):
            o = acc_ref[2 * h] / l_ref[2 * h] - lam * (acc_ref[2 * h + 1] / l_ref[2 * h + 1])
            o_ref[:, h * HEAD_W:(h + 1) * HEAD_W] = _subln(o, None, sub).astype(BF16)


def _attn_prompt(q, k, v, rel_bias, lam_vec, sub):
    t = q.shape[0]
    tq = ATT_TQ
    assert t % tq == 0 and tq % CHUNK == 0 and tq >= MAX_DISTANCE
    nq = t // tq
    qi = np.concatenate([np.full(i + 1, i, np.int32) for i in range(nq)])
    kj = np.concatenate([np.arange(i + 1, dtype=np.int32) for i in range(nq)])
    pos = jnp.arange(tq, dtype=jnp.int32)
    bias = jnp.stack([_rel_bias_table(rel_bias, pos + d * tq, pos) for d in range(3)])

    grid_spec = pltpu.PrefetchScalarGridSpec(
        num_scalar_prefetch=2,
        grid=(qi.shape[0],),
        in_specs=[pl.BlockSpec((tq, ATT_WIDTH), lambda s, qi, kj: (qi[s], 0)),
                  pl.BlockSpec((tq, ATT_WIDTH), lambda s, qi, kj: (kj[s], 0)),
                  pl.BlockSpec((tq, ATT_WIDTH), lambda s, qi, kj: (kj[s], 0)),
                  pl.BlockSpec((1, N_ATT_HEADS, tq, tq),
                               lambda s, qi, kj: (jnp.minimum(qi[s] - kj[s], 2), 0, 0, 0)),
                  pl.BlockSpec((1, HEAD_W), lambda s, qi, kj: (0, 0)),
                  pl.BlockSpec((1, HEAD_W), lambda s, qi, kj: (0, 0))],
        out_specs=pl.BlockSpec((tq, ATT_WIDTH), lambda s, qi, kj: (qi[s], 0)),
        scratch_shapes=[pltpu.VMEM((2 * N_ATT_HEADS, tq, HEAD_W), BF16),
                        pltpu.VMEM((2 * N_ATT_HEADS, tq, 1), F32),
                        pltpu.VMEM((2 * N_ATT_HEADS, tq, 1), F32),
                        pltpu.VMEM((2 * N_ATT_HEADS, tq, HEAD_W), F32)])
    return pl.pallas_call(
        _attn_prompt_kernel,
        out_shape=jax.ShapeDtypeStruct((t, ATT_WIDTH), BF16),
        grid_spec=grid_spec,
        compiler_params=pltpu.CompilerParams(dimension_semantics=("arbitrary",), vmem_limit_bytes=40 * MIB),
        name="attn_prompt",
    )(jnp.asarray(qi), jnp.asarray(kj), q, k, v, bias, lam_vec, sub)


def _attn_sample_kernel(q_ref, kn_ref, vn_ref, ck_ref, cv_ref, bc_ref, bn_ref, mc_ref, mn_ref, lam_ref,
                        sub_ref, o_ref):
    lam = lam_ref[...]
    sub = sub_ref[...]
    vis_c = mc_ref[...] > 0.5
    vis_n = mn_ref[...] > 0.5
    for h in range(N_ATT_HEADS):
        hs = slice(h * HEAD_W, (h + 1) * HEAD_W)
        qmaps = _split_maps(q_ref[:, hs].astype(F32))
        kc = ck_ref[:, hs].astype(BF16)
        vc = cv_ref[:, hs].astype(BF16)
        kn = kn_ref[:, hs]
        vn = vn_ref[:, hs]
        probs = []
        for c in range(2):
            sc = jnp.where(vis_c, _dot_nt(qmaps[c], kc) + bc_ref[h], NEG_INF)
            sn = jnp.where(vis_n, _dot_nt(qmaps[c], kn) + bn_ref[h], NEG_INF)
            m = jnp.maximum(jnp.max(sc, axis=-1, keepdims=True), jnp.max(sn, axis=-1, keepdims=True))
            pc = jnp.exp(sc - m)
            pn = jnp.exp(sn - m)
            den = jnp.sum(pc, axis=-1, keepdims=True) + jnp.sum(pn, axis=-1, keepdims=True)
            probs.append((pc / den, pn / den))
        wc = probs[0][0] - lam[:, :1] * probs[1][0]
        wn = probs[0][1] - lam[:, :1] * probs[1][1]
        o = _dot(wc.astype(BF16), vc) + _dot(wn.astype(BF16), vn)
        o_ref[:, hs] = _subln(o, None, sub).astype(BF16)


def _attn_sample(q, kn, vn, cache_k, cache_v, rel_bias, lam_vec, sub):
    bsz, past, _ = cache_k.shape
    seq = q.shape[0] // bsz
    q_pos = past + jnp.arange(seq, dtype=jnp.int32)
    k_pos = jnp.arange(past + seq, dtype=jnp.int32)
    bias = _rel_bias_table(rel_bias, q_pos, k_pos)
    visible = ((k_pos[None, :] // CHUNK) <= (q_pos[:, None] // CHUNK)).astype(F32)
    const = lambda *shape: pl.BlockSpec(shape, lambda b: (0,) * len(shape))
    row = pl.BlockSpec((seq, ATT_WIDTH), lambda b: (b, 0))
    cache = pl.BlockSpec((None, past, ATT_WIDTH), lambda b: (b, 0, 0))
    return pl.pallas_call(
        _attn_sample_kernel,
        out_shape=jax.ShapeDtypeStruct(q.shape, BF16),
        grid=(bsz,),
        in_specs=[row, row, row, cache, cache, const(N_ATT_HEADS, seq, past), const(N_ATT_HEADS, seq, seq),
                  const(seq, past), const(seq, seq), const(1, HEAD_W), const(1, HEAD_W)],
        out_specs=row,
        compiler_params=pltpu.CompilerParams(dimension_semantics=("arbitrary",), vmem_limit_bytes=40 * MIB),
        name="attn_sample",
    )(q, kn, vn, cache_k, cache_v, bias[:, :, :past], bias[:, :, past:], visible[:, :past], visible[:, past:],
      lam_vec, sub)


def _ssd_constants(l):
    hl = HEADS_PER_GROUP * l
    lp = SSD_LP
    e_head = np.zeros((LANES, SSM_D_INNER), np.float32)
    for h in range(N_SSM_HEADS):
        e_head[h, h * SSM_HEAD_DIM:(h + 1) * SSM_HEAD_DIM] = 1.0
    e_grp = np.zeros((N_SSM_GROUPS, LANES, hl), np.float32)
    for g in range(N_SSM_GROUPS):
        for r in range(HEADS_PER_GROUP):
            e_grp[g, g * HEADS_PER_GROUP + r, r * l:(r + 1) * l] = 1.0
    tile8 = np.zeros((lp, hl), np.float32)
    for r in range(HEADS_PER_GROUP):
        tile8[np.arange(l), r * l + np.arange(l)] = 1.0
    causal = np.zeros((l, hl), np.float32)
    for r in range(HEADS_PER_GROUP):
        causal[:, r * l:(r + 1) * l] = np.tril(np.ones((l, l), np.float32))
    tri = np.tril(np.ones((l, l), np.float32))
    ones = np.ones((l, l), np.float32)
    bmask = np.zeros((hl, GROUP_W), np.float32)
    for r in range(HEADS_PER_GROUP):
        bmask[r * l:(r + 1) * l, r * SSM_HEAD_DIM:(r + 1) * SSM_HEAD_DIM] = 1.0
    as_bf = lambda a: jnp.asarray(a, BF16)
    return (as_bf(e_head), as_bf(e_grp), as_bf(tile8), jnp.asarray(causal), as_bf(tri), as_bf(ones),
            as_bf(bmask))


def _ssd_kernel(xbc_ref, z_ref, dt_ref, hist_ref, st0_ref, cw_ref, cb_ref, dtb_ref, alog_ref, dskip_ref,
                norm_ref, eh_ref, eg_ref, t8_ref, caus_ref, tri_ref, ones_ref, bmask_ref,
                yz_ref, st_ref, buf_ref, state_ref):
    c = pl.program_id(1)
    l = xbc_ref.shape[0]
    lp = SSD_LP
    hist_rows = hist_ref.shape[0]

    @pl.when(c == 0)
    def _():
        buf_ref[0:hist_rows, :] = hist_ref[...]
        state_ref[...] = st0_ref[...]

    u = xbc_ref[...]
    buf_ref[hist_rows:hist_rows + l, :] = u
    conv = cb_ref[...] + cw_ref[CONV_WIDTH - 1:CONV_WIDTH, :] * u
    for w in range(CONV_WIDTH - 1):
        shift = CONV_WIDTH - 1 - w
        conv = conv + cw_ref[w:w + 1, :] * buf_ref[hist_rows - shift:hist_rows - shift + l, :]
    buf_ref[0:hist_rows, :] = buf_ref[l:l + hist_rows, :]
    xc = conv * _sigmoid(conv)
    xs = xc[:, :SSM_D_INNER]
    bm = xc[:, SSM_D_INNER:SSM_D_INNER + N_SSM_GROUPS * D_STATE]
    cm = xc[:, SSM_D_INNER + N_SSM_GROUPS * D_STATE:]

    dt_in = dt_ref[...] + dtb_ref[...]
    dt = jnp.maximum(dt_in, 0.0) + jnp.log(1.0 + jnp.exp(-jnp.abs(dt_in)))
    a = -jnp.exp(alog_ref[...])
    acum = _xdot_l(tri_ref[...], dt * a)
    eh = eh_ref[...]
    dt_e = _xdot_r(dt, eh)
    ac_e = _xdot_r(acum, eh)
    a_last = ac_e[l - 1:l, :]
    ecum = jnp.exp(ac_e)
    xdt = xs * dt_e
    xdtw_b = (xdt * jnp.exp(a_last - ac_e)).astype(BF16)
    xdt_b = xdt.astype(BF16)
    drow = jnp.exp(a_last)
    z = z_ref[...]
    caus = caus_ref[...] > 0.5
    t8 = t8_ref[...]
    t8_mask = t8[0:l, :] > 0
    bmask = bmask_ref[...] > 0
    row_pad = lp - l

    for g in range(N_SSM_GROUPS):
        gs = slice(g * GROUP_W, (g + 1) * GROUP_W)
        ns = slice(g * D_STATE, (g + 1) * D_STATE)
        bm_g = bm[:, ns]
        cm_b = cm[:, ns].astype(BF16)
        xw_g = xdtw_b[:, gs]
        if row_pad:
            bm_g = jnp.concatenate([bm_g, jnp.zeros((row_pad, D_STATE), F32)], axis=0)
            xw_g = jnp.concatenate([xw_g, jnp.zeros((row_pad, GROUP_W), BF16)], axis=0)
        bmt_b = bm_g.T.astype(BF16)
        cb8 = _dot(cm_b, _dot(bmt_b, t8).astype(BF16))
        a1 = _xdot_r(acum, eg_ref[g])
        a2 = _xdot_l(ones_ref[...], jnp.where(t8_mask, a1, 0.0))
        decay = jnp.where(caus, jnp.exp(jnp.where(caus, a1 - a2, 0.0)), 0.0)
        m_b = (cb8 * decay).astype(BF16)
        xg = xdt_b[:, gs]
        bd = jnp.concatenate([xg] * HEADS_PER_GROUP, axis=0)
        bd = jnp.where(bmask, bd, jnp.zeros_like(bd))
        y = _dot(m_b, bd)
        st_g = state_ref[g]
        y = y + _dot(cm_b, st_g.astype(BF16)) * ecum[:, gs]
        y = y + dskip_ref[:, gs] * xs[:, gs]
        state_ref[g] = st_g * drow[:, gs] + _dot(bmt_b, xw_g)
        zg = z[:, gs]
        yz = y * (zg * _sigmoid(zg))
        yz_ref[:, gs] = (_rms(yz) * norm_ref[:, gs]).astype(BF16)

    @pl.when(c == pl.num_programs(1) - 1)
    def _():
        st_ref[...] = state_ref[...]


def _ssd(xbc, z, dt, hist8, st0, conv_w, conv_b, dt_bias, a_log, dskip_e, ssm_norm, bsz, l):
    rows = xbc.shape[0]
    seq = rows // bsz
    assert seq % l == 0 and l % SUBLANES == 0 and l <= SSD_LP and l >= SUBLANES
    nc = seq // l
    consts = _ssd_constants(l)
    rowblk = lambda width: pl.BlockSpec((l, width), lambda b, c: (b * nc + c, 0))
    const = lambda arr: pl.BlockSpec(arr.shape, lambda b, c: (0,) * arr.ndim)
    params = (conv_w, conv_b, dt_bias, a_log, dskip_e, ssm_norm)
    return pl.pallas_call(
        _ssd_kernel,
        out_shape=(jax.ShapeDtypeStruct((rows, SSM_D_INNER), BF16),
                   jax.ShapeDtypeStruct(st0.shape, F32)),
        grid=(bsz, nc),
        in_specs=[rowblk(CONV_DIM), rowblk(SSM_D_INNER), rowblk(LANES),
                  pl.BlockSpec((None,) + hist8.shape[1:], lambda b, c: (b, 0, 0)),
                  pl.BlockSpec((None,) + st0.shape[1:], lambda b, c: (b, 0, 0, 0))]
                 + [const(p) for p in params] + [const(k) for k in consts],
        out_specs=(rowblk(SSM_D_INNER), pl.BlockSpec((None,) + st0.shape[1:], lambda b, c: (b, 0, 0, 0))),
        scratch_shapes=[pltpu.VMEM((SUBLANES + l, CONV_DIM), F32),
                        pltpu.VMEM(st0.shape[1:], F32)],
        compiler_params=pltpu.CompilerParams(dimension_semantics=("arbitrary", "arbitrary"),
                                             vmem_limit_bytes=48 * MIB),
        name="ssd",
    )(xbc, z, dt, hist8, st0, *params, *consts)


def _mix_kernel(on_ref, yz_ref, ga_ref, gs_ref, wa_ref, ws_ref, o_ref):
    att = _dot(on_ref[...], wa_ref[...])
    ssm = _dot(yz_ref[...], ws_ref[...])
    o_ref[...] = (_sigmoid(ga_ref[...]) * att + _sigmoid(gs_ref[...]) * ssm).astype(BF16)


def _mix(on, yz, ga, gs, wa, ws):
    t = on.shape[0]
    tm = TOK_TM
    rowblk = lambda width: pl.BlockSpec((tm, width), lambda i: (i, 0))
    const = lambda arr: pl.BlockSpec(arr.shape, lambda i: (0,) * arr.ndim)
    return pl.pallas_call(
        _mix_kernel,
        out_shape=jax.ShapeDtypeStruct((t, D_MODEL), BF16),
        grid=(t // tm,),
        in_specs=[rowblk(ATT_WIDTH), rowblk(SSM_D_INNER), rowblk(D_MODEL), rowblk(D_MODEL), const(wa), const(ws)],
        out_specs=rowblk(D_MODEL),
        compiler_params=pltpu.CompilerParams(dimension_semantics=("arbitrary",), vmem_limit_bytes=48 * MIB),
        name="mix",
    )(on, yz, ga, gs, wa, ws)


def _pack_bf16_pairs(x):
    w = x.shape[1] // 2
    lo = lax.bitcast_convert_type(x[:, :w].astype(BF16).astype(F32), U32)
    hi = lax.bitcast_convert_type(x[:, w:].astype(BF16).astype(F32), U32)
    return hi | (lo >> 16)


def _unpack_bf16_pairs(words):
    lo = lax.bitcast_convert_type(words << 16, F32)
    hi = lax.bitcast_convert_type(words & jnp.uint32(0xFFFF0000), F32)
    return jnp.concatenate([lo, hi], axis=1).astype(BF16)


def _resid_kernel(x_ref, mixed_ref, wo_ref, g_ref, wr_ref, br_ref, lt_ref, cin_ref,
                  h_ref, xp_ref, route_ref, cnt_ref, carry_ref):
    i = pl.program_id(0)

    @pl.when(i == 0)
    def _():
        carry_ref[...] = cin_ref[...]

    h = x_ref[...] + _dot(mixed_ref[...], wo_ref[...])
    h_ref[...] = h
    xn = _rms(h) * g_ref[...]
    xp_ref[...] = _pack_bf16_pairs(xn)
    logits = _dot(xn.astype(BF16), wr_ref[...]) + br_ref[...]

    lane = lax.broadcasted_iota(jnp.int32, logits.shape, 1)
    lane_f = lane.astype(F32)
    rest = logits
    vals, idxs, sels = [], [], []
    for _ in range(TOP_K):
        m = jnp.max(rest, axis=-1, keepdims=True)
        idx = jnp.min(jnp.where(rest == m, lane_f, float(LANES)), axis=-1, keepdims=True)
        sel = lane_f == idx
        rest = jnp.where(sel, -jnp.inf, rest)
        vals.append(m)
        idxs.append(idx)
        sels.append(sel)
    exps = [jnp.exp(v - vals[0]) for v in vals]
    den = exps[0] + exps[1] + exps[2] + exps[3]
    onehot = jnp.zeros(logits.shape, F32)
    for sel in sels:
        onehot = onehot + jnp.where(sel, 1.0, 0.0)
    before = _dot(lt_ref[...], onehot.astype(BF16)) + carry_ref[...]
    route = jnp.zeros(logits.shape, F32)
    for k in range(TOP_K):
        pos = jnp.sum(jnp.where(sels[k], before, 0.0), axis=-1, keepdims=True)
        route = route + jnp.where(lane == k, idxs[k], 0.0)
        route = route + jnp.where(lane == TOP_K + k, exps[k] / den, 0.0)
        route = route + jnp.where(lane == 2 * TOP_K + k, pos, 0.0)
    route_ref[...] = route
    carry_ref[...] = carry_ref[...] + jnp.sum(onehot, axis=0, keepdims=True)
    cnt_ref[...] = carry_ref[...]


def _resid(x, mixed, wo, g_ffn, wr, br, counts_in):
    t = x.shape[0]
    tm = TOK_TM
    lt = jnp.asarray(np.tril(np.ones((tm, tm), np.float32), -1), BF16)
    rowblk = lambda width: pl.BlockSpec((tm, width), lambda i: (i, 0))
    const = lambda arr: pl.BlockSpec(arr.shape, lambda i: (0,) * arr.ndim)
    return pl.pallas_call(
        _resid_kernel,
        out_shape=(jax.ShapeDtypeStruct((t, D_MODEL), F32),
                   jax.ShapeDtypeStruct((t, D_MODEL // 2), U32),
                   jax.ShapeDtypeStruct((t, LANES), F32),
                   jax.ShapeDtypeStruct((1, LANES), F32)),
        grid=(t // tm,),
        in_specs=[rowblk(D_MODEL), rowblk(D_MODEL), const(wo), const(g_ffn), const(wr), const(br), const(lt),
                  const(counts_in)],
        out_specs=(rowblk(D_MODEL), rowblk(D_MODEL // 2), rowblk(LANES), pl.BlockSpec((1, LANES), lambda i: (0, 0))),
        scratch_shapes=[pltpu.VMEM((1, LANES), F32)],
        compiler_params=pltpu.CompilerParams(dimension_semantics=("arbitrary",), vmem_limit_bytes=48 * MIB),
        name="resid_route",
    )(x, mixed, wo, g_ffn, wr, br, lt, counts_in)


def _row_copy(src, src_row, dst, dst_row, sem):
    return pltpu.make_async_copy(src.at[pl.ds(src_row, 1)], dst.at[pl.ds(dst_row, 1)], sem)


def _dispatch_kernel(dest_ref, xp_ref, xs_in_ref, xs_ref, sem):
    del xs_in_ref
    base = pl.program_id(0) * TOK_TM

    def issue(t, carry):
        for k in range(TOP_K):
            _row_copy(xp_ref, base + t, xs_ref, dest_ref[t * TOP_K + k], sem).start()
        return carry

    lax.fori_loop(0, TOK_TM, issue, 0)

    def drain(t, carry):
        for k in range(TOP_K):
            _row_copy(xp_ref, 0, xs_ref, 0, sem).wait()
        return carry

    lax.fori_loop(0, TOK_TM, drain, 0)


def _dispatch(dest_flat, xp, xs):
    t = xp.shape[0]
    return pl.pallas_call(
        _dispatch_kernel,
        out_shape=jax.ShapeDtypeStruct(xs.shape, xs.dtype),
        grid=(t // TOK_TM,),
        in_specs=[pl.BlockSpec((TOK_TM * TOP_K,), lambda i: (i,), memory_space=pltpu.SMEM),
                  pl.BlockSpec(memory_space=pl.ANY),
                  pl.BlockSpec(memory_space=pl.ANY)],
        out_specs=pl.BlockSpec(memory_space=pl.ANY),
        scratch_shapes=[pltpu.SemaphoreType.DMA(())],
        input_output_aliases={2: 0},
        compiler_params=pltpu.CompilerParams(dimension_semantics=("arbitrary",)),
        name="dispatch",
    )(dest_flat, xp, xs)


def _experts_kernel(sbe_ref, sbs_ref, sbr_ref, nsb_ref, xs_ref, wg_ref, wu_ref, wd_ref, bg_ref, bu_ref,
                    bd_ref, y_ref, xw_ref, xb_ref, acc_ref, wgb_ref, wub_ref, wdb_ref, sem_in, sem_out):
    s = pl.program_id(0)
    f = pl.program_id(1)
    nf = pl.num_programs(1)
    n_sub = SUPER_ROWS // ROW_BLK

    @pl.when(s < nsb_ref[0])
    def _():
        start = pl.multiple_of(sbs_ref[s], ROW_BLK)
        rows = sbr_ref[s]

        @pl.when(f == 0)
        def _():
            cp = pltpu.make_async_copy(xs_ref.at[pl.ds(start, SUPER_ROWS)], xw_ref, sem_in)
            cp.start()
            cp.wait()
            xb_ref[...] = _unpack_bf16_pairs(xw_ref[...])
            acc_ref[...] = jnp.zeros(acc_ref.shape, F32)

        wgb_ref[...] = wg_ref[0].astype(BF16)
        wub_ref[...] = wu_ref[0].astype(BF16)
        wdb_ref[...] = wd_ref[0].astype(BF16)
        for sub in range(n_sub):
            rs = slice(sub * ROW_BLK, (sub + 1) * ROW_BLK)

            @pl.when(sub * ROW_BLK < rows)
            def _():
                x = xb_ref[rs, :]
                gate = _dot(x, wgb_ref[...]) + bg_ref[0]
                up = _dot(x, wub_ref[...]) + bu_ref[0]
                gate = jnp.minimum(gate, SWIGLU_LIMIT)
                up = jnp.clip(up, -SWIGLU_LIMIT, SWIGLU_LIMIT)
                act = (up + 1.0) * gate * _sigmoid(SWIGLU_ALPHA * gate)
                acc_ref[rs, :] += _dot(act.astype(BF16), wdb_ref[...])

        @pl.when(f == nf - 1)
        def _():
            def out_copy(sub):
                rs = pl.ds(sub * ROW_BLK, ROW_BLK)
                return pltpu.make_async_copy(acc_ref.at[rs], y_ref.at[pl.ds(start + sub * ROW_BLK, ROW_BLK)],
                                             sem_out)

            for sub in range(n_sub):
                @pl.when(sub * ROW_BLK < rows)
                def _():
                    rs = slice(sub * ROW_BLK, (sub + 1) * ROW_BLK)
                    acc_ref[rs, :] += bd_ref[0]
                    out_copy(sub).start()

            for sub in range(n_sub):
                @pl.when(sub * ROW_BLK < rows)
                def _():
                    out_copy(sub).wait()

    @pl.when((s == pl.num_programs(0) - 1) & (f == nf - 1))
    def _():
        zero_ref = acc_ref.at[pl.ds(0, ROW_BLK)]
        zero_ref[...] = jnp.zeros(zero_ref.shape, F32)
        n_blocks = y_ref.shape[0] // ROW_BLK

        def tail_copy(b):
            return pltpu.make_async_copy(zero_ref, y_ref.at[pl.ds(pl.multiple_of(b * ROW_BLK, ROW_BLK), ROW_BLK)],
                                         sem_out)

        def issue(b, carry):
            tail_copy(b).start()
            return carry

        def drain(b, carry):
            tail_copy(b).wait()
            return carry

        lax.fori_loop(nsb_ref[1], n_blocks, issue, 0)
        lax.fori_loop(nsb_ref[1], n_blocks, drain, 0)


def _experts(sb_expert, sb_start, sb_rows, n_sb, xs, w_gate_up, b_gate_up, w_down, b_down, n_rows):
    n_super = sb_expert.shape[0]
    nf = D_FF // FF_TILE

    def widx(s, f, sbe, sbs, sbr, nsb):
        live = s < nsb[0]
        return sbe[s], jnp.where(live, f, nf - 1)

    def gate_map(s, f, *pref):
        e, ff = widx(s, f, *pref)
        return (e, 0, ff)

    def up_map(s, f, *pref):
        e, ff = widx(s, f, *pref)
        return (e, 0, nf + ff)

    def down_map(s, f, *pref):
        e, ff = widx(s, f, *pref)
        return (e, ff, 0)

    def bias_map(s, f, *pref):
        return (widx(s, f, *pref)[0], 0, 0)

    b_gu3 = b_gate_up.reshape(N_EXPERTS, 1, 2 * D_FF)
    b_dn3 = b_down.reshape(N_EXPERTS, 1, D_MODEL)
    grid_spec = pltpu.PrefetchScalarGridSpec(
        num_scalar_prefetch=4,
        grid=(n_super, nf),
        in_specs=[pl.BlockSpec(memory_space=pl.ANY),
                  pl.BlockSpec((1, D_MODEL, FF_TILE), gate_map),
                  pl.BlockSpec((1, D_MODEL, FF_TILE), up_map),
                  pl.BlockSpec((1, FF_TILE, D_MODEL), down_map),
                  pl.BlockSpec((1, 1, FF_TILE), gate_map),
                  pl.BlockSpec((1, 1, FF_TILE), up_map),
                  pl.BlockSpec((1, 1, D_MODEL), bias_map)],
        out_specs=pl.BlockSpec(memory_space=pl.ANY),
        scratch_shapes=[pltpu.VMEM((SUPER_ROWS, D_MODEL // 2), U32),
                        pltpu.VMEM((SUPER_ROWS, D_MODEL), BF16),
                        pltpu.VMEM((SUPER_ROWS, D_MODEL), F32),
                        pltpu.VMEM((D_MODEL, FF_TILE), BF16),
                        pltpu.VMEM((D_MODEL, FF_TILE), BF16),
                        pltpu.VMEM((FF_TILE, D_MODEL), BF16),
                        pltpu.SemaphoreType.DMA(()),
                        pltpu.SemaphoreType.DMA(())])
    return pl.pallas_call(
        _experts_kernel,
        out_shape=jax.ShapeDtypeStruct((n_rows, D_MODEL), F32),
        grid_spec=grid_spec,
        compiler_params=pltpu.CompilerParams(dimension_semantics=("arbitrary", "arbitrary"),
                                             vmem_limit_bytes=56 * MIB),
        name="experts",
    )(sb_expert, sb_start, sb_rows, n_sb, xs, w_gate_up, w_gate_up, w_down, b_gu3, b_gu3, b_dn3)


def _combine_kernel(dest_ref, h_ref, route_ref, p_ref, y_ref, gple_ref, wpg_ref, wpp_ref, gfin_ref,
                    o_ref, gbuf_ref, sem):
    def issue(t, carry):
        for k in range(TOP_K):
            _row_copy(y_ref, dest_ref[t * TOP_K + k], gbuf_ref.at[k], t, sem).start()
        return carry

    lax.fori_loop(0, TOK_TM, issue, 0)

    def drain(t, carry):
        for k in range(TOP_K):
            _row_copy(y_ref, 0, gbuf_ref.at[k], 0, sem).wait()
        return carry

    lax.fori_loop(0, TOK_TM, drain, 0)

    route = route_ref[...]
    h = h_ref[...]
    for k in range(TOP_K):
        h = h + route[:, TOP_K + k:TOP_K + k + 1] * gbuf_ref[k]
    xn = (_rms(h) * gple_ref[...]).astype(BF16)
    gate = _sigmoid(_dot(xn, wpg_ref[...]))
    h = h + gate * _dot(p_ref[...].astype(BF16), wpp_ref[...])
    o_ref[...] = _rms(h) * gfin_ref[...]


def _combine(dest_flat, h, route, p, y_sorted, g_ple, wpg, wpp, g_final):
    t = h.shape[0]
    tm = TOK_TM
    rowblk = lambda width: pl.BlockSpec((tm, width), lambda i: (i, 0))
    const = lambda arr: pl.BlockSpec(arr.shape, lambda i: (0,) * arr.ndim)
    return pl.pallas_call(
        _combine_kernel,
        out_shape=jax.ShapeDtypeStruct((t, D_MODEL), F32),
        grid=(t // tm,),
        in_specs=[pl.BlockSpec((tm * TOP_K,), lambda i: (i,), memory_space=pltpu.SMEM),
                  rowblk(D_MODEL), rowblk(LANES), rowblk(PLE_DIM),
                  pl.BlockSpec(memory_space=pl.ANY),
                  const(g_ple), const(wpg), const(wpp), const(g_final)],
        out_specs=rowblk(D_MODEL),
        scratch_shapes=[pltpu.VMEM((TOP_K, tm, D_MODEL), F32), pltpu.SemaphoreType.DMA(())],
        compiler_params=pltpu.CompilerParams(dimension_semantics=("arbitrary",), vmem_limit_bytes=48 * MIB),
        name="combine",
    )(dest_flat, h, route, p, y_sorted, g_ple, wpg, wpp, g_final)


def _super_blocks(counts, n_super):
    padded = (counts + ROW_BLK - 1) // ROW_BLK * ROW_BLK
    pad_start = jnp.cumsum(padded) - padded
    per_expert = (counts + SUPER_ROWS - 1) // SUPER_ROWS
    sb_end = jnp.cumsum(per_expert)
    n_sb = sb_end[-1]
    s = jnp.arange(n_super, dtype=jnp.int32)
    s_live = jnp.minimum(s, jnp.maximum(n_sb - 1, 0))
    expert = jnp.minimum(jnp.searchsorted(sb_end, s_live, side='right'), N_EXPERTS - 1).astype(jnp.int32)
    within = s_live - (sb_end - per_expert)[expert]
    start = pad_start[expert] + within * SUPER_ROWS
    rows = jnp.where(s < n_sb, jnp.clip(counts[expert] - within * SUPER_ROWS, 0, SUPER_ROWS), 0)
    first_unused_block = jnp.sum(padded) // ROW_BLK
    return (pad_start, expert, start.astype(jnp.int32), rows.astype(jnp.int32),
            jnp.stack([n_sb, first_unused_block]).astype(jnp.int32))


def _mixer(x, cache_k, cache_v, st0, conv0, prep, ssd_l):
    bsz, seq, _ = x.shape
    x2 = x.reshape(bsz * seq, D_MODEL)
    q, k, v, kb, vb, z, xbc, ga, gs, dt = _inproj(x2, prep["g_mix"], prep["w_main"], prep["w_dt"])
    if cache_k is None:
        assert bsz == 1
        on = _attn_prompt(q, kb, vb, prep["rel_bias"], prep["lam"], prep["subln"])
    else:
        past = cache_k.shape[1]
        on = _attn_sample(q, kb, vb, cache_k.reshape(bsz, past, ATT_WIDTH), cache_v.reshape(bsz, past, ATT_WIDTH),
                          prep["rel_bias"], prep["lam"], prep["subln"])
    hist8 = jnp.pad(conv0, ((0, 0), (SUBLANES - (CONV_WIDTH - 1), 0), (0, 0)))
    st0_t = st0.reshape(bsz, N_SSM_GROUPS, HEADS_PER_GROUP, SSM_HEAD_DIM, D_STATE)
    st0_t = jnp.transpose(st0_t, (0, 1, 4, 2, 3)).reshape(bsz, N_SSM_GROUPS, D_STATE, GROUP_W)
    yz, st_t = _ssd(xbc, z, dt, hist8, st0_t, prep["conv_w"], prep["conv_b"], prep["dt_bias"], prep["a_log"],
                    prep["dskip_e"], prep["ssm_norm"], bsz, ssd_l)
    st_new = st_t.reshape(bsz, N_SSM_GROUPS, D_STATE, HEADS_PER_GROUP, SSM_HEAD_DIM)
    st_new = jnp.transpose(st_new, (0, 1, 3, 4, 2)).reshape(1, bsz, N_SSM_HEADS, SSM_HEAD_DIM, D_STATE)
    mixed = _mix(on, yz, ga, gs, prep["w_attn_out"], prep["w_ssm_out"])
    k_rows = k.reshape(1, bsz, seq, N_ATT_HEADS, HEAD_W)
    v_rows = v.reshape(1, bsz, seq, N_ATT_HEADS, HEAD_W)
    conv_new = xbc.reshape(bsz, seq, CONV_DIM)[:, seq - (CONV_WIDTH - 1):].reshape(1, bsz, CONV_WIDTH - 1, CONV_DIM)
    return x2, mixed, k_rows, v_rows, st_new, conv_new


def kernel(x_prompt, x_sample, cache_k, cache_v, state_ssm, state_conv, p_prompt, p_sample, rel_bias, w_in,
           lambda_q1, lambda_k1, lambda_q2, lambda_k2, attn_subln, w_attn_out, conv_w, conv_b, dt_bias, a_log,
           d_skip, ssm_norm, w_ssm_out, w_o, g_mix, g_ffn, w_router, b_router, w_gate_up, b_gate_up, w_down,
           b_down, g_ple, w_ple_gate, w_ple_proj, g_final):
    w = w_in[0]
    c_dt = 3 * ATT_WIDTH + SSM_D_INNER + CONV_DIM
    w_main = jnp.concatenate([w[:, :c_dt], w[:, c_dt + N_SSM_HEADS:]], axis=1).astype(BF16)
    w_dt = jnp.pad(w[:, c_dt:c_dt + N_SSM_HEADS], ((0, 0), (0, LANES - N_SSM_HEADS))).astype(BF16)
    lam = (jnp.exp(jnp.sum(lambda_q1[0] * lambda_k1[0]).astype(F32))
           - jnp.exp(jnp.sum(lambda_q2[0] * lambda_k2[0]).astype(F32)) + LAM_INIT)
    pad_heads = lambda v: jnp.pad(v.reshape(1, N_SSM_HEADS), ((0, 0), (0, LANES - N_SSM_HEADS)))
    prep = dict(
        g_mix=g_mix, w_main=w_main, w_dt=w_dt, rel_bias=rel_bias,
        lam=jnp.full((1, HEAD_W), lam, F32), subln=attn_subln,
        conv_w=conv_w[0], conv_b=conv_b, dt_bias=pad_heads(dt_bias[0]), a_log=pad_heads(a_log[0]),
        dskip_e=jnp.repeat(d_skip[0], SSM_HEAD_DIM).reshape(1, SSM_D_INNER), ssm_norm=ssm_norm,
        w_attn_out=w_attn_out[0].astype(BF16), w_ssm_out=w_ssm_out[0].astype(BF16))
    wo = w_o[0].astype(BF16)
    wr = jnp.pad(w_router[0], ((0, 0), (0, LANES - N_EXPERTS))).astype(BF16)
    br = jnp.pad(b_router, ((0, 0), (0, LANES - N_EXPERTS)), constant_values=NEG_INF)
    wpg = w_ple_gate[0].astype(BF16)
    wpp = w_ple_proj[0].astype(BF16)

    bp, sp, _ = x_prompt.shape
    bs, ss, _ = x_sample.shape
    zeros_state = jnp.zeros((bp, N_SSM_HEADS, SSM_HEAD_DIM, D_STATE), F32)
    zeros_conv = jnp.zeros((bp, CONV_WIDTH - 1, CONV_DIM), F32)
    xp2, mixed_p, k_p, v_p, ssm_p, conv_p = _mixer(x_prompt, None, None, zeros_state, zeros_conv, prep, SSD_L_PROMPT)
    xs2, mixed_s, k_s, v_s, ssm_s, conv_s = _mixer(x_sample, cache_k[0], cache_v[0], state_ssm[0], state_conv[0],
                                                   prep, ss)

    zero_counts = jnp.zeros((1, LANES), F32)
    h_p, xpk_p, route_p, cnt_p = _resid(xp2, mixed_p, wo, g_ffn, wr, br, zero_counts)
    h_s, xpk_s, route_s, cnt = _resid(xs2, mixed_s, wo, g_ffn, wr, br, cnt_p)
    n_tok = xp2.shape[0] + xs2.shape[0]
    n_slots = n_tok * TOP_K
    n_super = -(-n_slots // SUPER_ROWS) + N_EXPERTS
    n_rows = n_slots + N_EXPERTS * ROW_BLK
    counts = cnt[0, :N_EXPERTS].astype(jnp.int32)
    pad_start, sb_expert, sb_start, sb_rows, n_sb = _super_blocks(counts, n_super)

    def dest_of(route):
        expert = route[:, :TOP_K].astype(jnp.int32)
        pos = route[:, 2 * TOP_K:3 * TOP_K].astype(jnp.int32)
        return (pad_start[expert] + pos).astype(jnp.int32).reshape(-1)

    dest_p = dest_of(route_p)
    dest_s = dest_of(route_s)
    xs_sorted = jnp.zeros((n_rows + SUPER_ROWS, D_MODEL // 2), U32)
    xs_sorted = _dispatch(dest_p, xpk_p, xs_sorted)
    xs_sorted = _dispatch(dest_s, xpk_s, xs_sorted)
    y_sorted = _experts(sb_expert, sb_start, sb_rows, n_sb, xs_sorted, w_gate_up[0], b_gate_up[0], w_down[0],
                        b_down[0], n_rows)

    y_p = _combine(dest_p, h_p, route_p, p_prompt[0].reshape(-1, PLE_DIM), y_sorted, g_ple, wpg, wpp,
                   g_final.reshape(1, D_MODEL))
    y_s = _combine(dest_s, h_s, route_s, p_sample[0].reshape(-1, PLE_DIM), y_sorted, g_ple, wpg, wpp,
                   g_final.reshape(1, D_MODEL))
    return (y_p.reshape(bp, sp, D_MODEL), y_s.reshape(bs, ss, D_MODEL), k_p, v_p, ssm_p, conv_p,
            k_s, v_s, ssm_s, conv_s)
```

```python
import functools
import math

import numpy as np
import jax
import jax.numpy as jnp
from jax import lax
from jax.experimental import pallas as pl
from jax.experimental.pallas import tpu as pltpu

F32 = jnp.float32
BF16 = jnp.bfloat16
U32 = jnp.uint32

D_MODEL = 2048
CHUNK = 64
N_ATT_HEADS = 8
ATT_HEAD_DIM = 64
HEAD_W = 2 * ATT_HEAD_DIM
ATT_WIDTH = N_ATT_HEADS * HEAD_W
ATT_SCALE = ATT_HEAD_DIM ** -0.5
N_BUCKETS = 32
MAX_DISTANCE = 128
NEG_INF = -1e30
SSM_D_INNER = 2048
SSM_HEAD_DIM = 64
N_SSM_HEADS = SSM_D_INNER // SSM_HEAD_DIM
N_SSM_GROUPS = 4
HEADS_PER_GROUP = N_SSM_HEADS // N_SSM_GROUPS
GROUP_W = HEADS_PER_GROUP * SSM_HEAD_DIM
D_STATE = 128
CONV_WIDTH = 4
CONV_DIM = SSM_D_INNER + 2 * N_SSM_GROUPS * D_STATE
N_EXPERTS = 32
TOP_K = 4
D_FF = 2048
SWIGLU_LIMIT = 7.0
SWIGLU_ALPHA = 1.702
PLE_DIM = 256
EPS = 1e-6
LAM_INIT = 0.8 - 0.6 * math.exp(-0.3 * 0)

LANES = 128
SUBLANES = 8
MIB = 1024 * 1024

INPROJ_TM = 512
INPROJ_TN = 512
ATT_TQ = 256
ATT_QB = 128
SSD_L_PROMPT = 128
SSD_LP = 128
TOK_TM = 256
ROW_BLK = 256
SUPER_ROWS = 1280
FF_TILE = 256


def _dot(a, b):
    return jnp.dot(a, b, preferred_element_type=F32)


def _dot_nt(a, b):
    return lax.dot_general(a, b, (((1,), (1,)), ((), ())), preferred_element_type=F32)


def _rms(x):
    return x * lax.rsqrt(jnp.mean(x * x, axis=-1, keepdims=True) + EPS)


def _sigmoid(x):
    return 1.0 / (1.0 + jnp.exp(-x))


def _split3(x):
    hi = x.astype(BF16)
    r = x - hi.astype(F32)
    mid = r.astype(BF16)
    lo = (r - mid.astype(F32)).astype(BF16)
    return hi, mid, lo


def _xdot_r(x, c):
    hi, mid, lo = _split3(x)
    return (_dot(hi, c) + _dot(mid, c)) + _dot(lo, c)


def _xdot_l(c, x):
    hi, mid, lo = _split3(x)
    return (_dot(c, hi) + _dot(c, mid)) + _dot(c, lo)


_SEG_WIDTHS = (("q", ATT_WIDTH), ("k", ATT_WIDTH), ("v", ATT_WIDTH), ("z", SSM_D_INNER),
               ("xbc", CONV_DIM), ("ga", D_MODEL), ("gs", D_MODEL))


def _segments():
    segs, first = {}, 0
    for name, width in _SEG_WIDTHS:
        assert width % INPROJ_TN == 0
        segs[name] = (first, width // INPROJ_TN)
        first += width // INPROJ_TN
    return segs, first


def _inproj_kernel(x_ref, g_ref, w_ref, wdt_ref, q_ref, k_ref, v_ref, kb_ref, vb_ref, vt_ref, z_ref, xbc_ref,
                   ga_ref, gs_ref, dt_ref, hn_ref):
    j = pl.program_id(1)
    segs, _ = _segments()

    @pl.when(j == 0)
    def _():
        hn = _rms(x_ref[...]) * g_ref[...]
        hn_ref[...] = hn.astype(BF16)
        dt_ref[...] = _dot(hn_ref[...], wdt_ref[...])

    acc = _dot(hn_ref[...], w_ref[...])

    def in_seg(name):
        lo, n = segs[name]
        return (j >= lo) & (j < lo + n)

    @pl.when(in_seg("q"))
    def _():
        q_ref[...] = acc.astype(BF16)

    @pl.when(in_seg("k"))
    def _():
        k_ref[...] = acc
        kb_ref[...] = acc.astype(BF16)

    @pl.when(in_seg("v"))
    def _():
        v_ref[...] = acc
        vb_ref[...] = acc.astype(BF16)
        vt_ref[...] = acc.T.astype(BF16)

    @pl.when(in_seg("z"))
    def _():
        z_ref[...] = acc

    @pl.when(in_seg("xbc"))
    def _():
        xbc_ref[...] = acc

    @pl.when(in_seg("ga"))
    def _():
        ga_ref[...] = acc

    @pl.when(in_seg("gs"))
    def _():
        gs_ref[...] = acc


def _inproj(x, g_mix, w_main, w_dt):
    t = x.shape[0]
    tm, tn = INPROJ_TM, INPROJ_TN
    segs, n_tiles = _segments()
    assert t % tm == 0 and w_main.shape[1] == n_tiles * tn

    def seg_spec(name):
        lo, n = segs[name]
        return pl.BlockSpec((tm, tn), lambda i, j: (i, jnp.clip(j - lo, 0, n - 1)))

    def seg_shape(name, dtype):
        return jax.ShapeDtypeStruct((t, segs[name][1] * tn), dtype)

    v_lo, v_n = segs["v"]
    vt_shape = jax.ShapeDtypeStruct((v_n * tn, t), BF16)
    vt_spec = pl.BlockSpec((tn, tm), lambda i, j: (jnp.clip(j - v_lo, 0, v_n - 1), i))
    out_shape = (seg_shape("q", BF16), seg_shape("k", F32), seg_shape("v", F32), seg_shape("k", BF16),
                 seg_shape("v", BF16), vt_shape, seg_shape("z", F32), seg_shape("xbc", F32), seg_shape("ga", F32),
                 seg_shape("gs", F32), jax.ShapeDtypeStruct((t, LANES), F32))
    out_specs = (seg_spec("q"), seg_spec("k"), seg_spec("v"), seg_spec("k"), seg_spec("v"), vt_spec, seg_spec("z"),
                 seg_spec("xbc"), seg_spec("ga"), seg_spec("gs"), pl.BlockSpec((tm, LANES), lambda i, j: (i, 0)))
    return pl.pallas_call(
        _inproj_kernel,
        out_shape=out_shape,
        grid=(t // tm, n_tiles),
        in_specs=[pl.BlockSpec((tm, D_MODEL), lambda i, j: (i, 0)),
                  pl.BlockSpec((1, D_MODEL), lambda i, j: (0, 0)),
                  pl.BlockSpec((D_MODEL, tn), lambda i, j: (0, j)),
                  pl.BlockSpec((D_MODEL, LANES), lambda i, j: (0, 0))],
        out_specs=out_specs,
        scratch_shapes=[pltpu.VMEM((tm, D_MODEL), BF16)],
        compiler_params=pltpu.CompilerParams(dimension_semantics=("arbitrary", "arbitrary"),
                                             vmem_limit_bytes=48 * MIB),
        name="inproj",
    )(x, g_mix, w_main, w_dt)


def _t5_bucket(rel):
    nb = N_BUCKETS // 2
    max_exact = nb // 2
    ret = jnp.where(rel > 0, nb, 0)
    n = jnp.abs(rel)
    large = max_exact + (jnp.log(jnp.maximum(n, 1).astype(jnp.float32) / max_exact)
                         / math.log(MAX_DISTANCE / max_exact) * (nb - max_exact)).astype(jnp.int32)
    large = jnp.minimum(large, nb - 1)
    return ret + jnp.where(n < max_exact, n, large)


def _rel_bias_table(rel_bias, q_pos, k_pos):
    bucket = _t5_bucket(k_pos[None, :] - q_pos[:, None])
    return jnp.transpose(_lookup(rel_bias.astype(F32).T, bucket), (2, 0, 1))


def _lookup(table, idx):
    n = table.shape[-1]
    hit = idx[..., None] == jnp.arange(n, dtype=idx.dtype)
    hit = hit.reshape(idx.shape + (1,) * (table.ndim - 1) + (n,))
    return jnp.sum(jnp.where(hit, table, jnp.zeros((), table.dtype)), axis=-1)


def _split_maps(qh):
    lane = lax.broadcasted_iota(jnp.int32, qh.shape, 1)
    q1 = jnp.where(lane < ATT_HEAD_DIM, qh, 0.0) * ATT_SCALE
    q2 = jnp.where(lane >= ATT_HEAD_DIM, qh, 0.0) * ATT_SCALE
    return q1.astype(BF16), q2.astype(BF16)


def _subln(o, lam_unused, sub):
    return (_rms(o) * sub) * (1.0 - LAM_INIT)


def _attn_prompt_kernel(qi_ref, kj_ref, q_ref, k_ref, vt_ref, bias_ref, lam_ref, subt_ref, o_ref,
                        qs_ref, m_ref, l_ref, acc_ref):
    s = pl.program_id(0)
    qi = qi_ref[s]
    kj = kj_ref[s]
    tq = q_ref.shape[0]
    tk = k_ref.shape[0]

    @pl.when(kj == 0)
    def _():
        for h in range(N_ATT_HEADS):
            q1, q2 = _split_maps(q_ref[:, h * HEAD_W:(h + 1) * HEAD_W].astype(F32))
            qs_ref[2 * h] = q1
            qs_ref[2 * h + 1] = q2
        m_ref[...] = jnp.full(m_ref.shape, NEG_INF, F32)
        l_ref[...] = jnp.zeros(l_ref.shape, F32)
        acc_ref[...] = jnp.zeros(acc_ref.shape, F32)

    def step(masked):
        if masked:
            key = lax.broadcasted_iota(jnp.int32, (tk, tq), 0)
            qry = lax.broadcasted_iota(jnp.int32, (tk, tq), 1)
            shift = CHUNK.bit_length() - 1
            visible = jnp.right_shift(key, shift) <= jnp.right_shift(qry, shift)
        for h in range(N_ATT_HEADS):
            kh = k_ref[:, h * HEAD_W:(h + 1) * HEAD_W]
            vth = vt_ref[h * HEAD_W:(h + 1) * HEAD_W, :]
            for c in range(2):
                idx = 2 * h + c
                for qb in range(tq // ATT_QB):
                    ql = slice(qb * ATT_QB, (qb + 1) * ATT_QB)
                    sc = _dot_nt(kh, qs_ref[idx, ql, :]) + bias_ref[0, h, :, ql]
                    if masked:
                        sc = jnp.where(visible[:, ql], sc, NEG_INF)
                    m_old = m_ref[idx, :, ql]
                    m_new = jnp.maximum(m_old, jnp.max(sc, axis=0, keepdims=True))
                    alpha = jnp.exp(m_old - m_new)
                    p = jnp.exp(sc - m_new)
                    l_ref[idx, :, ql] = alpha * l_ref[idx, :, ql] + jnp.sum(p, axis=0, keepdims=True)
                    acc_ref[idx, :, ql] = alpha * acc_ref[idx, :, ql] + _dot(vth, p.astype(BF16))
                    m_ref[idx, :, ql] = m_new

    @pl.when(kj == qi)
    def _():
        step(True)

    @pl.when(kj != qi)
    def _():
        step(False)

    @pl.when(kj == qi)
    def _():
        lam = lam_ref[:, :1]
        subt = subt_ref[...]
        for h in range(N_ATT_HEADS):
            ot = acc_ref[2 * h] / l_ref[2 * h] - lam * (acc_ref[2 * h + 1] / l_ref[2 * h + 1])
            ms = jnp.mean(ot * ot, axis=0, keepdims=True)
            ont = ((ot * lax.rsqrt(ms + EPS)) * subt) * (1.0 - LAM_INIT)
            o_ref[:, h * HEAD_W:(h + 1) * HEAD_W] = ont.T.astype(BF16)


def _attn_prompt(q, k, vt, rel_bias, lam_vec, sub):
    t = q.shape[0]
    tq = ATT_TQ
    assert t % tq == 0 and tq % CHUNK == 0 and tq >= MAX_DISTANCE
    nq = t // tq
    qi = np.concatenate([np.full(i + 1, i, np.int32) for i in range(nq)])
    kj = np.concatenate([np.arange(i + 1, dtype=np.int32) for i in range(nq)])
    pos = jnp.arange(tq, dtype=jnp.int32)
    bias = jnp.stack([_rel_bias_table(rel_bias, pos + d * tq, pos) for d in range(3)])
    bias = jnp.transpose(bias, (0, 1, 3, 2))
    sub = sub.reshape(HEAD_W, 1)
    v = vt

    grid_spec = pltpu.PrefetchScalarGridSpec(
        num_scalar_prefetch=2,
        grid=(qi.shape[0],),
        in_specs=[pl.BlockSpec((tq, ATT_WIDTH), lambda s, qi, kj: (qi[s], 0)),
                  pl.BlockSpec((tq, ATT_WIDTH), lambda s, qi, kj: (kj[s], 0)),
                  pl.BlockSpec((ATT_WIDTH, tq), lambda s, qi, kj: (0, kj[s])),
                  pl.BlockSpec((1, N_ATT_HEADS, tq, tq),
                               lambda s, qi, kj: (jnp.minimum(qi[s] - kj[s], 2), 0, 0, 0)),
                  pl.BlockSpec((1, HEAD_W), lambda s, qi, kj: (0, 0)),
                  pl.BlockSpec((HEAD_W, 1), lambda s, qi, kj: (0, 0))],
        out_specs=pl.BlockSpec((tq, ATT_WIDTH), lambda s, qi, kj: (qi[s], 0)),
        scratch_shapes=[pltpu.VMEM((2 * N_ATT_HEADS, tq, HEAD_W), BF16),
                        pltpu.VMEM((2 * N_ATT_HEADS, 1, tq), F32),
                        pltpu.VMEM((2 * N_ATT_HEADS, 1, tq), F32),
                        pltpu.VMEM((2 * N_ATT_HEADS, HEAD_W, tq), F32)])
    return pl.pallas_call(
        _attn_prompt_kernel,
        out_shape=jax.ShapeDtypeStruct((t, ATT_WIDTH), BF16),
        grid_spec=grid_spec,
        compiler_params=pltpu.CompilerParams(dimension_semantics=("arbitrary",), vmem_limit_bytes=40 * MIB),
        name="attn_prompt",
    )(jnp.asarray(qi), jnp.asarray(kj), q, k, v, bias, lam_vec, sub)


def _attn_sample_kernel(q_ref, kn_ref, vn_ref, ck_ref, cv_ref, bc_ref, bn_ref, mc_ref, mn_ref, lam_ref,
                        sub_ref, o_ref):
    lam = lam_ref[...]
    sub = sub_ref[...]
    vis_c = mc_ref[...] > 0.5
    vis_n = mn_ref[...] > 0.5
    for h in range(N_ATT_HEADS):
        hs = slice(h * HEAD_W, (h + 1) * HEAD_W)
        qmaps = _split_maps(q_ref[:, hs].astype(F32))
        kc = ck_ref[:, hs].astype(BF16)
        vc = cv_ref[:, hs].astype(BF16)
        kn = kn_ref[:, hs]
        vn = vn_ref[:, hs]
        probs = []
        for c in range(2):
            sc = jnp.where(vis_c, _dot_nt(qmaps[c], kc) + bc_ref[h], NEG_INF)
            sn = jnp.where(vis_n, _dot_nt(qmaps[c], kn) + bn_ref[h], NEG_INF)
            m = jnp.maximum(jnp.max(sc, axis=-1, keepdims=True), jnp.max(sn, axis=-1, keepdims=True))
            pc = jnp.exp(sc - m)
            pn = jnp.exp(sn - m)
            den = jnp.sum(pc, axis=-1, keepdims=True) + jnp.sum(pn, axis=-1, keepdims=True)
            probs.append((pc / den, pn / den))
        wc = probs[0][0] - lam[:, :1] * probs[1][0]
        wn = probs[0][1] - lam[:, :1] * probs[1][1]
        o = _dot(wc.astype(BF16), vc) + _dot(wn.astype(BF16), vn)
        o_ref[:, hs] = _subln(o, None, sub).astype(BF16)


def _attn_sample(q, kn, vn, cache_k, cache_v, rel_bias, lam_vec, sub):
    bsz, past, _ = cache_k.shape
    seq = q.shape[0] // bsz
    q_pos = past + jnp.arange(seq, dtype=jnp.int32)
    k_pos = jnp.arange(past + seq, dtype=jnp.int32)
    bias = _rel_bias_table(rel_bias, q_pos, k_pos)
    visible = ((k_pos[None, :] // CHUNK) <= (q_pos[:, None] // CHUNK)).astype(F32)
    const = lambda *shape: pl.BlockSpec(shape, lambda b: (0,) * len(shape))
    row = pl.BlockSpec((seq, ATT_WIDTH), lambda b: (b, 0))
    cache = pl.BlockSpec((None, past, ATT_WIDTH), lambda b: (b, 0, 0))
    return pl.pallas_call(
        _attn_sample_kernel,
        out_shape=jax.ShapeDtypeStruct(q.shape, BF16),
        grid=(bsz,),
        in_specs=[row, row, row, cache, cache, const(N_ATT_HEADS, seq, past), const(N_ATT_HEADS, seq, seq),
                  const(seq, past), const(seq, seq), const(1, HEAD_W), const(1, HEAD_W)],
        out_specs=row,
        compiler_params=pltpu.CompilerParams(dimension_semantics=("arbitrary",), vmem_limit_bytes=40 * MIB),
        name="attn_sample",
    )(q, kn, vn, cache_k, cache_v, bias[:, :, :past], bias[:, :, past:], visible[:, :past], visible[:, past:],
      lam_vec, sub)


def _ssd_constants(l):
    hl = HEADS_PER_GROUP * l
    lp = SSD_LP
    e_head = np.zeros((LANES, SSM_D_INNER), np.float32)
    for h in range(N_SSM_HEADS):
        e_head[h, h * SSM_HEAD_DIM:(h + 1) * SSM_HEAD_DIM] = 1.0
    e_grp = np.zeros((N_SSM_GROUPS, LANES, hl), np.float32)
    for g in range(N_SSM_GROUPS):
        for r in range(HEADS_PER_GROUP):
            e_grp[g, g * HEADS_PER_GROUP + r, r * l:(r + 1) * l] = 1.0
    tile8 = np.zeros((lp, hl), np.float32)
    for r in range(HEADS_PER_GROUP):
        tile8[np.arange(l), r * l + np.arange(l)] = 1.0
    causal = np.zeros((l, hl), np.float32)
    for r in range(HEADS_PER_GROUP):
        causal[:, r * l:(r + 1) * l] = np.tril(np.ones((l, l), np.float32))
    tri = np.tril(np.ones((l, l), np.float32))
    ones = np.ones((l, l), np.float32)
    bmask = np.zeros((hl, GROUP_W), np.float32)
    for r in range(HEADS_PER_GROUP):
        bmask[r * l:(r + 1) * l, r * SSM_HEAD_DIM:(r + 1) * SSM_HEAD_DIM] = 1.0
    as_bf = lambda a: jnp.asarray(a, BF16)
    return (as_bf(e_head), as_bf(e_grp), as_bf(tile8), jnp.asarray(causal), as_bf(tri), as_bf(ones),
            as_bf(bmask))


def _ssd_kernel(xbc_ref, z_ref, dt_ref, hist_ref, st0_ref, cw_ref, cb_ref, dtb_ref, alog_ref, dskip_ref,
                norm_ref, eh_ref, eg_ref, t8_ref, caus_ref, tri_ref, ones_ref, bmask_ref,
                yz_ref, st_ref, buf_ref, state_ref):
    c = pl.program_id(1)
    l = xbc_ref.shape[0]
    lp = SSD_LP
    hist_rows = hist_ref.shape[0]

    @pl.when(c == 0)
    def _():
        buf_ref[0:hist_rows, :] = hist_ref[...]
        state_ref[...] = st0_ref[...]

    u = xbc_ref[...]
    buf_ref[hist_rows:hist_rows + l, :] = u
    conv = cb_ref[...] + cw_ref[CONV_WIDTH - 1:CONV_WIDTH, :] * u
    for w in range(CONV_WIDTH - 1):
        shift = CONV_WIDTH - 1 - w
        conv = conv + cw_ref[w:w + 1, :] * buf_ref[hist_rows - shift:hist_rows - shift + l, :]
    buf_ref[0:hist_rows, :] = buf_ref[l:l + hist_rows, :]
    xc = conv * _sigmoid(conv)
    xs = xc[:, :SSM_D_INNER]
    bm = xc[:, SSM_D_INNER:SSM_D_INNER + N_SSM_GROUPS * D_STATE]
    cm = xc[:, SSM_D_INNER + N_SSM_GROUPS * D_STATE:]

    dt_in = dt_ref[...] + dtb_ref[...]
    dt = jnp.maximum(dt_in, 0.0) + jnp.log(1.0 + jnp.exp(-jnp.abs(dt_in)))
    a = -jnp.exp(alog_ref[...])
    acum = _xdot_l(tri_ref[...], dt * a)
    eh = eh_ref[...]
    dt_e = _xdot_r(dt, eh)
    ac_e = _xdot_r(acum, eh)
    a_last = ac_e[l - 1:l, :]
    ecum = jnp.exp(ac_e)
    xdt = xs * dt_e
    xdtw_b = (xdt * jnp.exp(a_last - ac_e)).astype(BF16)
    xdt_b = xdt.astype(BF16)
    drow = jnp.exp(a_last)
    z = z_ref[...]
    caus = caus_ref[...] > 0.5
    t8 = t8_ref[...]
    t8_mask = t8[0:l, :] > 0
    bmask = bmask_ref[...] > 0
    row_pad = lp - l

    for g in range(N_SSM_GROUPS):
        gs = slice(g * GROUP_W, (g + 1) * GROUP_W)
        ns = slice(g * D_STATE, (g + 1) * D_STATE)
        bm_g = bm[:, ns]
        cm_b = cm[:, ns].astype(BF16)
        xw_g = xdtw_b[:, gs]
        if row_pad:
            bm_g = jnp.concatenate([bm_g, jnp.zeros((row_pad, D_STATE), F32)], axis=0)
            xw_g = jnp.concatenate([xw_g, jnp.zeros((row_pad, GROUP_W), BF16)], axis=0)
        bmt_b = bm_g.T.astype(BF16)
        cb8 = _dot(cm_b, _dot(bmt_b, t8).astype(BF16))
        a1 = _xdot_r(acum, eg_ref[g])
        a2 = _xdot_l(ones_ref[...], jnp.where(t8_mask, a1, 0.0))
        decay = jnp.where(caus, jnp.exp(jnp.where(caus, a1 - a2, 0.0)), 0.0)
        m_b = (cb8 * decay).astype(BF16)
        xg = xdt_b[:, gs]
        bd = jnp.concatenate([xg] * HEADS_PER_GROUP, axis=0)
        bd = jnp.where(bmask, bd, jnp.zeros_like(bd))
        y = _dot(m_b, bd)
        st_g = state_ref[g]
        y = y + _dot(cm_b, st_g.astype(BF16)) * ecum[:, gs]
        y = y + dskip_ref[:, gs] * xs[:, gs]
        state_ref[g] = st_g * drow[:, gs] + _dot(bmt_b, xw_g)
        zg = z[:, gs]
        yz = y * (zg * _sigmoid(zg))
        yz_ref[:, gs] = (_rms(yz) * norm_ref[:, gs]).astype(BF16)

    @pl.when(c == pl.num_programs(1) - 1)
    def _():
        st_ref[...] = state_ref[...]


def _ssd(xbc, z, dt, hist8, st0, conv_w, conv_b, dt_bias, a_log, dskip_e, ssm_norm, bsz, l):
    rows = xbc.shape[0]
    seq = rows // bsz
    assert seq % l == 0 and l % SUBLANES == 0 and l <= SSD_LP and l >= SUBLANES
    nc = seq // l
    consts = _ssd_constants(l)
    rowblk = lambda width: pl.BlockSpec((l, width), lambda b, c: (b * nc + c, 0))
    const = lambda arr: pl.BlockSpec(arr.shape, lambda b, c: (0,) * arr.ndim)
    params = (conv_w, conv_b, dt_bias, a_log, dskip_e, ssm_norm)
    return pl.pallas_call(
        _ssd_kernel,
        out_shape=(jax.ShapeDtypeStruct((rows, SSM_D_INNER), BF16),
                   jax.ShapeDtypeStruct(st0.shape, F32)),
        grid=(bsz, nc),
        in_specs=[rowblk(CONV_DIM), rowblk(SSM_D_INNER), rowblk(LANES),
                  pl.BlockSpec((None,) + hist8.shape[1:], lambda b, c: (b, 0, 0)),
                  pl.BlockSpec((None,) + st0.shape[1:], lambda b, c: (b, 0, 0, 0))]
                 + [const(p) for p in params] + [const(k) for k in consts],
        out_specs=(rowblk(SSM_D_INNER), pl.BlockSpec((None,) + st0.shape[1:], lambda b, c: (b, 0, 0, 0))),
        scratch_shapes=[pltpu.VMEM((SUBLANES + l, CONV_DIM), F32),
                        pltpu.VMEM(st0.shape[1:], F32)],
        compiler_params=pltpu.CompilerParams(dimension_semantics=("arbitrary", "arbitrary"),
                                             vmem_limit_bytes=48 * MIB),
        name="ssd",
    )(xbc, z, dt, hist8, st0, *params, *consts)


def _mix_kernel(on_ref, yz_ref, ga_ref, gs_ref, wa_ref, ws_ref, o_ref):
    att = _dot(on_ref[...], wa_ref[...])
    ssm = _dot(yz_ref[...], ws_ref[...])
    o_ref[...] = (_sigmoid(ga_ref[...]) * att + _sigmoid(gs_ref[...]) * ssm).astype(BF16)


def _mix(on, yz, ga, gs, wa, ws):
    t = on.shape[0]
    tm = TOK_TM
    rowblk = lambda width: pl.BlockSpec((tm, width), lambda i: (i, 0))
    const = lambda arr: pl.BlockSpec(arr.shape, lambda i: (0,) * arr.ndim)
    return pl.pallas_call(
        _mix_kernel,
        out_shape=jax.ShapeDtypeStruct((t, D_MODEL), BF16),
        grid=(t // tm,),
        in_specs=[rowblk(ATT_WIDTH), rowblk(SSM_D_INNER), rowblk(D_MODEL), rowblk(D_MODEL), const(wa), const(ws)],
        out_specs=rowblk(D_MODEL),
        compiler_params=pltpu.CompilerParams(dimension_semantics=("arbitrary",), vmem_limit_bytes=48 * MIB),
        name="mix",
    )(on, yz, ga, gs, wa, ws)


def _pack_bf16_pairs(x):
    w = x.shape[1] // 2
    lo = lax.bitcast_convert_type(x[:, :w].astype(BF16).astype(F32), U32)
    hi = lax.bitcast_convert_type(x[:, w:].astype(BF16).astype(F32), U32)
    return hi | (lo >> 16)


def _unpack_bf16_pairs(words):
    lo = lax.bitcast_convert_type(words << 16, F32)
    hi = lax.bitcast_convert_type(words & jnp.uint32(0xFFFF0000), F32)
    return jnp.concatenate([lo, hi], axis=1).astype(BF16)


def _resid_kernel(x_ref, mixed_ref, wo_ref, g_ref, wr_ref, br_ref, lt_ref, cin_ref,
                  h_ref, xp_ref, route_ref, cnt_ref, carry_ref):
    i = pl.program_id(0)

    @pl.when(i == 0)
    def _():
        carry_ref[...] = cin_ref[...]

    h = x_ref[...] + _dot(mixed_ref[...], wo_ref[...])
    h_ref[...] = h
    xn = _rms(h) * g_ref[...]
    xp_ref[...] = _pack_bf16_pairs(xn)
    logits = _dot(xn.astype(BF16), wr_ref[...]) + br_ref[...]

    lane = lax.broadcasted_iota(jnp.int32, logits.shape, 1)
    lane_f = lane.astype(F32)
    rest = logits
    vals, idxs, sels = [], [], []
    for _ in range(TOP_K):
        m = jnp.max(rest, axis=-1, keepdims=True)
        idx = jnp.min(jnp.where(rest == m, lane_f, float(LANES)), axis=-1, keepdims=True)
        sel = lane_f == idx
        rest = jnp.where(sel, -jnp.inf, rest)
        vals.append(m)
        idxs.append(idx)
        sels.append(sel)
    exps = [jnp.exp(v - vals[0]) for v in vals]
    den = exps[0] + exps[1] + exps[2] + exps[3]
    onehot = jnp.zeros(logits.shape, F32)
    for sel in sels:
        onehot = onehot + jnp.where(sel, 1.0, 0.0)
    before = _dot(lt_ref[...], onehot.astype(BF16)) + carry_ref[...]
    route = jnp.zeros(logits.shape, F32)
    for k in range(TOP_K):
        pos = jnp.sum(jnp.where(sels[k], before, 0.0), axis=-1, keepdims=True)
        route = route + jnp.where(lane == k, idxs[k], 0.0)
        route = route + jnp.where(lane == TOP_K + k, exps[k] / den, 0.0)
        route = route + jnp.where(lane == 2 * TOP_K + k, pos, 0.0)
    route_ref[...] = route
    carry_ref[...] = carry_ref[...] + jnp.sum(onehot, axis=0, keepdims=True)
    cnt_ref[...] = carry_ref[...]


def _resid(x, mixed, wo, g_ffn, wr, br, counts_in):
    t = x.shape[0]
    tm = TOK_TM
    lt = jnp.asarray(np.tril(np.ones((tm, tm), np.float32), -1), BF16)
    rowblk = lambda width: pl.BlockSpec((tm, width), lambda i: (i, 0))
    const = lambda arr: pl.BlockSpec(arr.shape, lambda i: (0,) * arr.ndim)
    return pl.pallas_call(
        _resid_kernel,
        out_shape=(jax.ShapeDtypeStruct((t, D_MODEL), F32),
                   jax.ShapeDtypeStruct((t, D_MODEL // 2), U32),
                   jax.ShapeDtypeStruct((t, LANES), F32),
                   jax.ShapeDtypeStruct((1, LANES), F32)),
        grid=(t // tm,),
        in_specs=[rowblk(D_MODEL), rowblk(D_MODEL), const(wo), const(g_ffn), const(wr), const(br), const(lt),
                  const(counts_in)],
        out_specs=(rowblk(D_MODEL), rowblk(D_MODEL // 2), rowblk(LANES), pl.BlockSpec((1, LANES), lambda i: (0, 0))),
        scratch_shapes=[pltpu.VMEM((1, LANES), F32)],
        compiler_params=pltpu.CompilerParams(dimension_semantics=("arbitrary",), vmem_limit_bytes=48 * MIB),
        name="resid_route",
    )(x, mixed, wo, g_ffn, wr, br, lt, counts_in)


def _row_copy(src, src_row, dst, dst_row, sem):
    return pltpu.make_async_copy(src.at[pl.ds(src_row, 1)], dst.at[pl.ds(dst_row, 1)], sem)


def _dispatch_kernel(dest_ref, xp_ref, xs_in_ref, xs_ref, sem):
    del xs_in_ref

    def issue(t, carry):
        for k in range(TOP_K):
            _row_copy(xp_ref, t, xs_ref, dest_ref[t * TOP_K + k], sem).start()
        return carry

    lax.fori_loop(0, TOK_TM, issue, 0)

    def drain(t, carry):
        for k in range(TOP_K):
            _row_copy(xp_ref, 0, xs_ref, 0, sem).wait()
        return carry

    lax.fori_loop(0, TOK_TM, drain, 0)


def _dispatch(dest_flat, xp, xs):
    t = xp.shape[0]
    return pl.pallas_call(
        _dispatch_kernel,
        out_shape=jax.ShapeDtypeStruct(xs.shape, xs.dtype),
        grid=(t // TOK_TM,),
        in_specs=[pl.BlockSpec((TOK_TM * TOP_K,), lambda i: (i,), memory_space=pltpu.SMEM),
                  pl.BlockSpec((TOK_TM, xp.shape[1]), lambda i: (i, 0)),
                  pl.BlockSpec(memory_space=pl.ANY)],
        out_specs=pl.BlockSpec(memory_space=pl.ANY),
        scratch_shapes=[pltpu.SemaphoreType.DMA(())],
        input_output_aliases={2: 0},
        compiler_params=pltpu.CompilerParams(dimension_semantics=("arbitrary",)),
        name="dispatch",
    )(dest_flat, xp, xs)


def _experts_kernel(sbe_ref, sbs_ref, sbr_ref, nsb_ref, xs_ref, wg_ref, wu_ref, wd_ref, bg_ref, bu_ref,
                    bd_ref, y_ref, xw_ref, xb_ref, acc_ref, wgb_ref, wub_ref, wdb_ref, sem_in, sem_out):
    s = pl.program_id(0)
    f = pl.program_id(1)
    nf = pl.num_programs(1)
    n_sub = SUPER_ROWS // ROW_BLK

    @pl.when(s < nsb_ref[0])
    def _():
        start = pl.multiple_of(sbs_ref[s], ROW_BLK)
        rows = sbr_ref[s]

        def x_copy(sb, slot):
            src = xs_ref.at[pl.ds(pl.multiple_of(sbs_ref[sb], ROW_BLK), SUPER_ROWS)]
            return pltpu.make_async_copy(src, xw_ref.at[slot], sem_in.at[slot])

        @pl.when(f == 0)
        def _():
            slot = s % 2

            @pl.when(s == 0)
            def _():
                x_copy(0, 0).start()

            x_copy(s, slot).wait()

            @pl.when(s + 1 < nsb_ref[0])
            def _():
                x_copy(s + 1, 1 - slot).start()

            xb_ref[...] = _unpack_bf16_pairs(xw_ref[slot])
            acc_ref[...] = jnp.zeros(acc_ref.shape, F32)

        wgb_ref[...] = wg_ref[0].astype(BF16)
        wub_ref[...] = wu_ref[0].astype(BF16)
        wdb_ref[...] = wd_ref[0].astype(BF16)
        x = xb_ref[...]
        gate = _dot(x, wgb_ref[...]) + bg_ref[0]
        up = _dot(x, wub_ref[...]) + bu_ref[0]
        gate = jnp.minimum(gate, SWIGLU_LIMIT)
        up = jnp.clip(up, -SWIGLU_LIMIT, SWIGLU_LIMIT)
        act = (up + 1.0) * gate * _sigmoid(SWIGLU_ALPHA * gate)
        acc_ref[...] += _dot(act.astype(BF16), wdb_ref[...])

        @pl.when(f == nf - 1)
        def _():
            def out_copy(sub):
                rs = pl.ds(sub * ROW_BLK, ROW_BLK)
                return pltpu.make_async_copy(acc_ref.at[rs], y_ref.at[pl.ds(start + sub * ROW_BLK, ROW_BLK)],
                                             sem_out)

            for sub in range(n_sub):
                @pl.when(sub * ROW_BLK < rows)
                def _():
                    rs = slice(sub * ROW_BLK, (sub + 1) * ROW_BLK)
                    acc_ref[rs, :] += bd_ref[0]
                    out_copy(sub).start()

            for sub in range(n_sub):
                @pl.when(sub * ROW_BLK < rows)
                def _():
                    out_copy(sub).wait()

    @pl.when((s == pl.num_programs(0) - 1) & (f == nf - 1))
    def _():
        zero_ref = acc_ref.at[pl.ds(0, ROW_BLK)]
        zero_ref[...] = jnp.zeros(zero_ref.shape, F32)
        n_blocks = y_ref.shape[0] // ROW_BLK

        def tail_copy(b):
            return pltpu.make_async_copy(zero_ref, y_ref.at[pl.ds(pl.multiple_of(b * ROW_BLK, ROW_BLK), ROW_BLK)],
                                         sem_out)

        def issue(b, carry):
            tail_copy(b).start()
            return carry

        def drain(b, carry):
            tail_copy(b).wait()
            return carry

        lax.fori_loop(nsb_ref[1], n_blocks, issue, 0)
        lax.fori_loop(nsb_ref[1], n_blocks, drain, 0)


def _experts(sb_expert, sb_start, sb_rows, n_sb, xs, w_gate_up, b_gate_up, w_down, b_down, n_rows):
    n_super = sb_expert.shape[0]
    nf = D_FF // FF_TILE

    def widx(s, f, sbe, sbs, sbr, nsb):
        live = s < nsb[0]
        return sbe[s], jnp.where(live, f, nf - 1)

    def gate_map(s, f, *pref):
        e, ff = widx(s, f, *pref)
        return (e, 0, ff)

    def up_map(s, f, *pref):
        e, ff = widx(s, f, *pref)
        return (e, 0, nf + ff)

    def down_map(s, f, *pref):
        e, ff = widx(s, f, *pref)
        return (e, ff, 0)

    def bias_map(s, f, *pref):
        return (widx(s, f, *pref)[0], 0, 0)

    b_gu3 = b_gate_up.reshape(N_EXPERTS, 1, 2 * D_FF)
    b_dn3 = b_down.reshape(N_EXPERTS, 1, D_MODEL)
    grid_spec = pltpu.PrefetchScalarGridSpec(
        num_scalar_prefetch=4,
        grid=(n_super, nf),
        in_specs=[pl.BlockSpec(memory_space=pl.ANY),
                  pl.BlockSpec((1, D_MODEL, FF_TILE), gate_map),
                  pl.BlockSpec((1, D_MODEL, FF_TILE), up_map),
                  pl.BlockSpec((1, FF_TILE, D_MODEL), down_map),
                  pl.BlockSpec((1, 1, FF_TILE), gate_map),
                  pl.BlockSpec((1, 1, FF_TILE), up_map),
                  pl.BlockSpec((1, 1, D_MODEL), bias_map)],
        out_specs=pl.BlockSpec(memory_space=pl.ANY),
        scratch_shapes=[pltpu.VMEM((2, SUPER_ROWS, D_MODEL // 2), U32),
                        pltpu.VMEM((SUPER_ROWS, D_MODEL), BF16),
                        pltpu.VMEM((SUPER_ROWS, D_MODEL), F32),
                        pltpu.VMEM((D_MODEL, FF_TILE), BF16),
                        pltpu.VMEM((D_MODEL, FF_TILE), BF16),
                        pltpu.VMEM((FF_TILE, D_MODEL), BF16),
                        pltpu.SemaphoreType.DMA((2,)),
                        pltpu.SemaphoreType.DMA(())])
    return pl.pallas_call(
        _experts_kernel,
        out_shape=jax.ShapeDtypeStruct((n_rows, D_MODEL), F32),
        grid_spec=grid_spec,
        compiler_params=pltpu.CompilerParams(dimension_semantics=("arbitrary", "arbitrary"),
                                             vmem_limit_bytes=56 * MIB),
        name="experts",
    )(sb_expert, sb_start, sb_rows, n_sb, xs, w_gate_up, w_gate_up, w_down, b_gu3, b_gu3, b_dn3)


def _combine_kernel(dest_ref, dest_next_ref, h_ref, route_ref, p_ref, y_ref, gple_ref, wpg_ref, wpp_ref,
                    gfin_ref, o_ref, gbuf_ref, sem, *, n_tiles):
    i = pl.program_id(0)
    slot = i % 2

    def gather_tile(d_ref, into):
        def issue(t, carry):
            for k in range(TOP_K):
                _row_copy(y_ref, d_ref[t * TOP_K + k], gbuf_ref.at[into].at[k], t, sem.at[into]).start()
            return carry

        lax.fori_loop(0, TOK_TM, issue, 0)

    @pl.when(i == 0)
    def _():
        gather_tile(dest_ref, 0)

    @pl.when(i + 1 < n_tiles)
    def _():
        gather_tile(dest_next_ref, 1 - slot)

    def drain(t, carry):
        for k in range(TOP_K):
            _row_copy(y_ref, 0, gbuf_ref.at[slot].at[k], 0, sem.at[slot]).wait()
        return carry

    lax.fori_loop(0, TOK_TM, drain, 0)

    route = route_ref[...]
    h = h_ref[...]
    for k in range(TOP_K):
        h = h + route[:, TOP_K + k:TOP_K + k + 1] * gbuf_ref[slot, k]
    xn = (_rms(h) * gple_ref[...]).astype(BF16)
    gate = _sigmoid(_dot(xn, wpg_ref[...]))
    h = h + gate * _dot(p_ref[...].astype(BF16), wpp_ref[...])
    o_ref[...] = _rms(h) * gfin_ref[...]


def _combine(dest_flat, h, route, p, y_sorted, g_ple, wpg, wpp, g_final):
    t = h.shape[0]
    tm = TOK_TM
    rowblk = lambda width: pl.BlockSpec((tm, width), lambda i: (i, 0))
    const = lambda arr: pl.BlockSpec(arr.shape, lambda i: (0,) * arr.ndim)
    return pl.pallas_call(
        functools.partial(_combine_kernel, n_tiles=t // tm),
        out_shape=jax.ShapeDtypeStruct((t, D_MODEL), F32),
        grid=(t // tm,),
        in_specs=[pl.BlockSpec((tm * TOP_K,), lambda i: (i,), memory_space=pltpu.SMEM),
                  pl.BlockSpec((tm * TOP_K,), lambda i: (jnp.minimum(i + 1, t // tm - 1),),
                               memory_space=pltpu.SMEM),
                  rowblk(D_MODEL), rowblk(LANES), rowblk(PLE_DIM),
                  pl.BlockSpec(memory_space=pl.ANY),
                  const(g_ple), const(wpg), const(wpp), const(g_final)],
        out_specs=rowblk(D_MODEL),
        scratch_shapes=[pltpu.VMEM((2, TOP_K, tm, D_MODEL), F32), pltpu.SemaphoreType.DMA((2,))],
        compiler_params=pltpu.CompilerParams(dimension_semantics=("arbitrary",), vmem_limit_bytes=52 * MIB),
        name="combine",
    )(dest_flat, dest_flat, h, route, p, y_sorted, g_ple, wpg, wpp, g_final)


def _super_blocks(counts, n_super):
    padded = (counts + ROW_BLK - 1) // ROW_BLK * ROW_BLK
    pad_start = jnp.cumsum(padded) - padded
    per_expert = (counts + SUPER_ROWS - 1) // SUPER_ROWS
    sb_end = jnp.cumsum(per_expert)
    n_sb = sb_end[-1]
    s = jnp.arange(n_super, dtype=jnp.int32)
    s_live = jnp.minimum(s, jnp.maximum(n_sb - 1, 0))
    expert = jnp.sum(sb_end[None, :] <= s_live[:, None], axis=1).astype(jnp.int32)
    expert = jnp.minimum(expert, N_EXPERTS - 1)
    within = s_live - _lookup(sb_end - per_expert, expert)
    start = _lookup(pad_start, expert) + within * SUPER_ROWS
    rows = jnp.where(s < n_sb, jnp.clip(_lookup(counts, expert) - within * SUPER_ROWS, 0, SUPER_ROWS), 0)
    first_unused_block = jnp.sum(padded) // ROW_BLK
    return (pad_start, expert, start.astype(jnp.int32), rows.astype(jnp.int32),
            jnp.stack([n_sb, first_unused_block]).astype(jnp.int32))


def _mixer(x, cache_k, cache_v, st0, conv0, prep, ssd_l):
    bsz, seq, _ = x.shape
    x2 = x.reshape(bsz * seq, D_MODEL)
    q, k, v, kb, vb, vt, z, xbc, ga, gs, dt = _inproj(x2, prep["g_mix"], prep["w_main"], prep["w_dt"])
    if cache_k is None:
        assert bsz == 1
        on = _attn_prompt(q, kb, vt, prep["rel_bias"], prep["lam"], prep["subln"])
    else:
        past = cache_k.shape[1]
        on = _attn_sample(q, kb, vb, cache_k.reshape(bsz, past, ATT_WIDTH), cache_v.reshape(bsz, past, ATT_WIDTH),
                          prep["rel_bias"], prep["lam"], prep["subln"])
    hist8 = jnp.pad(conv0, ((0, 0), (SUBLANES - (CONV_WIDTH - 1), 0), (0, 0)))
    st0_t = st0.reshape(bsz, N_SSM_GROUPS, HEADS_PER_GROUP, SSM_HEAD_DIM, D_STATE)
    st0_t = jnp.transpose(st0_t, (0, 1, 4, 2, 3)).reshape(bsz, N_SSM_GROUPS, D_STATE, GROUP_W)
    yz, st_t = _ssd(xbc, z, dt, hist8, st0_t, prep["conv_w"], prep["conv_b"], prep["dt_bias"], prep["a_log"],
                    prep["dskip_e"], prep["ssm_norm"], bsz, ssd_l)
    st_new = st_t.reshape(bsz, N_SSM_GROUPS, D_STATE, HEADS_PER_GROUP, SSM_HEAD_DIM)
    st_new = jnp.transpose(st_new, (0, 1, 3, 4, 2)).reshape(1, bsz, N_SSM_HEADS, SSM_HEAD_DIM, D_STATE)
    mixed = _mix(on, yz, ga, gs, prep["w_attn_out"], prep["w_ssm_out"])
    k_rows = k.reshape(1, bsz, seq, N_ATT_HEADS, HEAD_W)
    v_rows = v.reshape(1, bsz, seq, N_ATT_HEADS, HEAD_W)
    conv_new = xbc.reshape(bsz, seq, CONV_DIM)[:, seq - (CONV_WIDTH - 1):].reshape(1, bsz, CONV_WIDTH - 1, CONV_DIM)
    return x2, mixed, k_rows, v_rows, st_new, conv_new


def kernel(x_prompt, x_sample, cache_k, cache_v, state_ssm, state_conv, p_prompt, p_sample, rel_bias, w_in,
           lambda_q1, lambda_k1, lambda_q2, lambda_k2, attn_subln, w_attn_out, conv_w, conv_b, dt_bias, a_log,
           d_skip, ssm_norm, w_ssm_out, w_o, g_mix, g_ffn, w_router, b_router, w_gate_up, b_gate_up, w_down,
           b_down, g_ple, w_ple_gate, w_ple_proj, g_final):
    w = w_in[0]
    c_dt = 3 * ATT_WIDTH + SSM_D_INNER + CONV_DIM
    w_main = jnp.concatenate([w[:, :c_dt], w[:, c_dt + N_SSM_HEADS:]], axis=1).astype(BF16)
    w_dt = jnp.pad(w[:, c_dt:c_dt + N_SSM_HEADS], ((0, 0), (0, LANES - N_SSM_HEADS))).astype(BF16)
    lam = (jnp.exp(jnp.sum(lambda_q1[0] * lambda_k1[0]).astype(F32))
           - jnp.exp(jnp.sum(lambda_q2[0] * lambda_k2[0]).astype(F32)) + LAM_INIT)
    pad_heads = lambda v: jnp.pad(v.reshape(1, N_SSM_HEADS), ((0, 0), (0, LANES - N_SSM_HEADS)))
    prep = dict(
        g_mix=g_mix, w_main=w_main, w_dt=w_dt, rel_bias=rel_bias,
        lam=jnp.full((1, HEAD_W), lam, F32), subln=attn_subln,
        conv_w=conv_w[0], conv_b=conv_b, dt_bias=pad_heads(dt_bias[0]), a_log=pad_heads(a_log[0]),
        dskip_e=jnp.repeat(d_skip[0], SSM_HEAD_DIM).reshape(1, SSM_D_INNER), ssm_norm=ssm_norm,
        w_attn_out=w_attn_out[0].astype(BF16), w_ssm_out=w_ssm_out[0].astype(BF16))
    wo = w_o[0].astype(BF16)
    wr = jnp.pad(w_router[0], ((0, 0), (0, LANES - N_EXPERTS))).astype(BF16)
    br = jnp.pad(b_router, ((0, 0), (0, LANES - N_EXPERTS)), constant_values=NEG_INF)
    wpg = w_ple_gate[0].astype(BF16)
    wpp = w_ple_proj[0].astype(BF16)

    bp, sp, _ = x_prompt.shape
    bs, ss, _ = x_sample.shape
    zeros_state = jnp.zeros((bp, N_SSM_HEADS, SSM_HEAD_DIM, D_STATE), F32)
    zeros_conv = jnp.zeros((bp, CONV_WIDTH - 1, CONV_DIM), F32)
    xp2, mixed_p, k_p, v_p, ssm_p, conv_p = _mixer(x_prompt, None, None, zeros_state, zeros_conv, prep, SSD_L_PROMPT)
    xs2, mixed_s, k_s, v_s, ssm_s, conv_s = _mixer(x_sample, cache_k[0], cache_v[0], state_ssm[0], state_conv[0],
                                                   prep, ss)

    zero_counts = jnp.zeros((1, LANES), F32)
    h_p, xpk_p, route_p, cnt_p = _resid(xp2, mixed_p, wo, g_ffn, wr, br, zero_counts)
    h_s, xpk_s, route_s, cnt = _resid(xs2, mixed_s, wo, g_ffn, wr, br, cnt_p)
    n_tok = xp2.shape[0] + xs2.shape[0]
    n_slots = n_tok * TOP_K
    n_super = -(-n_slots // SUPER_ROWS) + N_EXPERTS
    n_rows = n_slots + N_EXPERTS * ROW_BLK
    counts = cnt[0, :N_EXPERTS].astype(jnp.int32)
    pad_start, sb_expert, sb_start, sb_rows, n_sb = _super_blocks(counts, n_super)

    def dest_of(route):
        expert = route[:, :TOP_K].astype(jnp.int32)
        pos = route[:, 2 * TOP_K:3 * TOP_K].astype(jnp.int32)
        return (_lookup(pad_start, expert) + pos).astype(jnp.int32).reshape(-1)

    dest_p = dest_of(route_p)
    dest_s = dest_of(route_s)
    xs_sorted = jnp.zeros((n_rows + SUPER_ROWS, D_MODEL // 2), U32)
    xs_sorted = _dispatch(dest_p, xpk_p, xs_sorted)
    xs_sorted = _dispatch(dest_s, xpk_s, xs_sorted)
    y_sorted = _experts(sb_expert, sb_start, sb_rows, n_sb, xs_sorted, w_gate_up[0], b_gate_up[0], w_down[0],
                        b_down[0], n_rows)

    y_p = _combine(dest_p, h_p, route_p, p_prompt[0].reshape(-1, PLE_DIM), y_sorted, g_ple, wpg, wpp,
                   g_final.reshape(1, D_MODEL))
    y_s = _combine(dest_s, h_s, route_s, p_sample[0].reshape(-1, PLE_DIM), y_sorted, g_ple, wpg, wpp,
                   g_final.reshape(1, D_MODEL))
    return (y_p.reshape(bp, sp, D_MODEL), y_s.reshape(bs, ss, D_MODEL), k_p, v_p, ssm_p, conv_p,
            k_s, v_s, ssm_s, conv_s)
```

```python
import functools
import math

import numpy as np
import jax
import jax.numpy as jnp
from jax import lax
from jax.experimental import pallas as pl
from jax.experimental.pallas import tpu as pltpu

F32 = jnp.float32
BF16 = jnp.bfloat16
U32 = jnp.uint32

D_MODEL = 2048
CHUNK = 64
N_ATT_HEADS = 8
ATT_HEAD_DIM = 64
HEAD_W = 2 * ATT_HEAD_DIM
ATT_WIDTH = N_ATT_HEADS * HEAD_W
ATT_SCALE = ATT_HEAD_DIM ** -0.5
N_BUCKETS = 32
MAX_DISTANCE = 128
NEG_INF = -1e30
SSM_D_INNER = 2048
SSM_HEAD_DIM = 64
N_SSM_HEADS = SSM_D_INNER // SSM_HEAD_DIM
N_SSM_GROUPS = 4
HEADS_PER_GROUP = N_SSM_HEADS // N_SSM_GROUPS
GROUP_W = HEADS_PER_GROUP * SSM_HEAD_DIM
D_STATE = 128
CONV_WIDTH = 4
CONV_DIM = SSM_D_INNER + 2 * N_SSM_GROUPS * D_STATE
N_EXPERTS = 32
TOP_K = 4
D_FF = 2048
SWIGLU_LIMIT = 7.0
SWIGLU_ALPHA = 1.702
PLE_DIM = 256
EPS = 1e-6
LAM_INIT = 0.8 - 0.6 * math.exp(-0.3 * 0)

LANES = 128
SUBLANES = 8
MIB = 1024 * 1024

INPROJ_TM = 1024
INPROJ_TN = 512
ATT_TQ = 256
ATT_QB = 128
SSD_L_PROMPT = 128
SSD_LP = 128
TOK_TM = 256
ROW_BLK = 256
SUPER_ROWS = 1280
FF_TILE = 512
DMA_UNROLL = 8


def _dot(a, b):
    return jnp.dot(a, b, preferred_element_type=F32)


def _dot_nt(a, b):
    return lax.dot_general(a, b, (((1,), (1,)), ((), ())), preferred_element_type=F32)


def _rms(x):
    return x * lax.rsqrt(jnp.mean(x * x, axis=-1, keepdims=True) + EPS)


def _sigmoid(x):
    return 1.0 / (1.0 + jnp.exp(-x))


def _split3(x):
    hi = x.astype(BF16)
    r = x - hi.astype(F32)
    mid = r.astype(BF16)
    lo = (r - mid.astype(F32)).astype(BF16)
    return hi, mid, lo


def _xdot_r(x, c):
    hi, mid, lo = _split3(x)
    return (_dot(hi, c) + _dot(mid, c)) + _dot(lo, c)


def _xdot_l(c, x):
    hi, mid, lo = _split3(x)
    return (_dot(c, hi) + _dot(c, mid)) + _dot(c, lo)


_SEG_WIDTHS = (("q", ATT_WIDTH), ("k", ATT_WIDTH), ("v", ATT_WIDTH), ("z", SSM_D_INNER),
               ("xbc", CONV_DIM), ("ga", D_MODEL), ("gs", D_MODEL))


def _segments():
    segs, first = {}, 0
    for name, width in _SEG_WIDTHS:
        assert width % INPROJ_TN == 0
        segs[name] = (first, width // INPROJ_TN)
        first += width // INPROJ_TN
    return segs, first


def _norm_kernel(x_ref, g_ref, wdt_ref, hn_ref, dt_ref):
    hn = (_rms(x_ref[...]) * g_ref[...]).astype(BF16)
    hn_ref[...] = hn
    dt_ref[...] = _dot(hn, wdt_ref[...])


def _proj_kernel(hn_ref, w_ref, *out_refs, mode):
    acc = _dot(hn_ref[...], w_ref[...])
    if mode == "bf16":
        out_refs[0][...] = acc.astype(BF16)
    else:
        out_refs[0][...] = acc
        if mode in ("f32_bf16", "f32_bf16_t"):
            out_refs[1][...] = acc.astype(BF16)
        if mode == "f32_bf16_t":
            out_refs[2][...] = acc.T.astype(BF16)


def _proj(hn, w_main, name, mode):
    t = hn.shape[0]
    tm, tn = min(INPROJ_TM, t), INPROJ_TN
    lo, n = _segments()[0][name]
    assert t % tm == 0
    row_major = pl.BlockSpec((tm, tn), lambda i, j: (i, j))
    dtypes = {"bf16": [BF16], "f32": [F32], "f32_bf16": [F32, BF16], "f32_bf16_t": [F32, BF16]}[mode]
    out_shape = [jax.ShapeDtypeStruct((t, n * tn), d) for d in dtypes]
    out_specs = [row_major for _ in dtypes]
    if mode == "f32_bf16_t":
        out_shape.append(jax.ShapeDtypeStruct((n * tn, t), BF16))
        out_specs.append(pl.BlockSpec((tn, tm), lambda i, j: (j, i)))
    return pl.pallas_call(
        functools.partial(_proj_kernel, mode=mode),
        out_shape=tuple(out_shape),
        grid=(t // tm, n),
        in_specs=[pl.BlockSpec((tm, D_MODEL), lambda i, j: (i, 0)),
                  pl.BlockSpec((D_MODEL, tn), lambda i, j: (0, lo + j))],
        out_specs=tuple(out_specs),
        compiler_params=pltpu.CompilerParams(dimension_semantics=("arbitrary", "arbitrary"),
                                             vmem_limit_bytes=40 * MIB),
        name="proj_" + name,
    )(hn, w_main)


def _inproj(x, g_mix, w_main, w_dt):
    t = x.shape[0]
    tm = TOK_TM
    assert t % tm == 0 and w_main.shape[1] == _segments()[1] * INPROJ_TN
    hn, dt = pl.pallas_call(
        _norm_kernel,
        out_shape=(jax.ShapeDtypeStruct((t, D_MODEL), BF16), jax.ShapeDtypeStruct((t, LANES), F32)),
        grid=(t // tm,),
        in_specs=[pl.BlockSpec((tm, D_MODEL), lambda i: (i, 0)),
                  pl.BlockSpec((1, D_MODEL), lambda i: (0, 0)),
                  pl.BlockSpec((D_MODEL, LANES), lambda i: (0, 0))],
        out_specs=(pl.BlockSpec((tm, D_MODEL), lambda i: (i, 0)), pl.BlockSpec((tm, LANES), lambda i: (i, 0))),
        compiler_params=pltpu.CompilerParams(dimension_semantics=("arbitrary",), vmem_limit_bytes=32 * MIB),
        name="norm_dt",
    )(x, g_mix, w_dt)
    (q,) = _proj(hn, w_main, "q", "bf16")
    k, kb = _proj(hn, w_main, "k", "f32_bf16")
    v, vb, vt = _proj(hn, w_main, "v", "f32_bf16_t")
    (z,) = _proj(hn, w_main, "z", "f32")
    (xbc,) = _proj(hn, w_main, "xbc", "f32")
    (ga,) = _proj(hn, w_main, "ga", "f32")
    (gs,) = _proj(hn, w_main, "gs", "f32")
    return q, k, v, kb, vb, vt, z, xbc, ga, gs, dt


def _t5_bucket(rel):
    nb = N_BUCKETS // 2
    max_exact = nb // 2
    ret = jnp.where(rel > 0, nb, 0)
    n = jnp.abs(rel)
    large = max_exact + (jnp.log(jnp.maximum(n, 1).astype(jnp.float32) / max_exact)
                         / math.log(MAX_DISTANCE / max_exact) * (nb - max_exact)).astype(jnp.int32)
    large = jnp.minimum(large, nb - 1)
    return ret + jnp.where(n < max_exact, n, large)


def _rel_bias_table(rel_bias, q_pos, k_pos):
    bucket = _t5_bucket(k_pos[None, :] - q_pos[:, None])
    return jnp.transpose(_lookup(rel_bias.astype(F32).T, bucket), (2, 0, 1))


def _toeplitz_bias(rel_bias, tq, d):
    n = 2 * tq
    rel = jnp.arange(n, dtype=jnp.int32) - (tq - 1) - d * tq
    w = _lookup(rel_bias.astype(F32).T, _t5_bucket(rel)).T
    skew = jnp.tile(w, (1, tq))[:, :tq * (n - 1)].reshape(w.shape[0], tq, n - 1)
    return skew[:, :, tq - 1:]


def _lookup(table, idx):
    n = table.shape[-1]
    hit = idx[..., None] == jnp.arange(n, dtype=idx.dtype)
    hit = hit.reshape(idx.shape + (1,) * (table.ndim - 1) + (n,))
    return jnp.sum(jnp.where(hit, table, jnp.zeros((), table.dtype)), axis=-1)


def _split_maps(qh):
    lane = lax.broadcasted_iota(jnp.int32, qh.shape, 1)
    q1 = jnp.where(lane < ATT_HEAD_DIM, qh, 0.0) * ATT_SCALE
    q2 = jnp.where(lane >= ATT_HEAD_DIM, qh, 0.0) * ATT_SCALE
    return q1.astype(BF16), q2.astype(BF16)


def _subln(o, lam_unused, sub):
    return (_rms(o) * sub) * (1.0 - LAM_INIT)


def _attn_prompt_kernel(qi_ref, kj_ref, q_ref, k_ref, vt_ref, bias_ref, lam_ref, subt_ref, o_ref,
                        qs_ref, m_ref, l_ref, acc_ref):
    s = pl.program_id(0)
    qi = qi_ref[s]
    kj = kj_ref[s]
    tq = q_ref.shape[0]
    tk = k_ref.shape[0]

    @pl.when(kj == 0)
    def _():
        for h in range(N_ATT_HEADS):
            q1, q2 = _split_maps(q_ref[:, h * HEAD_W:(h + 1) * HEAD_W].astype(F32))
            qs_ref[2 * h] = q1
            qs_ref[2 * h + 1] = q2
        m_ref[...] = jnp.full(m_ref.shape, NEG_INF, F32)
        l_ref[...] = jnp.zeros(l_ref.shape, F32)
        acc_ref[...] = jnp.zeros(acc_ref.shape, F32)

    def step(masked):
        if masked:
            key = lax.broadcasted_iota(jnp.int32, (tk, tq), 0)
            qry = lax.broadcasted_iota(jnp.int32, (tk, tq), 1)
            shift = CHUNK.bit_length() - 1
            visible = jnp.right_shift(key, shift) <= jnp.right_shift(qry, shift)
        for h in range(N_ATT_HEADS):
            kh = k_ref[:, h * HEAD_W:(h + 1) * HEAD_W]
            vth = vt_ref[h * HEAD_W:(h + 1) * HEAD_W, :]
            for c in range(2):
                idx = 2 * h + c
                for qb in range(tq // ATT_QB):
                    ql = slice(qb * ATT_QB, (qb + 1) * ATT_QB)
                    sc = _dot_nt(kh, qs_ref[idx, ql, :]) + bias_ref[0, h, :, ql]
                    if masked:
                        sc = jnp.where(visible[:, ql], sc, NEG_INF)
                    m_old = m_ref[idx, :, ql]
                    m_new = jnp.maximum(m_old, jnp.max(sc, axis=0, keepdims=True))
                    alpha = jnp.exp(m_old - m_new)
                    p = jnp.exp(sc - m_new)
                    l_ref[idx, :, ql] = alpha * l_ref[idx, :, ql] + jnp.sum(p, axis=0, keepdims=True)
                    acc_ref[idx, :, ql] = alpha * acc_ref[idx, :, ql] + _dot(vth, p.astype(BF16))
                    m_ref[idx, :, ql] = m_new

    @pl.when(kj == qi)
    def _():
        step(True)

    @pl.when(kj != qi)
    def _():
        step(False)

    @pl.when(kj == qi)
    def _():
        lam = lam_ref[:, :1]
        subt = subt_ref[...]
        for h in range(N_ATT_HEADS):
            ot = acc_ref[2 * h] / l_ref[2 * h] - lam * (acc_ref[2 * h + 1] / l_ref[2 * h + 1])
            ms = jnp.mean(ot * ot, axis=0, keepdims=True)
            ont = ((ot * lax.rsqrt(ms + EPS)) * subt) * (1.0 - LAM_INIT)
            o_ref[:, h * HEAD_W:(h + 1) * HEAD_W] = ont.T.astype(BF16)


def _attn_prompt(q, k, vt, rel_bias, lam_vec, sub):
    t = q.shape[0]
    tq = ATT_TQ
    assert t % tq == 0 and tq % CHUNK == 0 and tq >= MAX_DISTANCE
    nq = t // tq
    qi = np.concatenate([np.full(i + 1, i, np.int32) for i in range(nq)])
    kj = np.concatenate([np.arange(i + 1, dtype=np.int32) for i in range(nq)])
    bias = jnp.stack([_toeplitz_bias(rel_bias, tq, d) for d in range(3)])
    bias = jnp.transpose(bias, (0, 1, 3, 2))
    sub = sub.reshape(HEAD_W, 1)
    v = vt

    grid_spec = pltpu.PrefetchScalarGridSpec(
        num_scalar_prefetch=2,
        grid=(qi.shape[0],),
        in_specs=[pl.BlockSpec((tq, ATT_WIDTH), lambda s, qi, kj: (qi[s], 0)),
                  pl.BlockSpec((tq, ATT_WIDTH), lambda s, qi, kj: (kj[s], 0)),
                  pl.BlockSpec((ATT_WIDTH, tq), lambda s, qi, kj: (0, kj[s])),
                  pl.BlockSpec((1, N_ATT_HEADS, tq, tq),
                               lambda s, qi, kj: (jnp.minimum(qi[s] - kj[s], 2), 0, 0, 0)),
                  pl.BlockSpec((1, HEAD_W), lambda s, qi, kj: (0, 0)),
                  pl.BlockSpec((HEAD_W, 1), lambda s, qi, kj: (0, 0))],
        out_specs=pl.BlockSpec((tq, ATT_WIDTH), lambda s, qi, kj: (qi[s], 0)),
        scratch_shapes=[pltpu.VMEM((2 * N_ATT_HEADS, tq, HEAD_W), BF16),
                        pltpu.VMEM((2 * N_ATT_HEADS, 1, tq), F32),
                        pltpu.VMEM((2 * N_ATT_HEADS, 1, tq), F32),
                        pltpu.VMEM((2 * N_ATT_HEADS, HEAD_W, tq), F32)])
    return pl.pallas_call(
        _attn_prompt_kernel,
        out_shape=jax.ShapeDtypeStruct((t, ATT_WIDTH), BF16),
        grid_spec=grid_spec,
        compiler_params=pltpu.CompilerParams(dimension_semantics=("arbitrary",), vmem_limit_bytes=40 * MIB),
        name="attn_prompt",
    )(jnp.asarray(qi), jnp.asarray(kj), q, k, v, bias, lam_vec, sub)


def _attn_sample_kernel(q_ref, kn_ref, vn_ref, ck_ref, cv_ref, bc_ref, bn_ref, mc_ref, mn_ref, lam_ref,
                        sub_ref, o_ref):
    lam = lam_ref[...]
    sub = sub_ref[...]
    vis_c = mc_ref[...] > 0.5
    vis_n = mn_ref[...] > 0.5
    for h in range(N_ATT_HEADS):
        hs = slice(h * HEAD_W, (h + 1) * HEAD_W)
        qmaps = _split_maps(q_ref[:, hs].astype(F32))
        kc = ck_ref[:, hs].astype(BF16)
        vc = cv_ref[:, hs].astype(BF16)
        kn = kn_ref[:, hs]
        vn = vn_ref[:, hs]
        probs = []
        for c in range(2):
            sc = jnp.where(vis_c, _dot_nt(qmaps[c], kc) + bc_ref[h], NEG_INF)
            sn = jnp.where(vis_n, _dot_nt(qmaps[c], kn) + bn_ref[h], NEG_INF)
            m = jnp.maximum(jnp.max(sc, axis=-1, keepdims=True), jnp.max(sn, axis=-1, keepdims=True))
            pc = jnp.exp(sc - m)
            pn = jnp.exp(sn - m)
            den = jnp.sum(pc, axis=-1, keepdims=True) + jnp.sum(pn, axis=-1, keepdims=True)
            probs.append((pc / den, pn / den))
        wc = probs[0][0] - lam[:, :1] * probs[1][0]
        wn = probs[0][1] - lam[:, :1] * probs[1][1]
        o = _dot(wc.astype(BF16), vc) + _dot(wn.astype(BF16), vn)
        o_ref[:, hs] = _subln(o, None, sub).astype(BF16)


def _attn_sample(q, kn, vn, cache_k, cache_v, rel_bias, lam_vec, sub):
    bsz, past, _ = cache_k.shape
    seq = q.shape[0] // bsz
    q_pos = past + jnp.arange(seq, dtype=jnp.int32)
    k_pos = jnp.arange(past + seq, dtype=jnp.int32)
    bias = _rel_bias_table(rel_bias, q_pos, k_pos)
    visible = ((k_pos[None, :] // CHUNK) <= (q_pos[:, None] // CHUNK)).astype(F32)
    const = lambda *shape: pl.BlockSpec(shape, lambda b: (0,) * len(shape))
    row = pl.BlockSpec((seq, ATT_WIDTH), lambda b: (b, 0))
    cache = pl.BlockSpec((None, past, ATT_WIDTH), lambda b: (b, 0, 0))
    return pl.pallas_call(
        _attn_sample_kernel,
        out_shape=jax.ShapeDtypeStruct(q.shape, BF16),
        grid=(bsz,),
        in_specs=[row, row, row, cache, cache, const(N_ATT_HEADS, seq, past), const(N_ATT_HEADS, seq, seq),
                  const(seq, past), const(seq, seq), const(1, HEAD_W), const(1, HEAD_W)],
        out_specs=row,
        compiler_params=pltpu.CompilerParams(dimension_semantics=("arbitrary",), vmem_limit_bytes=40 * MIB),
        name="attn_sample",
    )(q, kn, vn, cache_k, cache_v, bias[:, :, :past], bias[:, :, past:], visible[:, :past], visible[:, past:],
      lam_vec, sub)


def _ssd_constants(l):
    hl = HEADS_PER_GROUP * l
    lp = SSD_LP
    e_head = np.zeros((LANES, SSM_D_INNER), np.float32)
    for h in range(N_SSM_HEADS):
        e_head[h, h * SSM_HEAD_DIM:(h + 1) * SSM_HEAD_DIM] = 1.0
    e_grp = np.zeros((N_SSM_GROUPS, LANES, hl), np.float32)
    for g in range(N_SSM_GROUPS):
        for r in range(HEADS_PER_GROUP):
            e_grp[g, g * HEADS_PER_GROUP + r, r * l:(r + 1) * l] = 1.0
    tile8 = np.zeros((lp, hl), np.float32)
    for r in range(HEADS_PER_GROUP):
        tile8[np.arange(l), r * l + np.arange(l)] = 1.0
    causal = np.zeros((l, hl), np.float32)
    for r in range(HEADS_PER_GROUP):
        causal[:, r * l:(r + 1) * l] = np.tril(np.ones((l, l), np.float32))
    tri = np.tril(np.ones((l, l), np.float32))
    ones = np.ones((l, l), np.float32)
    bmask = np.zeros((hl, GROUP_W), np.float32)
    for r in range(HEADS_PER_GROUP):
        bmask[r * l:(r + 1) * l, r * SSM_HEAD_DIM:(r + 1) * SSM_HEAD_DIM] = 1.0
    as_bf = lambda a: jnp.asarray(a, BF16)
    return (as_bf(e_head), as_bf(e_grp), as_bf(tile8), jnp.asarray(causal), as_bf(tri), as_bf(ones),
            as_bf(bmask))


def _ssd_kernel(xbc_ref, z_ref, dt_ref, hist_ref, st0_ref, cw_ref, cb_ref, dtb_ref, alog_ref, dskip_ref,
                norm_ref, eh_ref, eg_ref, t8_ref, caus_ref, tri_ref, ones_ref, bmask_ref,
                yz_ref, st_ref, buf_ref, state_ref):
    c = pl.program_id(1)
    l = xbc_ref.shape[0]
    lp = SSD_LP
    hist_rows = hist_ref.shape[0]

    @pl.when(c == 0)
    def _():
        buf_ref[0:hist_rows, :] = hist_ref[...]
        state_ref[...] = st0_ref[...]

    u = xbc_ref[...]
    buf_ref[hist_rows:hist_rows + l, :] = u
    conv = cb_ref[...] + cw_ref[CONV_WIDTH - 1:CONV_WIDTH, :] * u
    for w in range(CONV_WIDTH - 1):
        shift = CONV_WIDTH - 1 - w
        conv = conv + cw_ref[w:w + 1, :] * buf_ref[hist_rows - shift:hist_rows - shift + l, :]
    buf_ref[0:hist_rows, :] = buf_ref[l:l + hist_rows, :]
    xc = conv * _sigmoid(conv)
    xs = xc[:, :SSM_D_INNER]
    bm = xc[:, SSM_D_INNER:SSM_D_INNER + N_SSM_GROUPS * D_STATE]
    cm = xc[:, SSM_D_INNER + N_SSM_GROUPS * D_STATE:]

    dt_in = dt_ref[...] + dtb_ref[...]
    dt = jnp.maximum(dt_in, 0.0) + jnp.log(1.0 + jnp.exp(-jnp.abs(dt_in)))
    a = -jnp.exp(alog_ref[...])
    acum = _xdot_l(tri_ref[...], dt * a)
    eh = eh_ref[...]
    dt_e = _xdot_r(dt, eh)
    ac_e = _xdot_r(acum, eh)
    a_last = ac_e[l - 1:l, :]
    ecum = jnp.exp(ac_e)
    xdt = xs * dt_e
    xdtw_b = (xdt * jnp.exp(a_last - ac_e)).astype(BF16)
    xdt_b = xdt.astype(BF16)
    drow = jnp.exp(a_last)
    z = z_ref[...]
    caus = caus_ref[...] > 0.5
    t8 = t8_ref[...]
    t8_mask = t8[0:l, :] > 0
    bmask = bmask_ref[...] > 0
    row_pad = lp - l

    for g in range(N_SSM_GROUPS):
        gs = slice(g * GROUP_W, (g + 1) * GROUP_W)
        ns = slice(g * D_STATE, (g + 1) * D_STATE)
        bm_g = bm[:, ns]
        cm_b = cm[:, ns].astype(BF16)
        xw_g = xdtw_b[:, gs]
        if row_pad:
            bm_g = jnp.concatenate([bm_g, jnp.zeros((row_pad, D_STATE), F32)], axis=0)
            xw_g = jnp.concatenate([xw_g, jnp.zeros((row_pad, GROUP_W), BF16)], axis=0)
        bmt_b = bm_g.T.astype(BF16)
        cb8 = _dot(cm_b, _dot(bmt_b, t8).astype(BF16))
        a1 = _xdot_r(acum, eg_ref[g])
        a2 = _xdot_l(ones_ref[...], jnp.where(t8_mask, a1, 0.0))
        decay = jnp.where(caus, jnp.exp(jnp.where(caus, a1 - a2, 0.0)), 0.0)
        m_b = (cb8 * decay).astype(BF16)
        xg = xdt_b[:, gs]
        bd = jnp.concatenate([xg] * HEADS_PER_GROUP, axis=0)
        bd = jnp.where(bmask, bd, jnp.zeros_like(bd))
        y = _dot(m_b, bd)
        st_g = state_ref[g]
        y = y + _dot(cm_b, st_g.astype(BF16)) * ecum[:, gs]
        y = y + dskip_ref[:, gs] * xs[:, gs]
        state_ref[g] = st_g * drow[:, gs] + _dot(bmt_b, xw_g)
        zg = z[:, gs]
        yz = y * (zg * _sigmoid(zg))
        yz_ref[:, gs] = (_rms(yz) * norm_ref[:, gs]).astype(BF16)

    @pl.when(c == pl.num_programs(1) - 1)
    def _():
        st_ref[...] = state_ref[...]


def _ssd(xbc, z, dt, hist8, st0, conv_w, conv_b, dt_bias, a_log, dskip_e, ssm_norm, bsz, l):
    rows = xbc.shape[0]
    seq = rows // bsz
    assert seq % l == 0 and l % SUBLANES == 0 and l <= SSD_LP and l >= SUBLANES
    nc = seq // l
    consts = _ssd_constants(l)
    rowblk = lambda width: pl.BlockSpec((l, width), lambda b, c: (b * nc + c, 0))
    const = lambda arr: pl.BlockSpec(arr.shape, lambda b, c: (0,) * arr.ndim)
    params = (conv_w, conv_b, dt_bias, a_log, dskip_e, ssm_norm)
    return pl.pallas_call(
        _ssd_kernel,
        out_shape=(jax.ShapeDtypeStruct((rows, SSM_D_INNER), BF16),
                   jax.ShapeDtypeStruct(st0.shape, F32)),
        grid=(bsz, nc),
        in_specs=[rowblk(CONV_DIM), rowblk(SSM_D_INNER), rowblk(LANES),
                  pl.BlockSpec((None,) + hist8.shape[1:], lambda b, c: (b, 0, 0)),
                  pl.BlockSpec((None,) + st0.shape[1:], lambda b, c: (b, 0, 0, 0))]
                 + [const(p) for p in params] + [const(k) for k in consts],
        out_specs=(rowblk(SSM_D_INNER), pl.BlockSpec((None,) + st0.shape[1:], lambda b, c: (b, 0, 0, 0))),
        scratch_shapes=[pltpu.VMEM((SUBLANES + l, CONV_DIM), F32),
                        pltpu.VMEM(st0.shape[1:], F32)],
        compiler_params=pltpu.CompilerParams(dimension_semantics=("arbitrary", "arbitrary"),
                                             vmem_limit_bytes=48 * MIB),
        name="ssd",
    )(xbc, z, dt, hist8, st0, *params, *consts)


def _mix_kernel(on_ref, yz_ref, ga_ref, gs_ref, wa_ref, ws_ref, o_ref):
    att = _dot(on_ref[...], wa_ref[...])
    ssm = _dot(yz_ref[...], ws_ref[...])
    o_ref[...] = (_sigmoid(ga_ref[...]) * att + _sigmoid(gs_ref[...]) * ssm).astype(BF16)


def _mix(on, yz, ga, gs, wa, ws):
    t = on.shape[0]
    tm = TOK_TM
    rowblk = lambda width: pl.BlockSpec((tm, width), lambda i: (i, 0))
    const = lambda arr: pl.BlockSpec(arr.shape, lambda i: (0,) * arr.ndim)
    return pl.pallas_call(
        _mix_kernel,
        out_shape=jax.ShapeDtypeStruct((t, D_MODEL), BF16),
        grid=(t // tm,),
        in_specs=[rowblk(ATT_WIDTH), rowblk(SSM_D_INNER), rowblk(D_MODEL), rowblk(D_MODEL), const(wa), const(ws)],
        out_specs=rowblk(D_MODEL),
        compiler_params=pltpu.CompilerParams(dimension_semantics=("arbitrary",), vmem_limit_bytes=48 * MIB),
        name="mix",
    )(on, yz, ga, gs, wa, ws)


def _pack_bf16_pairs(x):
    w = x.shape[1] // 2
    lo = lax.bitcast_convert_type(x[:, :w].astype(BF16).astype(F32), U32)
    hi = lax.bitcast_convert_type(x[:, w:].astype(BF16).astype(F32), U32)
    return hi | (lo >> 16)


def _unpack_bf16_pairs(words):
    lo = lax.bitcast_convert_type(words << 16, F32)
    hi = lax.bitcast_convert_type(words & jnp.uint32(0xFFFF0000), F32)
    return jnp.concatenate([lo, hi], axis=1).astype(BF16)


def _resid_kernel(x_ref, mixed_ref, wo_ref, g_ref, wr_ref, br_ref, lt_ref, cin_ref,
                  h_ref, xp_ref, route_ref, cnt_ref, carry_ref):
    i = pl.program_id(0)

    @pl.when(i == 0)
    def _():
        carry_ref[...] = cin_ref[...]

    h = x_ref[...] + _dot(mixed_ref[...], wo_ref[...])
    h_ref[...] = h
    xn = _rms(h) * g_ref[...]
    xp_ref[...] = _pack_bf16_pairs(xn)
    logits = _dot(xn.astype(BF16), wr_ref[...]) + br_ref[...]

    lane = lax.broadcasted_iota(jnp.int32, logits.shape, 1)
    lane_f = lane.astype(F32)
    rest = logits
    vals, idxs, sels = [], [], []
    for _ in range(TOP_K):
        m = jnp.max(rest, axis=-1, keepdims=True)
        idx = jnp.min(jnp.where(rest == m, lane_f, float(LANES)), axis=-1, keepdims=True)
        sel = lane_f == idx
        rest = jnp.where(sel, -jnp.inf, rest)
        vals.append(m)
        idxs.append(idx)
        sels.append(sel)
    exps = [jnp.exp(v - vals[0]) for v in vals]
    den = exps[0] + exps[1] + exps[2] + exps[3]
    onehot = jnp.zeros(logits.shape, F32)
    for sel in sels:
        onehot = onehot + jnp.where(sel, 1.0, 0.0)
    before = _dot(lt_ref[...], onehot.astype(BF16)) + carry_ref[...]
    route = jnp.zeros(logits.shape, F32)
    for k in range(TOP_K):
        pos = jnp.sum(jnp.where(sels[k], before, 0.0), axis=-1, keepdims=True)
        route = route + jnp.where(lane == k, idxs[k], 0.0)
        route = route + jnp.where(lane == TOP_K + k, exps[k] / den, 0.0)
        route = route + jnp.where(lane == 2 * TOP_K + k, pos, 0.0)
    route_ref[...] = route
    carry_ref[...] = carry_ref[...] + jnp.sum(onehot, axis=0, keepdims=True)
    cnt_ref[...] = carry_ref[...]


def _resid(x, mixed, wo, g_ffn, wr, br, counts_in):
    t = x.shape[0]
    tm = TOK_TM
    lt = jnp.asarray(np.tril(np.ones((tm, tm), np.float32), -1), BF16)
    rowblk = lambda width: pl.BlockSpec((tm, width), lambda i: (i, 0))
    const = lambda arr: pl.BlockSpec(arr.shape, lambda i: (0,) * arr.ndim)
    return pl.pallas_call(
        _resid_kernel,
        out_shape=(jax.ShapeDtypeStruct((t, D_MODEL), F32),
                   jax.ShapeDtypeStruct((t, D_MODEL // 2), U32),
                   jax.ShapeDtypeStruct((t, LANES), F32),
                   jax.ShapeDtypeStruct((1, LANES), F32)),
        grid=(t // tm,),
        in_specs=[rowblk(D_MODEL), rowblk(D_MODEL), const(wo), const(g_ffn), const(wr), const(br), const(lt),
                  const(counts_in)],
        out_specs=(rowblk(D_MODEL), rowblk(D_MODEL // 2), rowblk(LANES), pl.BlockSpec((1, LANES), lambda i: (0, 0))),
        scratch_shapes=[pltpu.VMEM((1, LANES), F32)],
        compiler_params=pltpu.CompilerParams(dimension_semantics=("arbitrary",), vmem_limit_bytes=48 * MIB),
        name="resid_route",
    )(x, mixed, wo, g_ffn, wr, br, lt, counts_in)


def _row_copy(src, src_row, dst, dst_row, sem):
    return pltpu.make_async_copy(src.at[pl.ds(src_row, 1)], dst.at[pl.ds(dst_row, 1)], sem)


def _dispatch_kernel(dest_ref, xp_ref, xs_in_ref, xs_ref, sem):
    del xs_in_ref

    def issue(t, carry):
        for k in range(TOP_K):
            _row_copy(xp_ref, t, xs_ref, dest_ref[t * TOP_K + k], sem).start()
        return carry

    lax.fori_loop(0, TOK_TM, issue, 0, unroll=DMA_UNROLL)

    def drain(t, carry):
        for k in range(TOP_K):
            _row_copy(xp_ref, 0, xs_ref, 0, sem).wait()
        return carry

    lax.fori_loop(0, TOK_TM, drain, 0, unroll=DMA_UNROLL)


def _dispatch(dest_flat, xp, xs):
    t = xp.shape[0]
    return pl.pallas_call(
        _dispatch_kernel,
        out_shape=jax.ShapeDtypeStruct(xs.shape, xs.dtype),
        grid=(t // TOK_TM,),
        in_specs=[pl.BlockSpec((TOK_TM * TOP_K,), lambda i: (i,), memory_space=pltpu.SMEM),
                  pl.BlockSpec((TOK_TM, xp.shape[1]), lambda i: (i, 0)),
                  pl.BlockSpec(memory_space=pl.ANY)],
        out_specs=pl.BlockSpec(memory_space=pl.ANY),
        scratch_shapes=[pltpu.SemaphoreType.DMA(())],
        input_output_aliases={2: 0},
        compiler_params=pltpu.CompilerParams(dimension_semantics=("arbitrary",)),
        name="dispatch",
    )(dest_flat, xp, xs)


def _experts_kernel(sbe_ref, sbs_ref, sbr_ref, nsb_ref, xs_ref, wg_ref, wu_ref, wd_ref, bg_ref, bu_ref,
                    bd_ref, y_ref, xw_ref, xb_ref, acc_ref, wgb_ref, wub_ref, wdb_ref, sem_in, sem_out):
    s = pl.program_id(0)
    f = pl.program_id(1)
    nf = pl.num_programs(1)
    n_sub = SUPER_ROWS // ROW_BLK

    @pl.when(s < nsb_ref[0])
    def _():
        start = pl.multiple_of(sbs_ref[s], ROW_BLK)
        rows = sbr_ref[s]

        def x_copy(sb, slot):
            src = xs_ref.at[pl.ds(pl.multiple_of(sbs_ref[sb], ROW_BLK), SUPER_ROWS)]
            return pltpu.make_async_copy(src, xw_ref.at[slot], sem_in.at[slot])

        @pl.when(f == 0)
        def _():
            slot = s % 2

            @pl.when(s == 0)
            def _():
                x_copy(0, 0).start()

            x_copy(s, slot).wait()

            @pl.when(s + 1 < nsb_ref[0])
            def _():
                x_copy(s + 1, 1 - slot).start()

            xb_ref[...] = _unpack_bf16_pairs(xw_ref[slot])
            acc_ref[...] = jnp.zeros(acc_ref.shape, F32)

        wgb_ref[...] = wg_ref[0].astype(BF16)
        wub_ref[...] = wu_ref[0].astype(BF16)
        wdb_ref[...] = wd_ref[0].astype(BF16)
        half_rows = SUPER_ROWS // 2
        for half in range(2):
            rs = slice(half * half_rows, (half + 1) * half_rows)
            x = xb_ref[rs, :]
            gate = _dot(x, wgb_ref[...]) + bg_ref[0]
            up = _dot(x, wub_ref[...]) + bu_ref[0]
            gate = jnp.minimum(gate, SWIGLU_LIMIT)
            up = jnp.clip(up, -SWIGLU_LIMIT, SWIGLU_LIMIT)
            act = (up + 1.0) * gate * _sigmoid(SWIGLU_ALPHA * gate)
            acc_ref[rs, :] += _dot(act.astype(BF16), wdb_ref[...])

        @pl.when(f == nf - 1)
        def _():
            def out_copy(sub):
                rs = pl.ds(sub * ROW_BLK, ROW_BLK)
                return pltpu.make_async_copy(acc_ref.at[rs], y_ref.at[pl.ds(start + sub * ROW_BLK, ROW_BLK)],
                                             sem_out)

            for sub in range(n_sub):
                @pl.when(sub * ROW_BLK < rows)
                def _():
                    rs = slice(sub * ROW_BLK, (sub + 1) * ROW_BLK)
                    acc_ref[rs, :] += bd_ref[0]
                    out_copy(sub).start()

            for sub in range(n_sub):
                @pl.when(sub * ROW_BLK < rows)
                def _():
                    out_copy(sub).wait()

    @pl.when((s == pl.num_programs(0) - 1) & (f == nf - 1))
    def _():
        zero_ref = acc_ref.at[pl.ds(0, ROW_BLK)]
        zero_ref[...] = jnp.zeros(zero_ref.shape, F32)
        n_blocks = y_ref.shape[0] // ROW_BLK

        def tail_copy(b):
            return pltpu.make_async_copy(zero_ref, y_ref.at[pl.ds(pl.multiple_of(b * ROW_BLK, ROW_BLK), ROW_BLK)],
                                         sem_out)

        def issue(b, carry):
            tail_copy(b).start()
            return carry

        def drain(b, carry):
            tail_copy(b).wait()
            return carry

        lax.fori_loop(nsb_ref[1], n_blocks, issue, 0)
        lax.fori_loop(nsb_ref[1], n_blocks, drain, 0)


def _experts(sb_expert, sb_start, sb_rows, n_sb, xs, w_gate_up, b_gate_up, w_down, b_down, n_rows):
    n_super = sb_expert.shape[0]
    nf = D_FF // FF_TILE

    def widx(s, f, sbe, sbs, sbr, nsb):
        live = s < nsb[0]
        return sbe[s], jnp.where(live, f, nf - 1)

    def gate_map(s, f, *pref):
        e, ff = widx(s, f, *pref)
        return (e, 0, ff)

    def up_map(s, f, *pref):
        e, ff = widx(s, f, *pref)
        return (e, 0, nf + ff)

    def down_map(s, f, *pref):
        e, ff = widx(s, f, *pref)
        return (e, ff, 0)

    def bias_map(s, f, *pref):
        return (widx(s, f, *pref)[0], 0, 0)

    b_gu3 = b_gate_up.reshape(N_EXPERTS, 1, 2 * D_FF)
    b_dn3 = b_down.reshape(N_EXPERTS, 1, D_MODEL)
    grid_spec = pltpu.PrefetchScalarGridSpec(
        num_scalar_prefetch=4,
        grid=(n_super, nf),
        in_specs=[pl.BlockSpec(memory_space=pl.ANY),
                  pl.BlockSpec((1, D_MODEL, FF_TILE), gate_map),
                  pl.BlockSpec((1, D_MODEL, FF_TILE), up_map),
                  pl.BlockSpec((1, FF_TILE, D_MODEL), down_map),
                  pl.BlockSpec((1, 1, FF_TILE), gate_map),
                  pl.BlockSpec((1, 1, FF_TILE), up_map),
                  pl.BlockSpec((1, 1, D_MODEL), bias_map)],
        out_specs=pl.BlockSpec(memory_space=pl.ANY),
        scratch_shapes=[pltpu.VMEM((2, SUPER_ROWS, D_MODEL // 2), U32),
                        pltpu.VMEM((SUPER_ROWS, D_MODEL), BF16),
                        pltpu.VMEM((SUPER_ROWS, D_MODEL), F32),
                        pltpu.VMEM((D_MODEL, FF_TILE), BF16),
                        pltpu.VMEM((D_MODEL, FF_TILE), BF16),
                        pltpu.VMEM((FF_TILE, D_MODEL), BF16),
                        pltpu.SemaphoreType.DMA((2,)),
                        pltpu.SemaphoreType.DMA(())])
    return pl.pallas_call(
        _experts_kernel,
        out_shape=jax.ShapeDtypeStruct((n_rows, D_MODEL), F32),
        grid_spec=grid_spec,
        compiler_params=pltpu.CompilerParams(dimension_semantics=("arbitrary", "arbitrary"),
                                             vmem_limit_bytes=56 * MIB),
        name="experts",
    )(sb_expert, sb_start, sb_rows, n_sb, xs, w_gate_up, w_gate_up, w_down, b_gu3, b_gu3, b_dn3)


def _combine_kernel(dest_ref, dest_next_ref, h_ref, route_ref, p_ref, y_ref, gple_ref, wpg_ref, wpp_ref,
                    gfin_ref, o_ref, gbuf_ref, sem, *, n_tiles):
    i = pl.program_id(0)
    slot = i % 2

    def gather_tile(d_ref, into):
        def issue(t, carry):
            for k in range(TOP_K):
                _row_copy(y_ref, d_ref[t * TOP_K + k], gbuf_ref.at[into].at[k], t, sem.at[into]).start()
            return carry

        lax.fori_loop(0, TOK_TM, issue, 0, unroll=DMA_UNROLL)

    @pl.when(i == 0)
    def _():
        gather_tile(dest_ref, 0)

    @pl.when(i + 1 < n_tiles)
    def _():
        gather_tile(dest_next_ref, 1 - slot)

    def drain(t, carry):
        for k in range(TOP_K):
            _row_copy(y_ref, 0, gbuf_ref.at[slot].at[k], 0, sem.at[slot]).wait()
        return carry

    lax.fori_loop(0, TOK_TM, drain, 0, unroll=DMA_UNROLL)

    route = route_ref[...]
    h = h_ref[...]
    for k in range(TOP_K):
        h = h + route[:, TOP_K + k:TOP_K + k + 1] * gbuf_ref[slot, k]
    xn = (_rms(h) * gple_ref[...]).astype(BF16)
    gate = _sigmoid(_dot(xn, wpg_ref[...]))
    h = h + gate * _dot(p_ref[...].astype(BF16), wpp_ref[...])
    o_ref[...] = _rms(h) * gfin_ref[...]


def _combine(dest_flat, h, route, p, y_sorted, g_ple, wpg, wpp, g_final):
    t = h.shape[0]
    tm = TOK_TM
    rowblk = lambda width: pl.BlockSpec((tm, width), lambda i: (i, 0))
    const = lambda arr: pl.BlockSpec(arr.shape, lambda i: (0,) * arr.ndim)
    return pl.pallas_call(
        functools.partial(_combine_kernel, n_tiles=t // tm),
        out_shape=jax.ShapeDtypeStruct((t, D_MODEL), F32),
        grid=(t // tm,),
        in_specs=[pl.BlockSpec((tm * TOP_K,), lambda i: (i,), memory_space=pltpu.SMEM),
                  pl.BlockSpec((tm * TOP_K,), lambda i: (jnp.minimum(i + 1, t // tm - 1),),
                               memory_space=pltpu.SMEM),
                  rowblk(D_MODEL), rowblk(LANES), rowblk(PLE_DIM),
                  pl.BlockSpec(memory_space=pl.ANY),
                  const(g_ple), const(wpg), const(wpp), const(g_final)],
        out_specs=rowblk(D_MODEL),
        scratch_shapes=[pltpu.VMEM((2, TOP_K, tm, D_MODEL), F32), pltpu.SemaphoreType.DMA((2,))],
        compiler_params=pltpu.CompilerParams(dimension_semantics=("arbitrary",), vmem_limit_bytes=52 * MIB),
        name="combine",
    )(dest_flat, dest_flat, h, route, p, y_sorted, g_ple, wpg, wpp, g_final)


def _super_blocks(counts, n_super):
    padded = (counts + ROW_BLK - 1) // ROW_BLK * ROW_BLK
    pad_start = jnp.cumsum(padded) - padded
    per_expert = (counts + SUPER_ROWS - 1) // SUPER_ROWS
    sb_end = jnp.cumsum(per_expert)
    n_sb = sb_end[-1]
    s = jnp.arange(n_super, dtype=jnp.int32)
    s_live = jnp.minimum(s, jnp.maximum(n_sb - 1, 0))
    expert = jnp.sum(sb_end[None, :] <= s_live[:, None], axis=1).astype(jnp.int32)
    expert = jnp.minimum(expert, N_EXPERTS - 1)
    within = s_live - _lookup(sb_end - per_expert, expert)
    start = _lookup(pad_start, expert) + within * SUPER_ROWS
    rows = jnp.where(s < n_sb, jnp.clip(_lookup(counts, expert) - within * SUPER_ROWS, 0, SUPER_ROWS), 0)
    first_unused_block = jnp.sum(padded) // ROW_BLK
    return (pad_start, expert, start.astype(jnp.int32), rows.astype(jnp.int32),
            jnp.stack([n_sb, first_unused_block]).astype(jnp.int32))


def _mixer(x, cache_k, cache_v, st0, conv0, prep, ssd_l):
    bsz, seq, _ = x.shape
    x2 = x.reshape(bsz * seq, D_MODEL)
    q, k, v, kb, vb, vt, z, xbc, ga, gs, dt = _inproj(x2, prep["g_mix"], prep["w_main"], prep["w_dt"])
    if cache_k is None:
        assert bsz == 1
        on = _attn_prompt(q, kb, vt, prep["rel_bias"], prep["lam"], prep["subln"])
    else:
        past = cache_k.shape[1]
        on = _attn_sample(q, kb, vb, cache_k.reshape(bsz, past, ATT_WIDTH), cache_v.reshape(bsz, past, ATT_WIDTH),
                          prep["rel_bias"], prep["lam"], prep["subln"])
    hist8 = jnp.pad(conv0, ((0, 0), (SUBLANES - (CONV_WIDTH - 1), 0), (0, 0)))
    st0_t = st0.reshape(bsz, N_SSM_GROUPS, HEADS_PER_GROUP, SSM_HEAD_DIM, D_STATE)
    st0_t = jnp.transpose(st0_t, (0, 1, 4, 2, 3)).reshape(bsz, N_SSM_GROUPS, D_STATE, GROUP_W)
    yz, st_t = _ssd(xbc, z, dt, hist8, st0_t, prep["conv_w"], prep["conv_b"], prep["dt_bias"], prep["a_log"],
                    prep["dskip_e"], prep["ssm_norm"], bsz, ssd_l)
    st_new = st_t.reshape(bsz, N_SSM_GROUPS, D_STATE, HEADS_PER_GROUP, SSM_HEAD_DIM)
    st_new = jnp.transpose(st_new, (0, 1, 3, 4, 2)).reshape(1, bsz, N_SSM_HEADS, SSM_HEAD_DIM, D_STATE)
    mixed = _mix(on, yz, ga, gs, prep["w_attn_out"], prep["w_ssm_out"])
    k_rows = k.reshape(1, bsz, seq, N_ATT_HEADS, HEAD_W)
    v_rows = v.reshape(1, bsz, seq, N_ATT_HEADS, HEAD_W)
    conv_new = xbc.reshape(bsz, seq, CONV_DIM)[:, seq - (CONV_WIDTH - 1):].reshape(1, bsz, CONV_WIDTH - 1, CONV_DIM)
    return x2, mixed, k_rows, v_rows, st_new, conv_new


def kernel(x_prompt, x_sample, cache_k, cache_v, state_ssm, state_conv, p_prompt, p_sample, rel_bias, w_in,
           lambda_q1, lambda_k1, lambda_q2, lambda_k2, attn_subln, w_attn_out, conv_w, conv_b, dt_bias, a_log,
           d_skip, ssm_norm, w_ssm_out, w_o, g_mix, g_ffn, w_router, b_router, w_gate_up, b_gate_up, w_down,
           b_down, g_ple, w_ple_gate, w_ple_proj, g_final):
    w = w_in[0]
    c_dt = 3 * ATT_WIDTH + SSM_D_INNER + CONV_DIM
    w_main = jnp.concatenate([w[:, :c_dt], w[:, c_dt + N_SSM_HEADS:]], axis=1).astype(BF16)
    w_dt = jnp.pad(w[:, c_dt:c_dt + N_SSM_HEADS], ((0, 0), (0, LANES - N_SSM_HEADS))).astype(BF16)
    lam = (jnp.exp(jnp.sum(lambda_q1[0] * lambda_k1[0]).astype(F32))
           - jnp.exp(jnp.sum(lambda_q2[0] * lambda_k2[0]).astype(F32)) + LAM_INIT)
    pad_heads = lambda v: jnp.pad(v.reshape(1, N_SSM_HEADS), ((0, 0), (0, LANES - N_SSM_HEADS)))
    prep = dict(
        g_mix=g_mix, w_main=w_main, w_dt=w_dt, rel_bias=rel_bias,
        lam=jnp.full((1, HEAD_W), lam, F32), subln=attn_subln,
        conv_w=conv_w[0], conv_b=conv_b, dt_bias=pad_heads(dt_bias[0]), a_log=pad_heads(a_log[0]),
        dskip_e=jnp.repeat(d_skip[0], SSM_HEAD_DIM).reshape(1, SSM_D_INNER), ssm_norm=ssm_norm,
        w_attn_out=w_attn_out[0].astype(BF16), w_ssm_out=w_ssm_out[0].astype(BF16))
    wo = w_o[0].astype(BF16)
    wr = jnp.pad(w_router[0], ((0, 0), (0, LANES - N_EXPERTS))).astype(BF16)
    br = jnp.pad(b_router, ((0, 0), (0, LANES - N_EXPERTS)), constant_values=NEG_INF)
    wpg = w_ple_gate[0].astype(BF16)
    wpp = w_ple_proj[0].astype(BF16)

    bp, sp, _ = x_prompt.shape
    bs, ss, _ = x_sample.shape
    zeros_state = jnp.zeros((bp, N_SSM_HEADS, SSM_HEAD_DIM, D_STATE), F32)
    zeros_conv = jnp.zeros((bp, CONV_WIDTH - 1, CONV_DIM), F32)
    xp2, mixed_p, k_p, v_p, ssm_p, conv_p = _mixer(x_prompt, None, None, zeros_state, zeros_conv, prep, SSD_L_PROMPT)
    xs2, mixed_s, k_s, v_s, ssm_s, conv_s = _mixer(x_sample, cache_k[0], cache_v[0], state_ssm[0], state_conv[0],
                                                   prep, ss)

    zero_counts = jnp.zeros((1, LANES), F32)
    h_p, xpk_p, route_p, cnt_p = _resid(xp2, mixed_p, wo, g_ffn, wr, br, zero_counts)
    h_s, xpk_s, route_s, cnt = _resid(xs2, mixed_s, wo, g_ffn, wr, br, cnt_p)
    n_tok = xp2.shape[0] + xs2.shape[0]
    n_slots = n_tok * TOP_K
    n_super = -(-n_slots // SUPER_ROWS) + N_EXPERTS
    n_rows = n_slots + N_EXPERTS * ROW_BLK
    counts = cnt[0, :N_EXPERTS].astype(jnp.int32)
    pad_start, sb_expert, sb_start, sb_rows, n_sb = _super_blocks(counts, n_super)

    def dest_of(route):
        expert = route[:, :TOP_K].astype(jnp.int32)
        pos = route[:, 2 * TOP_K:3 * TOP_K].astype(jnp.int32)
        return (_lookup(pad_start, expert) + pos).astype(jnp.int32).reshape(-1)

    dest_p = dest_of(route_p)
    dest_s = dest_of(route_s)
    xs_sorted = jnp.zeros((n_rows + SUPER_ROWS, D_MODEL // 2), U32)
    xs_sorted = _dispatch(dest_p, xpk_p, xs_sorted)
    xs_sorted = _dispatch(dest_s, xpk_s, xs_sorted)
    y_sorted = _experts(sb_expert, sb_start, sb_rows, n_sb, xs_sorted, w_gate_up[0], b_gate_up[0], w_down[0],
                        b_down[0], n_rows)

    y_p = _combine(dest_p, h_p, route_p, p_prompt[0].reshape(-1, PLE_DIM), y_sorted, g_ple, wpg, wpp,
                   g_final.reshape(1, D_MODEL))
    y_s = _combine(dest_s, h_s, route_s, p_sample[0].reshape(-1, PLE_DIM), y_sorted, g_ple, wpg, wpp,
                   g_final.reshape(1, D_MODEL))
    return (y_p.reshape(bp, sp, D_MODEL), y_s.reshape(bs, ss, D_MODEL), k_p, v_p, ssm_p, conv_p,
            k_s, v_s, ssm_s, conv_s)
```

```python
import functools
import math

import numpy as np
import jax
import jax.numpy as jnp
from jax import lax
from jax.experimental import pallas as pl
from jax.experimental.pallas import tpu as pltpu

F32 = jnp.float32
BF16 = jnp.bfloat16
U32 = jnp.uint32

D_MODEL = 2048
CHUNK = 64
N_ATT_HEADS = 8
ATT_HEAD_DIM = 64
HEAD_W = 2 * ATT_HEAD_DIM
ATT_WIDTH = N_ATT_HEADS * HEAD_W
ATT_SCALE = ATT_HEAD_DIM ** -0.5
N_BUCKETS = 32
MAX_DISTANCE = 128
NEG_INF = -1e30
SSM_D_INNER = 2048
SSM_HEAD_DIM = 64
N_SSM_HEADS = SSM_D_INNER // SSM_HEAD_DIM
N_SSM_GROUPS = 4
HEADS_PER_GROUP = N_SSM_HEADS // N_SSM_GROUPS
GROUP_W = HEADS_PER_GROUP * SSM_HEAD_DIM
D_STATE = 128
CONV_WIDTH = 4
CONV_DIM = SSM_D_INNER + 2 * N_SSM_GROUPS * D_STATE
N_EXPERTS = 32
TOP_K = 4
D_FF = 2048
SWIGLU_LIMIT = 7.0
SWIGLU_ALPHA = 1.702
PLE_DIM = 256
EPS = 1e-6
LAM_INIT = 0.8 - 0.6 * math.exp(-0.3 * 0)

LANES = 128
SUBLANES = 8
MIB = 1024 * 1024

INPROJ_TM = 1024
INPROJ_TN = 512
ATT_TQ = 256
ATT_QB = 128
SSD_L_PROMPT = 128
SSD_LP = 128
TOK_TM = 256
ROW_BLK = 256
SUPER_ROWS = 1280
FF_TILE = 512
DMA_UNROLL = 8


def _dot(a, b):
    return jnp.dot(a, b, preferred_element_type=F32)


def _dot_nt(a, b):
    return lax.dot_general(a, b, (((1,), (1,)), ((), ())), preferred_element_type=F32)


def _rms(x):
    return x * lax.rsqrt(jnp.mean(x * x, axis=-1, keepdims=True) + EPS)


def _sigmoid(x):
    return 1.0 / (1.0 + jnp.exp(-x))


def _split3(x):
    hi = x.astype(BF16)
    r = x - hi.astype(F32)
    mid = r.astype(BF16)
    lo = (r - mid.astype(F32)).astype(BF16)
    return hi, mid, lo


def _xdot_r(x, c):
    hi, mid, lo = _split3(x)
    return (_dot(hi, c) + _dot(mid, c)) + _dot(lo, c)


def _xdot_l(c, x):
    hi, mid, lo = _split3(x)
    return (_dot(c, hi) + _dot(c, mid)) + _dot(c, lo)


_SEG_WIDTHS = (("q", ATT_WIDTH), ("k", ATT_WIDTH), ("v", ATT_WIDTH), ("z", SSM_D_INNER),
               ("xbc", CONV_DIM), ("ga", D_MODEL), ("gs", D_MODEL))


def _segments():
    segs, first = {}, 0
    for name, width in _SEG_WIDTHS:
        assert width % INPROJ_TN == 0
        segs[name] = (first, width // INPROJ_TN)
        first += width // INPROJ_TN
    return segs, first


def _norm_kernel(x_ref, g_ref, wdt_ref, hn_ref, dt_ref):
    hn = (_rms(x_ref[...]) * g_ref[...]).astype(BF16)
    hn_ref[...] = hn
    dt_ref[...] = _dot(hn, wdt_ref[...])


def _proj_kernel(hn_ref, w_ref, *out_refs, mode):
    acc = _dot(hn_ref[...], w_ref[...])
    if mode == "bf16":
        out_refs[0][...] = acc.astype(BF16)
    else:
        out_refs[0][...] = acc
        if mode in ("f32_bf16", "f32_bf16_t"):
            out_refs[1][...] = acc.astype(BF16)
        if mode == "f32_bf16_t":
            out_refs[2][...] = acc.T.astype(BF16)


def _proj(hn, w_main, name, mode):
    t = hn.shape[0]
    tm, tn = min(INPROJ_TM, t), INPROJ_TN
    lo, n = _segments()[0][name]
    assert t % tm == 0
    row_major = pl.BlockSpec((tm, tn), lambda i, j: (i, j))
    dtypes = {"bf16": [BF16], "f32": [F32], "f32_bf16": [F32, BF16], "f32_bf16_t": [F32, BF16]}[mode]
    out_shape = [jax.ShapeDtypeStruct((t, n * tn), d) for d in dtypes]
    out_specs = [row_major for _ in dtypes]
    if mode == "f32_bf16_t":
        out_shape.append(jax.ShapeDtypeStruct((n * tn, t), BF16))
        out_specs.append(pl.BlockSpec((tn, tm), lambda i, j: (j, i)))
    return pl.pallas_call(
        functools.partial(_proj_kernel, mode=mode),
        out_shape=tuple(out_shape),
        grid=(t // tm, n),
        in_specs=[pl.BlockSpec((tm, D_MODEL), lambda i, j: (i, 0)),
                  pl.BlockSpec((None, D_MODEL, tn), lambda i, j: (lo + j, 0, 0))],
        out_specs=tuple(out_specs),
        compiler_params=pltpu.CompilerParams(dimension_semantics=("arbitrary", "arbitrary"),
                                             vmem_limit_bytes=40 * MIB),
        name="proj_" + name,
    )(hn, w_main)


def _inproj(x, g_mix, w_main, w_dt):
    t = x.shape[0]
    tm = TOK_TM
    assert t % tm == 0 and w_main.shape == (_segments()[1], D_MODEL, INPROJ_TN)
    hn, dt = pl.pallas_call(
        _norm_kernel,
        out_shape=(jax.ShapeDtypeStruct((t, D_MODEL), BF16), jax.ShapeDtypeStruct((t, LANES), F32)),
        grid=(t // tm,),
        in_specs=[pl.BlockSpec((tm, D_MODEL), lambda i: (i, 0)),
                  pl.BlockSpec((1, D_MODEL), lambda i: (0, 0)),
                  pl.BlockSpec((D_MODEL, LANES), lambda i: (0, 0))],
        out_specs=(pl.BlockSpec((tm, D_MODEL), lambda i: (i, 0)), pl.BlockSpec((tm, LANES), lambda i: (i, 0))),
        compiler_params=pltpu.CompilerParams(dimension_semantics=("arbitrary",), vmem_limit_bytes=32 * MIB),
        name="norm_dt",
    )(x, g_mix, w_dt)
    (q,) = _proj(hn, w_main, "q", "bf16")
    k, kb = _proj(hn, w_main, "k", "f32_bf16")
    v, vb, vt = _proj(hn, w_main, "v", "f32_bf16_t")
    (z,) = _proj(hn, w_main, "z", "f32")
    (xbc,) = _proj(hn, w_main, "xbc", "f32")
    (ga,) = _proj(hn, w_main, "ga", "f32")
    (gs,) = _proj(hn, w_main, "gs", "f32")
    return q, k, v, kb, vb, vt, z, xbc, ga, gs, dt


def _t5_bucket(rel):
    nb = N_BUCKETS // 2
    max_exact = nb // 2
    ret = jnp.where(rel > 0, nb, 0)
    n = jnp.abs(rel)
    large = max_exact + (jnp.log(jnp.maximum(n, 1).astype(jnp.float32) / max_exact)
                         / math.log(MAX_DISTANCE / max_exact) * (nb - max_exact)).astype(jnp.int32)
    large = jnp.minimum(large, nb - 1)
    return ret + jnp.where(n < max_exact, n, large)


def _rel_bias_table(rel_bias, q_pos, k_pos):
    bucket = _t5_bucket(k_pos[None, :] - q_pos[:, None])
    return jnp.transpose(_lookup(rel_bias.astype(F32).T, bucket), (2, 0, 1))


def _toeplitz_bias(rel_bias, tq, d):
    n = 2 * tq
    rel = jnp.arange(n, dtype=jnp.int32) - (tq - 1) - d * tq
    w = _lookup(rel_bias.astype(F32).T, _t5_bucket(rel)).T
    skew = jnp.tile(w, (1, tq))[:, :tq * (n - 1)].reshape(w.shape[0], tq, n - 1)
    return skew[:, :, tq - 1:]


def _lookup(table, idx):
    n = table.shape[-1]
    hit = idx[..., None] == jnp.arange(n, dtype=idx.dtype)
    hit = hit.reshape(idx.shape + (1,) * (table.ndim - 1) + (n,))
    return jnp.sum(jnp.where(hit, table, jnp.zeros((), table.dtype)), axis=-1)


def _split_maps(qh):
    lane = lax.broadcasted_iota(jnp.int32, qh.shape, 1)
    q1 = jnp.where(lane < ATT_HEAD_DIM, qh, 0.0) * ATT_SCALE
    q2 = jnp.where(lane >= ATT_HEAD_DIM, qh, 0.0) * ATT_SCALE
    return q1.astype(BF16), q2.astype(BF16)


def _subln(o, lam_unused, sub):
    return (_rms(o) * sub) * (1.0 - LAM_INIT)


def _attn_prompt_kernel(qi_ref, kj_ref, q_ref, k_ref, vt_ref, bias_ref, far_ref, lam_ref, subt_ref, o_ref,
                        qs_ref, m_ref, l_ref, acc_ref):
    s = pl.program_id(0)
    qi = qi_ref[s]
    kj = kj_ref[s]
    tq = q_ref.shape[0]
    tk = k_ref.shape[0]

    @pl.when(kj == 0)
    def _():
        for h in range(N_ATT_HEADS):
            q1, q2 = _split_maps(q_ref[:, h * HEAD_W:(h + 1) * HEAD_W].astype(F32))
            qs_ref[2 * h] = q1
            qs_ref[2 * h + 1] = q2
        m_ref[...] = jnp.full(m_ref.shape, NEG_INF, F32)
        l_ref[...] = jnp.zeros(l_ref.shape, F32)
        acc_ref[...] = jnp.zeros(acc_ref.shape, F32)

    def step(mode):
        if mode == "diag":
            key = lax.broadcasted_iota(jnp.int32, (tk, tq), 0)
            qry = lax.broadcasted_iota(jnp.int32, (tk, tq), 1)
            shift = CHUNK.bit_length() - 1
            visible = jnp.right_shift(key, shift) <= jnp.right_shift(qry, shift)
        for h in range(N_ATT_HEADS):
            kh = k_ref[:, h * HEAD_W:(h + 1) * HEAD_W]
            vth = vt_ref[h * HEAD_W:(h + 1) * HEAD_W, :]
            for c in range(2):
                idx = 2 * h + c
                for qb in range(tq // ATT_QB):
                    ql = slice(qb * ATT_QB, (qb + 1) * ATT_QB)
                    sc = _dot_nt(kh, qs_ref[idx, ql, :])
                    if mode != "far":
                        sc = sc + bias_ref[0, h, :, ql]
                    if mode == "diag":
                        sc = jnp.where(visible[:, ql], sc, NEG_INF)
                    m_old = m_ref[idx, :, ql]
                    col_max = jnp.max(sc, axis=0, keepdims=True)
                    if mode == "far":
                        m_new = jnp.maximum(m_old, col_max + far_ref[h])
                        offset = m_new - far_ref[h]
                    else:
                        m_new = jnp.maximum(m_old, col_max)
                        offset = m_new
                    alpha = jnp.exp(m_old - m_new)
                    p = jnp.exp(sc - offset)
                    l_ref[idx, :, ql] = alpha * l_ref[idx, :, ql] + jnp.sum(p, axis=0, keepdims=True)
                    acc_ref[idx, :, ql] = alpha * acc_ref[idx, :, ql] + _dot(vth, p.astype(BF16))
                    m_ref[idx, :, ql] = m_new

    @pl.when(kj == qi)
    def _():
        step("diag")

    @pl.when(kj == qi - 1)
    def _():
        step("near")

    @pl.when(kj < qi - 1)
    def _():
        step("far")

    @pl.when(kj == qi)
    def _():
        lam = lam_ref[:, :1]
        subt = subt_ref[...]
        for h in range(N_ATT_HEADS):
            ot = acc_ref[2 * h] / l_ref[2 * h] - lam * (acc_ref[2 * h + 1] / l_ref[2 * h + 1])
            ms = jnp.mean(ot * ot, axis=0, keepdims=True)
            ont = ((ot * lax.rsqrt(ms + EPS)) * subt) * (1.0 - LAM_INIT)
            o_ref[:, h * HEAD_W:(h + 1) * HEAD_W] = ont.T.astype(BF16)


def _attn_prompt(q, k, vt, rel_bias, lam_vec, sub):
    t = q.shape[0]
    tq = ATT_TQ
    assert t % tq == 0 and tq % CHUNK == 0 and tq >= MAX_DISTANCE
    nq = t // tq
    qi = np.concatenate([np.full(i + 1, i, np.int32) for i in range(nq)])
    kj = np.concatenate([np.arange(i + 1, dtype=np.int32) for i in range(nq)])
    bias = jnp.stack([_toeplitz_bias(rel_bias, tq, d) for d in range(2)])
    bias = jnp.transpose(bias, (0, 1, 3, 2))
    far = _lookup(rel_bias.astype(F32).T, _t5_bucket(jnp.full((1,), -(tq + 1), jnp.int32)))[0]
    sub = sub.reshape(HEAD_W, 1)
    v = vt

    grid_spec = pltpu.PrefetchScalarGridSpec(
        num_scalar_prefetch=2,
        grid=(qi.shape[0],),
        in_specs=[pl.BlockSpec((tq, ATT_WIDTH), lambda s, qi, kj: (qi[s], 0)),
                  pl.BlockSpec((tq, ATT_WIDTH), lambda s, qi, kj: (kj[s], 0)),
                  pl.BlockSpec((ATT_WIDTH, tq), lambda s, qi, kj: (0, kj[s])),
                  pl.BlockSpec((1, N_ATT_HEADS, tq, tq),
                               lambda s, qi, kj: (jnp.minimum(qi[s] - kj[s], 1), 0, 0, 0)),
                  pl.BlockSpec(memory_space=pltpu.SMEM),
                  pl.BlockSpec((1, HEAD_W), lambda s, qi, kj: (0, 0)),
                  pl.BlockSpec((HEAD_W, 1), lambda s, qi, kj: (0, 0))],
        out_specs=pl.BlockSpec((tq, ATT_WIDTH), lambda s, qi, kj: (qi[s], 0)),
        scratch_shapes=[pltpu.VMEM((2 * N_ATT_HEADS, tq, HEAD_W), BF16),
                        pltpu.VMEM((2 * N_ATT_HEADS, 1, tq), F32),
                        pltpu.VMEM((2 * N_ATT_HEADS, 1, tq), F32),
                        pltpu.VMEM((2 * N_ATT_HEADS, HEAD_W, tq), F32)])
    return pl.pallas_call(
        _attn_prompt_kernel,
        out_shape=jax.ShapeDtypeStruct((t, ATT_WIDTH), BF16),
        grid_spec=grid_spec,
        compiler_params=pltpu.CompilerParams(dimension_semantics=("arbitrary",), vmem_limit_bytes=40 * MIB),
        name="attn_prompt",
    )(jnp.asarray(qi), jnp.asarray(kj), q, k, v, bias, far, lam_vec, sub)


def _attn_sample_kernel(q_ref, kn_ref, vn_ref, ck_hbm, cv_hbm, bc_ref, bn_ref, mc_ref, mn_ref, lam_ref,
                        sub_ref, o_ref, kbuf_ref, vbuf_ref, sem, *, n_batch):
    b = pl.program_id(0)
    slot = b % 2

    def cache_copies(bb, into):
        copies = []
        for h in range(N_ATT_HEADS):
            copies.append(pltpu.make_async_copy(ck_hbm.at[bb, :, h, :], kbuf_ref.at[into, h], sem.at[0, into]))
            copies.append(pltpu.make_async_copy(cv_hbm.at[bb, :, h, :], vbuf_ref.at[into, h], sem.at[1, into]))
        return copies

    @pl.when(b == 0)
    def _():
        for cp in cache_copies(0, 0):
            cp.start()

    @pl.when(b + 1 < n_batch)
    def _():
        for cp in cache_copies(b + 1, 1 - slot):
            cp.start()

    for cp in cache_copies(b, slot):
        cp.wait()

    lam = lam_ref[...]
    sub = sub_ref[...]
    vis_c = mc_ref[...] > 0.5
    vis_n = mn_ref[...] > 0.5
    for h in range(N_ATT_HEADS):
        hs = slice(h * HEAD_W, (h + 1) * HEAD_W)
        qmaps = _split_maps(q_ref[:, hs].astype(F32))
        kc = kbuf_ref[slot, h].astype(BF16)
        vc = vbuf_ref[slot, h].astype(BF16)
        kn = kn_ref[:, hs]
        vn = vn_ref[:, hs]
        probs = []
        for c in range(2):
            sc = jnp.where(vis_c, _dot_nt(qmaps[c], kc) + bc_ref[h], NEG_INF)
            sn = jnp.where(vis_n, _dot_nt(qmaps[c], kn) + bn_ref[h], NEG_INF)
            m = jnp.maximum(jnp.max(sc, axis=-1, keepdims=True), jnp.max(sn, axis=-1, keepdims=True))
            pc = jnp.exp(sc - m)
            pn = jnp.exp(sn - m)
            den = jnp.sum(pc, axis=-1, keepdims=True) + jnp.sum(pn, axis=-1, keepdims=True)
            probs.append((pc / den, pn / den))
        wc = probs[0][0] - lam[:, :1] * probs[1][0]
        wn = probs[0][1] - lam[:, :1] * probs[1][1]
        o = _dot(wc.astype(BF16), vc) + _dot(wn.astype(BF16), vn)
        o_ref[:, hs] = _subln(o, None, sub).astype(BF16)


def _attn_sample(q, kn, vn, cache_k, cache_v, rel_bias, lam_vec, sub):
    bsz, past = cache_k.shape[:2]
    seq = q.shape[0] // bsz
    q_pos = past + jnp.arange(seq, dtype=jnp.int32)
    k_pos = jnp.arange(past + seq, dtype=jnp.int32)
    bias = _rel_bias_table(rel_bias, q_pos, k_pos)
    visible = ((k_pos[None, :] // CHUNK) <= (q_pos[:, None] // CHUNK)).astype(F32)
    const = lambda *shape: pl.BlockSpec(shape, lambda b: (0,) * len(shape))
    row = pl.BlockSpec((seq, ATT_WIDTH), lambda b: (b, 0))
    cache = pl.BlockSpec(memory_space=pl.ANY)
    head_major = pltpu.VMEM((2, N_ATT_HEADS, past, HEAD_W), F32)
    return pl.pallas_call(
        functools.partial(_attn_sample_kernel, n_batch=bsz),
        out_shape=jax.ShapeDtypeStruct(q.shape, BF16),
        grid=(bsz,),
        in_specs=[row, row, row, cache, cache, const(N_ATT_HEADS, seq, past), const(N_ATT_HEADS, seq, seq),
                  const(seq, past), const(seq, seq), const(1, HEAD_W), const(1, HEAD_W)],
        out_specs=row,
        scratch_shapes=[head_major, head_major, pltpu.SemaphoreType.DMA((2, 2))],
        compiler_params=pltpu.CompilerParams(dimension_semantics=("arbitrary",), vmem_limit_bytes=40 * MIB),
        name="attn_sample",
    )(q, kn, vn, cache_k, cache_v, bias[:, :, :past], bias[:, :, past:], visible[:, :past], visible[:, past:],
      lam_vec, sub)


def _ssd_constants(l):
    hl = HEADS_PER_GROUP * l
    lp = SSD_LP
    e_head = np.zeros((LANES, SSM_D_INNER), np.float32)
    for h in range(N_SSM_HEADS):
        e_head[h, h * SSM_HEAD_DIM:(h + 1) * SSM_HEAD_DIM] = 1.0
    e_grp = np.zeros((N_SSM_GROUPS, LANES, hl), np.float32)
    for g in range(N_SSM_GROUPS):
        for r in range(HEADS_PER_GROUP):
            e_grp[g, g * HEADS_PER_GROUP + r, r * l:(r + 1) * l] = 1.0
    tile8 = np.zeros((lp, hl), np.float32)
    for r in range(HEADS_PER_GROUP):
        tile8[np.arange(l), r * l + np.arange(l)] = 1.0
    causal = np.zeros((l, hl), np.float32)
    for r in range(HEADS_PER_GROUP):
        causal[:, r * l:(r + 1) * l] = np.tril(np.ones((l, l), np.float32))
    tri = np.tril(np.ones((l, l), np.float32))
    ones = np.ones((l, l), np.float32)
    bmask = np.zeros((hl, GROUP_W), np.float32)
    for r in range(HEADS_PER_GROUP):
        bmask[r * l:(r + 1) * l, r * SSM_HEAD_DIM:(r + 1) * SSM_HEAD_DIM] = 1.0
    as_bf = lambda a: jnp.asarray(a, BF16)
    return (as_bf(e_head), as_bf(e_grp), as_bf(tile8), jnp.asarray(causal), as_bf(tri), as_bf(ones),
            as_bf(bmask))


def _ssd_kernel(xbc_ref, z_ref, dt_ref, hist_ref, st0_ref, cw_ref, cb_ref, dtb_ref, alog_ref, dskip_ref,
                norm_ref, eh_ref, eg_ref, t8_ref, caus_ref, tri_ref, ones_ref, bmask_ref,
                yz_ref, st_ref, buf_ref, state_ref):
    c = pl.program_id(1)
    l = xbc_ref.shape[0]
    lp = SSD_LP
    hist_rows = hist_ref.shape[0]

    @pl.when(c == 0)
    def _():
        buf_ref[0:hist_rows, :] = hist_ref[...]
        state_ref[...] = st0_ref[...]

    u = xbc_ref[...]
    buf_ref[hist_rows:hist_rows + l, :] = u
    conv = cb_ref[...] + cw_ref[CONV_WIDTH - 1:CONV_WIDTH, :] * u
    for w in range(CONV_WIDTH - 1):
        shift = CONV_WIDTH - 1 - w
        conv = conv + cw_ref[w:w + 1, :] * buf_ref[hist_rows - shift:hist_rows - shift + l, :]
    buf_ref[0:hist_rows, :] = buf_ref[l:l + hist_rows, :]
    xc = conv * _sigmoid(conv)
    xs = xc[:, :SSM_D_INNER]
    bm = xc[:, SSM_D_INNER:SSM_D_INNER + N_SSM_GROUPS * D_STATE]
    cm = xc[:, SSM_D_INNER + N_SSM_GROUPS * D_STATE:]

    dt_in = dt_ref[...] + dtb_ref[...]
    dt = jnp.maximum(dt_in, 0.0) + jnp.log(1.0 + jnp.exp(-jnp.abs(dt_in)))
    a = -jnp.exp(alog_ref[...])
    acum = _xdot_l(tri_ref[...], dt * a)
    eh = eh_ref[...]
    dt_e = _xdot_r(dt, eh)
    ac_e = _xdot_r(acum, eh)
    a_last = ac_e[l - 1:l, :]
    ecum = jnp.exp(ac_e)
    xdt = xs * dt_e
    xdtw_b = (xdt * jnp.exp(a_last - ac_e)).astype(BF16)
    xdt_b = xdt.astype(BF16)
    drow = jnp.exp(a_last)
    z = z_ref[...]
    caus = caus_ref[...] > 0.5
    t8 = t8_ref[...]
    t8_mask = t8[0:l, :] > 0
    bmask = bmask_ref[...] > 0
    row_pad = lp - l

    for g in range(N_SSM_GROUPS):
        gs = slice(g * GROUP_W, (g + 1) * GROUP_W)
        ns = slice(g * D_STATE, (g + 1) * D_STATE)
        bm_g = bm[:, ns]
        cm_b = cm[:, ns].astype(BF16)
        xw_g = xdtw_b[:, gs]
        if row_pad:
            bm_g = jnp.concatenate([bm_g, jnp.zeros((row_pad, D_STATE), F32)], axis=0)
            xw_g = jnp.concatenate([xw_g, jnp.zeros((row_pad, GROUP_W), BF16)], axis=0)
        bmt_b = bm_g.T.astype(BF16)
        cb8 = _dot(cm_b, _dot(bmt_b, t8).astype(BF16))
        a1 = _xdot_r(acum, eg_ref[g])
        a2 = _xdot_l(ones_ref[...], jnp.where(t8_mask, a1, 0.0))
        decay = jnp.where(caus, jnp.exp(jnp.where(caus, a1 - a2, 0.0)), 0.0)
        m_b = (cb8 * decay).astype(BF16)
        xg = xdt_b[:, gs]
        bd = jnp.concatenate([xg] * HEADS_PER_GROUP, axis=0)
        bd = jnp.where(bmask, bd, jnp.zeros_like(bd))
        y = _dot(m_b, bd)
        st_g = state_ref[g]
        y = y + _dot(cm_b, st_g.astype(BF16)) * ecum[:, gs]
        y = y + dskip_ref[:, gs] * xs[:, gs]
        state_ref[g] = st_g * drow[:, gs] + _dot(bmt_b, xw_g)
        zg = z[:, gs]
        yz = y * (zg * _sigmoid(zg))
        yz_ref[:, gs] = (_rms(yz) * norm_ref[:, gs]).astype(BF16)

    @pl.when(c == pl.num_programs(1) - 1)
    def _():
        st_ref[...] = state_ref[...]


def _ssd(xbc, z, dt, hist8, st0, conv_w, conv_b, dt_bias, a_log, dskip_e, ssm_norm, bsz, l):
    rows = xbc.shape[0]
    seq = rows // bsz
    assert seq % l == 0 and l % SUBLANES == 0 and l <= SSD_LP and l >= SUBLANES
    nc = seq // l
    consts = _ssd_constants(l)
    rowblk = lambda width: pl.BlockSpec((l, width), lambda b, c: (b * nc + c, 0))
    const = lambda arr: pl.BlockSpec(arr.shape, lambda b, c: (0,) * arr.ndim)
    params = (conv_w, conv_b, dt_bias, a_log, dskip_e, ssm_norm)
    return pl.pallas_call(
        _ssd_kernel,
        out_shape=(jax.ShapeDtypeStruct((rows, SSM_D_INNER), BF16),
                   jax.ShapeDtypeStruct(st0.shape, F32)),
        grid=(bsz, nc),
        in_specs=[rowblk(CONV_DIM), rowblk(SSM_D_INNER), rowblk(LANES),
                  pl.BlockSpec((None,) + hist8.shape[1:], lambda b, c: (b, 0, 0)),
                  pl.BlockSpec((None,) + st0.shape[1:], lambda b, c: (b, 0, 0, 0))]
                 + [const(p) for p in params] + [const(k) for k in consts],
        out_specs=(rowblk(SSM_D_INNER), pl.BlockSpec((None,) + st0.shape[1:], lambda b, c: (b, 0, 0, 0))),
        scratch_shapes=[pltpu.VMEM((SUBLANES + l, CONV_DIM), F32),
                        pltpu.VMEM(st0.shape[1:], F32)],
        compiler_params=pltpu.CompilerParams(dimension_semantics=("arbitrary", "arbitrary"),
                                             vmem_limit_bytes=48 * MIB),
        name="ssd",
    )(xbc, z, dt, hist8, st0, *params, *consts)


def _mix_kernel(on_ref, yz_ref, ga_ref, gs_ref, wa_ref, ws_ref, o_ref):
    att = _dot(on_ref[...], wa_ref[...])
    ssm = _dot(yz_ref[...], ws_ref[...])
    o_ref[...] = (_sigmoid(ga_ref[...]) * att + _sigmoid(gs_ref[...]) * ssm).astype(BF16)


def _mix(on, yz, ga, gs, wa, ws):
    t = on.shape[0]
    tm = TOK_TM
    rowblk = lambda width: pl.BlockSpec((tm, width), lambda i: (i, 0))
    const = lambda arr: pl.BlockSpec(arr.shape, lambda i: (0,) * arr.ndim)
    return pl.pallas_call(
        _mix_kernel,
        out_shape=jax.ShapeDtypeStruct((t, D_MODEL), BF16),
        grid=(t // tm,),
        in_specs=[rowblk(ATT_WIDTH), rowblk(SSM_D_INNER), rowblk(D_MODEL), rowblk(D_MODEL), const(wa), const(ws)],
        out_specs=rowblk(D_MODEL),
        compiler_params=pltpu.CompilerParams(dimension_semantics=("arbitrary",), vmem_limit_bytes=48 * MIB),
        name="mix",
    )(on, yz, ga, gs, wa, ws)


def _pack_bf16_pairs(x):
    w = x.shape[1] // 2
    lo = lax.bitcast_convert_type(x[:, :w].astype(BF16).astype(F32), U32)
    hi = lax.bitcast_convert_type(x[:, w:].astype(BF16).astype(F32), U32)
    return hi | (lo >> 16)


def _unpack_bf16_pairs(words):
    lo = lax.bitcast_convert_type(words << 16, F32)
    hi = lax.bitcast_convert_type(words & jnp.uint32(0xFFFF0000), F32)
    return jnp.concatenate([lo, hi], axis=1).astype(BF16)


def _resid_kernel(x_ref, mixed_ref, wo_ref, g_ref, wr_ref, br_ref, lt_ref, cin_ref,
                  h_ref, xp_ref, route_ref, cnt_ref, carry_ref):
    i = pl.program_id(0)

    @pl.when(i == 0)
    def _():
        carry_ref[...] = cin_ref[...]

    h = x_ref[...] + _dot(mixed_ref[...], wo_ref[...])
    h_ref[...] = h
    xn = _rms(h) * g_ref[...]
    xp_ref[...] = _pack_bf16_pairs(xn)
    logits = _dot(xn.astype(BF16), wr_ref[...]) + br_ref[...]

    lane = lax.broadcasted_iota(jnp.int32, logits.shape, 1)
    lane_f = lane.astype(F32)
    rest = logits
    vals, idxs, sels = [], [], []
    for _ in range(TOP_K):
        m = jnp.max(rest, axis=-1, keepdims=True)
        idx = jnp.min(jnp.where(rest == m, lane_f, float(LANES)), axis=-1, keepdims=True)
        sel = lane_f == idx
        rest = jnp.where(sel, -jnp.inf, rest)
        vals.append(m)
        idxs.append(idx)
        sels.append(sel)
    exps = [jnp.exp(v - vals[0]) for v in vals]
    den = exps[0] + exps[1] + exps[2] + exps[3]
    onehot = jnp.zeros(logits.shape, F32)
    for sel in sels:
        onehot = onehot + jnp.where(sel, 1.0, 0.0)
    before = _dot(lt_ref[...], onehot.astype(BF16)) + carry_ref[...]
    route = jnp.zeros(logits.shape, F32)
    for k in range(TOP_K):
        pos = jnp.sum(jnp.where(sels[k], before, 0.0), axis=-1, keepdims=True)
        route = route + jnp.where(lane == k, idxs[k], 0.0)
        route = route + jnp.where(lane == TOP_K + k, exps[k] / den, 0.0)
        route = route + jnp.where(lane == 2 * TOP_K + k, pos, 0.0)
    route_ref[...] = route
    carry_ref[...] = carry_ref[...] + jnp.sum(onehot, axis=0, keepdims=True)
    cnt_ref[...] = carry_ref[...]


def _resid(x, mixed, wo, g_ffn, wr, br, counts_in):
    t = x.shape[0]
    tm = TOK_TM
    lt = jnp.asarray(np.tril(np.ones((tm, tm), np.float32), -1), BF16)
    rowblk = lambda width: pl.BlockSpec((tm, width), lambda i: (i, 0))
    const = lambda arr: pl.BlockSpec(arr.shape, lambda i: (0,) * arr.ndim)
    return pl.pallas_call(
        _resid_kernel,
        out_shape=(jax.ShapeDtypeStruct((t, D_MODEL), F32),
                   jax.ShapeDtypeStruct((t, D_MODEL // 2), U32),
                   jax.ShapeDtypeStruct((t, LANES), F32),
                   jax.ShapeDtypeStruct((1, LANES), F32)),
        grid=(t // tm,),
        in_specs=[rowblk(D_MODEL), rowblk(D_MODEL), const(wo), const(g_ffn), const(wr), const(br), const(lt),
                  const(counts_in)],
        out_specs=(rowblk(D_MODEL), rowblk(D_MODEL // 2), rowblk(LANES), pl.BlockSpec((1, LANES), lambda i: (0, 0))),
        scratch_shapes=[pltpu.VMEM((1, LANES), F32)],
        compiler_params=pltpu.CompilerParams(dimension_semantics=("arbitrary",), vmem_limit_bytes=48 * MIB),
        name="resid_route",
    )(x, mixed, wo, g_ffn, wr, br, lt, counts_in)


def _row_copy(src, src_row, dst, dst_row, sem):
    return pltpu.make_async_copy(src.at[pl.ds(src_row, 1)], dst.at[pl.ds(dst_row, 1)], sem)


def _dispatch_kernel(dest_ref, xp_ref, *rest, first):
    if first:
        xs_ref, sem, zero_ref = rest

        @pl.when(pl.program_id(0) == 0)
        def _():
            zero_ref[...] = jnp.zeros(zero_ref.shape, zero_ref.dtype)
            blk = zero_ref.shape[0]

            def fill(b):
                return pltpu.make_async_copy(zero_ref, xs_ref.at[pl.ds(pl.multiple_of(b * blk, blk), blk)], sem)

            def start_fill(b, carry):
                fill(b).start()
                return carry

            def wait_fill(b, carry):
                fill(b).wait()
                return carry

            lax.fori_loop(0, xs_ref.shape[0] // blk, start_fill, 0)
            lax.fori_loop(0, xs_ref.shape[0] // blk, wait_fill, 0)
    else:
        _, xs_ref, sem = rest

    def issue(t, carry):
        for k in range(TOP_K):
            _row_copy(xp_ref, t, xs_ref, dest_ref[t * TOP_K + k], sem).start()
        return carry

    lax.fori_loop(0, TOK_TM, issue, 0, unroll=DMA_UNROLL)

    def drain(t, carry):
        for k in range(TOP_K):
            _row_copy(xp_ref, 0, xs_ref, 0, sem).wait()
        return carry

    lax.fori_loop(0, TOK_TM, drain, 0, unroll=DMA_UNROLL)


def _dispatch(dest_flat, xp, xs, n_rows):
    t, width = xp.shape
    first = xs is None
    assert n_rows % TOK_TM == 0
    in_specs = [pl.BlockSpec((TOK_TM * TOP_K,), lambda i: (i,), memory_space=pltpu.SMEM),
                pl.BlockSpec((TOK_TM, width), lambda i: (i, 0))]
    scratch = [pltpu.SemaphoreType.DMA(())]
    operands = [dest_flat, xp]
    if first:
        scratch.append(pltpu.VMEM((TOK_TM, width), xp.dtype))
    else:
        in_specs.append(pl.BlockSpec(memory_space=pl.ANY))
        operands.append(xs)
    return pl.pallas_call(
        functools.partial(_dispatch_kernel, first=first),
        out_shape=jax.ShapeDtypeStruct((n_rows, width), xp.dtype),
        grid=(t // TOK_TM,),
        in_specs=in_specs,
        out_specs=pl.BlockSpec(memory_space=pl.ANY),
        scratch_shapes=scratch,
        input_output_aliases={} if first else {2: 0},
        compiler_params=pltpu.CompilerParams(dimension_semantics=("arbitrary",)),
        name="dispatch_first" if first else "dispatch",
    )(*operands)


def _experts_kernel(sbe_ref, sbs_ref, sbr_ref, nsb_ref, xs_ref, wg_ref, wu_ref, wd_ref, bg_ref, bu_ref,
                    bd_ref, y_ref, xw_ref, xb_ref, acc_ref, wgb_ref, wub_ref, wdb_ref, sem_in, sem_out):
    s = pl.program_id(0)
    f = pl.program_id(1)
    nf = pl.num_programs(1)
    n_sub = SUPER_ROWS // ROW_BLK

    @pl.when(s < nsb_ref[0])
    def _():
        start = pl.multiple_of(sbs_ref[s], ROW_BLK)
        rows = sbr_ref[s]

        def x_copy(sb, slot):
            src = xs_ref.at[pl.ds(pl.multiple_of(sbs_ref[sb], ROW_BLK), SUPER_ROWS)]
            return pltpu.make_async_copy(src, xw_ref.at[slot], sem_in.at[slot])

        @pl.when(f == 0)
        def _():
            slot = s % 2

            @pl.when(s == 0)
            def _():
                x_copy(0, 0).start()

            x_copy(s, slot).wait()

            @pl.when(s + 1 < nsb_ref[0])
            def _():
                x_copy(s + 1, 1 - slot).start()

            xb_ref[...] = _unpack_bf16_pairs(xw_ref[slot])
            acc_ref[...] = jnp.zeros(acc_ref.shape, F32)

        wgb_ref[...] = wg_ref[0].astype(BF16)
        wub_ref[...] = wu_ref[0].astype(BF16)
        wdb_ref[...] = wd_ref[0].astype(BF16)
        half_rows = SUPER_ROWS // 2
        for half in range(2):
            rs = slice(half * half_rows, (half + 1) * half_rows)
            x = xb_ref[rs, :]
            gate = _dot(x, wgb_ref[...]) + bg_ref[0]
            up = _dot(x, wub_ref[...]) + bu_ref[0]
            gate = jnp.minimum(gate, SWIGLU_LIMIT)
            up = jnp.clip(up, -SWIGLU_LIMIT, SWIGLU_LIMIT)
            act = (up + 1.0) * gate * _sigmoid(SWIGLU_ALPHA * gate)
            acc_ref[rs, :] += _dot(act.astype(BF16), wdb_ref[...])

        @pl.when(f == nf - 1)
        def _():
            def out_copy(sub):
                rs = pl.ds(sub * ROW_BLK, ROW_BLK)
                return pltpu.make_async_copy(acc_ref.at[rs], y_ref.at[pl.ds(start + sub * ROW_BLK, ROW_BLK)],
                                             sem_out)

            for sub in range(n_sub):
                @pl.when(sub * ROW_BLK < rows)
                def _():
                    rs = slice(sub * ROW_BLK, (sub + 1) * ROW_BLK)
                    acc_ref[rs, :] += bd_ref[0]
                    out_copy(sub).start()

            for sub in range(n_sub):
                @pl.when(sub * ROW_BLK < rows)
                def _():
                    out_copy(sub).wait()

    @pl.when((s == pl.num_programs(0) - 1) & (f == nf - 1))
    def _():
        zero_ref = acc_ref.at[pl.ds(0, ROW_BLK)]
        zero_ref[...] = jnp.zeros(zero_ref.shape, F32)
        n_blocks = y_ref.shape[0] // ROW_BLK

        def tail_copy(b):
            return pltpu.make_async_copy(zero_ref, y_ref.at[pl.ds(pl.multiple_of(b * ROW_BLK, ROW_BLK), ROW_BLK)],
                                         sem_out)

        def issue(b, carry):
            tail_copy(b).start()
            return carry

        def drain(b, carry):
            tail_copy(b).wait()
            return carry

        lax.fori_loop(nsb_ref[1], n_blocks, issue, 0)
        lax.fori_loop(nsb_ref[1], n_blocks, drain, 0)


def _experts(sb_expert, sb_start, sb_rows, n_sb, xs, w_gate_up, b_gate_up, w_down, b_down, n_rows):
    n_super = sb_expert.shape[0]
    nf = D_FF // FF_TILE

    def widx(s, f, sbe, sbs, sbr, nsb):
        live = s < nsb[0]
        return sbe[s], jnp.where(live, f, nf - 1)

    def gate_map(s, f, *pref):
        e, ff = widx(s, f, *pref)
        return (e, 0, ff)

    def up_map(s, f, *pref):
        e, ff = widx(s, f, *pref)
        return (e, 0, nf + ff)

    def down_map(s, f, *pref):
        e, ff = widx(s, f, *pref)
        return (e, ff, 0)

    def bias_map(s, f, *pref):
        return (widx(s, f, *pref)[0], 0, 0)

    b_gu3 = b_gate_up.reshape(N_EXPERTS, 1, 2 * D_FF)
    b_dn3 = b_down.reshape(N_EXPERTS, 1, D_MODEL)
    grid_spec = pltpu.PrefetchScalarGridSpec(
        num_scalar_prefetch=4,
        grid=(n_super, nf),
        in_specs=[pl.BlockSpec(memory_space=pl.ANY),
                  pl.BlockSpec((1, D_MODEL, FF_TILE), gate_map),
                  pl.BlockSpec((1, D_MODEL, FF_TILE), up_map),
                  pl.BlockSpec((1, FF_TILE, D_MODEL), down_map),
                  pl.BlockSpec((1, 1, FF_TILE), gate_map),
                  pl.BlockSpec((1, 1, FF_TILE), up_map),
                  pl.BlockSpec((1, 1, D_MODEL), bias_map)],
        out_specs=pl.BlockSpec(memory_space=pl.ANY),
        scratch_shapes=[pltpu.VMEM((2, SUPER_ROWS, D_MODEL // 2), U32),
                        pltpu.VMEM((SUPER_ROWS, D_MODEL), BF16),
                        pltpu.VMEM((SUPER_ROWS, D_MODEL), F32),
                        pltpu.VMEM((D_MODEL, FF_TILE), BF16),
                        pltpu.VMEM((D_MODEL, FF_TILE), BF16),
                        pltpu.VMEM((FF_TILE, D_MODEL), BF16),
                        pltpu.SemaphoreType.DMA((2,)),
                        pltpu.SemaphoreType.DMA(())])
    return pl.pallas_call(
        _experts_kernel,
        out_shape=jax.ShapeDtypeStruct((n_rows, D_MODEL), F32),
        grid_spec=grid_spec,
        compiler_params=pltpu.CompilerParams(dimension_semantics=("arbitrary", "arbitrary"),
                                             vmem_limit_bytes=56 * MIB),
        name="experts",
    )(sb_expert, sb_start, sb_rows, n_sb, xs, w_gate_up, w_gate_up, w_down, b_gu3, b_gu3, b_dn3)


def _combine_kernel(dest_ref, dest_next_ref, h_ref, route_ref, p_ref, y_ref, gple_ref, wpg_ref, wpp_ref,
                    gfin_ref, o_ref, gbuf_ref, sem, *, n_tiles):
    i = pl.program_id(0)
    slot = i % 2

    def gather_tile(d_ref, into):
        def issue(t, carry):
            for k in range(TOP_K):
                _row_copy(y_ref, d_ref[t * TOP_K + k], gbuf_ref.at[into].at[k], t, sem.at[into]).start()
            return carry

        lax.fori_loop(0, TOK_TM, issue, 0, unroll=DMA_UNROLL)

    @pl.when(i == 0)
    def _():
        gather_tile(dest_ref, 0)

    @pl.when(i + 1 < n_tiles)
    def _():
        gather_tile(dest_next_ref, 1 - slot)

    def drain(t, carry):
        for k in range(TOP_K):
            _row_copy(y_ref, 0, gbuf_ref.at[slot].at[k], 0, sem.at[slot]).wait()
        return carry

    lax.fori_loop(0, TOK_TM, drain, 0, unroll=DMA_UNROLL)

    route = route_ref[...]
    h = h_ref[...]
    for k in range(TOP_K):
        h = h + route[:, TOP_K + k:TOP_K + k + 1] * gbuf_ref[slot, k]
    xn = (_rms(h) * gple_ref[...]).astype(BF16)
    gate = _sigmoid(_dot(xn, wpg_ref[...]))
    h = h + gate * _dot(p_ref[...].astype(BF16), wpp_ref[...])
    o_ref[...] = _rms(h) * gfin_ref[...]


def _combine(dest_flat, h, route, p, y_sorted, g_ple, wpg, wpp, g_final):
    t = h.shape[0]
    tm = TOK_TM
    rowblk = lambda width: pl.BlockSpec((tm, width), lambda i: (i, 0))
    const = lambda arr: pl.BlockSpec(arr.shape, lambda i: (0,) * arr.ndim)
    return pl.pallas_call(
        functools.partial(_combine_kernel, n_tiles=t // tm),
        out_shape=jax.ShapeDtypeStruct((t, D_MODEL), F32),
        grid=(t // tm,),
        in_specs=[pl.BlockSpec((tm * TOP_K,), lambda i: (i,), memory_space=pltpu.SMEM),
                  pl.BlockSpec((tm * TOP_K,), lambda i: (jnp.minimum(i + 1, t // tm - 1),),
                               memory_space=pltpu.SMEM),
                  rowblk(D_MODEL), rowblk(LANES), rowblk(PLE_DIM),
                  pl.BlockSpec(memory_space=pl.ANY),
                  const(g_ple), const(wpg), const(wpp), const(g_final)],
        out_specs=rowblk(D_MODEL),
        scratch_shapes=[pltpu.VMEM((2, TOP_K, tm, D_MODEL), F32), pltpu.SemaphoreType.DMA((2,))],
        compiler_params=pltpu.CompilerParams(dimension_semantics=("arbitrary",), vmem_limit_bytes=52 * MIB),
        name="combine",
    )(dest_flat, dest_flat, h, route, p, y_sorted, g_ple, wpg, wpp, g_final)


def _super_blocks(counts, n_super):
    padded = (counts + ROW_BLK - 1) // ROW_BLK * ROW_BLK
    pad_start = jnp.cumsum(padded) - padded
    per_expert = (counts + SUPER_ROWS - 1) // SUPER_ROWS
    sb_end = jnp.cumsum(per_expert)
    n_sb = sb_end[-1]
    s = jnp.arange(n_super, dtype=jnp.int32)
    s_live = jnp.minimum(s, jnp.maximum(n_sb - 1, 0))
    expert = jnp.sum(sb_end[None, :] <= s_live[:, None], axis=1).astype(jnp.int32)
    expert = jnp.minimum(expert, N_EXPERTS - 1)
    within = s_live - _lookup(sb_end - per_expert, expert)
    start = _lookup(pad_start, expert) + within * SUPER_ROWS
    rows = jnp.where(s < n_sb, jnp.clip(_lookup(counts, expert) - within * SUPER_ROWS, 0, SUPER_ROWS), 0)
    first_unused_block = jnp.sum(padded) // ROW_BLK
    return (pad_start, expert, start.astype(jnp.int32), rows.astype(jnp.int32),
            jnp.stack([n_sb, first_unused_block]).astype(jnp.int32))


def _mixer(x, cache_k, cache_v, st0, conv0, prep, ssd_l):
    bsz, seq, _ = x.shape
    x2 = x.reshape(bsz * seq, D_MODEL)
    q, k, v, kb, vb, vt, z, xbc, ga, gs, dt = _inproj(x2, prep["g_mix"], prep["w_main"], prep["w_dt"])
    if cache_k is None:
        assert bsz == 1
        on = _attn_prompt(q, kb, vt, prep["rel_bias"], prep["lam"], prep["subln"])
    else:
        past = cache_k.shape[1]
        on = _attn_sample(q, kb, vb, cache_k, cache_v, prep["rel_bias"], prep["lam"], prep["subln"])
    hist8 = jnp.pad(conv0, ((0, 0), (SUBLANES - (CONV_WIDTH - 1), 0), (0, 0)))
    st0_t = st0.reshape(bsz, N_SSM_GROUPS, HEADS_PER_GROUP, SSM_HEAD_DIM, D_STATE)
    st0_t = jnp.transpose(st0_t, (0, 1, 4, 2, 3)).reshape(bsz, N_SSM_GROUPS, D_STATE, GROUP_W)
    yz, st_t = _ssd(xbc, z, dt, hist8, st0_t, prep["conv_w"], prep["conv_b"], prep["dt_bias"], prep["a_log"],
                    prep["dskip_e"], prep["ssm_norm"], bsz, ssd_l)
    st_new = st_t.reshape(bsz, N_SSM_GROUPS, D_STATE, HEADS_PER_GROUP, SSM_HEAD_DIM)
    st_new = jnp.transpose(st_new, (0, 1, 3, 4, 2)).reshape(1, bsz, N_SSM_HEADS, SSM_HEAD_DIM, D_STATE)
    mixed = _mix(on, yz, ga, gs, prep["w_attn_out"], prep["w_ssm_out"])
    k_rows = k.reshape(1, bsz, seq, N_ATT_HEADS, HEAD_W)
    v_rows = v.reshape(1, bsz, seq, N_ATT_HEADS, HEAD_W)
    conv_new = xbc.reshape(bsz, seq, CONV_DIM)[:, seq - (CONV_WIDTH - 1):].reshape(1, bsz, CONV_WIDTH - 1, CONV_DIM)
    return x2, mixed, k_rows, v_rows, st_new, conv_new


def kernel(x_prompt, x_sample, cache_k, cache_v, state_ssm, state_conv, p_prompt, p_sample, rel_bias, w_in,
           lambda_q1, lambda_k1, lambda_q2, lambda_k2, attn_subln, w_attn_out, conv_w, conv_b, dt_bias, a_log,
           d_skip, ssm_norm, w_ssm_out, w_o, g_mix, g_ffn, w_router, b_router, w_gate_up, b_gate_up, w_down,
           b_down, g_ple, w_ple_gate, w_ple_proj, g_final):
    w = w_in[0]
    c_dt = 3 * ATT_WIDTH + SSM_D_INNER + CONV_DIM
    w_main = jnp.concatenate([w[:, :c_dt], w[:, c_dt + N_SSM_HEADS:]], axis=1).astype(BF16)
    w_main = jnp.transpose(w_main.reshape(D_MODEL, -1, INPROJ_TN), (1, 0, 2))
    w_dt = jnp.pad(w[:, c_dt:c_dt + N_SSM_HEADS], ((0, 0), (0, LANES - N_SSM_HEADS))).astype(BF16)
    lam = (jnp.exp(jnp.sum(lambda_q1[0] * lambda_k1[0]).astype(F32))
           - jnp.exp(jnp.sum(lambda_q2[0] * lambda_k2[0]).astype(F32)) + LAM_INIT)
    pad_heads = lambda v: jnp.pad(v.reshape(1, N_SSM_HEADS), ((0, 0), (0, LANES - N_SSM_HEADS)))
    prep = dict(
        g_mix=g_mix, w_main=w_main, w_dt=w_dt, rel_bias=rel_bias,
        lam=jnp.full((1, HEAD_W), lam, F32), subln=attn_subln,
        conv_w=conv_w[0], conv_b=conv_b, dt_bias=pad_heads(dt_bias[0]), a_log=pad_heads(a_log[0]),
        dskip_e=jnp.repeat(d_skip[0], SSM_HEAD_DIM).reshape(1, SSM_D_INNER), ssm_norm=ssm_norm,
        w_attn_out=w_attn_out[0].astype(BF16), w_ssm_out=w_ssm_out[0].astype(BF16))
    wo = w_o[0].astype(BF16)
    wr = jnp.pad(w_router[0], ((0, 0), (0, LANES - N_EXPERTS))).astype(BF16)
    br = jnp.pad(b_router, ((0, 0), (0, LANES - N_EXPERTS)), constant_values=NEG_INF)
    wpg = w_ple_gate[0].astype(BF16)
    wpp = w_ple_proj[0].astype(BF16)

    bp, sp, _ = x_prompt.shape
    bs, ss, _ = x_sample.shape
    zeros_state = jnp.zeros((bp, N_SSM_HEADS, SSM_HEAD_DIM, D_STATE), F32)
    zeros_conv = jnp.zeros((bp, CONV_WIDTH - 1, CONV_DIM), F32)
    xp2, mixed_p, k_p, v_p, ssm_p, conv_p = _mixer(x_prompt, None, None, zeros_state, zeros_conv, prep, SSD_L_PROMPT)
    xs2, mixed_s, k_s, v_s, ssm_s, conv_s = _mixer(x_sample, cache_k[0], cache_v[0], state_ssm[0], state_conv[0],
                                                   prep, ss)

    zero_counts = jnp.zeros((1, LANES), F32)
    h_p, xpk_p, route_p, cnt_p = _resid(xp2, mixed_p, wo, g_ffn, wr, br, zero_counts)
    h_s, xpk_s, route_s, cnt = _resid(xs2, mixed_s, wo, g_ffn, wr, br, cnt_p)
    n_tok = xp2.shape[0] + xs2.shape[0]
    n_slots = n_tok * TOP_K
    n_super = -(-n_slots // SUPER_ROWS) + N_EXPERTS
    n_rows = n_slots + N_EXPERTS * ROW_BLK
    counts = cnt[0, :N_EXPERTS].astype(jnp.int32)
    pad_start, sb_expert, sb_start, sb_rows, n_sb = _super_blocks(counts, n_super)

    def dest_of(route):
        expert = route[:, :TOP_K].astype(jnp.int32)
        pos = route[:, 2 * TOP_K:3 * TOP_K].astype(jnp.int32)
        return (_lookup(pad_start, expert) + pos).astype(jnp.int32).reshape(-1)

    dest_p = dest_of(route_p)
    dest_s = dest_of(route_s)
    xs_sorted = _dispatch(dest_p, xpk_p, None, n_rows + SUPER_ROWS)
    xs_sorted = _dispatch(dest_s, xpk_s, xs_sorted, n_rows + SUPER_ROWS)
    y_sorted = _experts(sb_expert, sb_start, sb_rows, n_sb, xs_sorted, w_gate_up[0], b_gate_up[0], w_down[0],
                        b_down[0], n_rows)

    y_p = _combine(dest_p, h_p, route_p, p_prompt[0].reshape(-1, PLE_DIM), y_sorted, g_ple, wpg, wpp,
                   g_final.reshape(1, D_MODEL))
    y_s = _combine(dest_s, h_s, route_s, p_sample[0].reshape(-1, PLE_DIM), y_sorted, g_ple, wpg, wpp,
                   g_final.reshape(1, D_MODEL))
    return (y_p.reshape(bp, sp, D_MODEL), y_s.reshape(bs, ss, D_MODEL), k_p, v_p, ssm_p, conv_p,
            k_s, v_s, ssm_s, conv_s)
```

```python
import functools
import math

import numpy as np
import jax
import jax.numpy as jnp
from jax import lax
from jax.experimental import pallas as pl
from jax.experimental.pallas import tpu as pltpu

F32 = jnp.float32
BF16 = jnp.bfloat16
U32 = jnp.uint32

D_MODEL = 2048
CHUNK = 64
N_ATT_HEADS = 8
ATT_HEAD_DIM = 64
HEAD_W = 2 * ATT_HEAD_DIM
ATT_WIDTH = N_ATT_HEADS * HEAD_W
ATT_SCALE = ATT_HEAD_DIM ** -0.5
N_BUCKETS = 32
MAX_DISTANCE = 128
NEG_INF = -1e30
SSM_D_INNER = 2048
SSM_HEAD_DIM = 64
N_SSM_HEADS = SSM_D_INNER // SSM_HEAD_DIM
N_SSM_GROUPS = 4
HEADS_PER_GROUP = N_SSM_HEADS // N_SSM_GROUPS
GROUP_W = HEADS_PER_GROUP * SSM_HEAD_DIM
D_STATE = 128
CONV_WIDTH = 4
CONV_DIM = SSM_D_INNER + 2 * N_SSM_GROUPS * D_STATE
N_EXPERTS = 32
TOP_K = 4
D_FF = 2048
SWIGLU_LIMIT = 7.0
SWIGLU_ALPHA = 1.702
PLE_DIM = 256
EPS = 1e-6
LAM_INIT = 0.8 - 0.6 * math.exp(-0.3 * 0)

LANES = 128
SUBLANES = 8
MIB = 1024 * 1024

INPROJ_TM = 1024
INPROJ_TN = 512
ATT_TQ = 512
ATT_QB = 128
ATT_KB = 256
SSD_L_PROMPT = 128
SSD_LP = 128
TOK_TM = 256
ROW_BLK = 256
SUPER_ROWS = 1280
FF_TILE = 512
DMA_UNROLL = 8


def _dot(a, b):
    return jnp.dot(a, b, preferred_element_type=F32)


def _dot_nt(a, b):
    return lax.dot_general(a, b, (((1,), (1,)), ((), ())), preferred_element_type=F32)


def _rms(x):
    return x * lax.rsqrt(jnp.mean(x * x, axis=-1, keepdims=True) + EPS)


def _sigmoid(x):
    return 1.0 / (1.0 + jnp.exp(-x))


def _split3(x):
    hi = x.astype(BF16)
    r = x - hi.astype(F32)
    mid = r.astype(BF16)
    lo = (r - mid.astype(F32)).astype(BF16)
    return hi, mid, lo


def _xdot_r(x, c):
    hi, mid, lo = _split3(x)
    return (_dot(hi, c) + _dot(mid, c)) + _dot(lo, c)


def _xdot_l(c, x):
    hi, mid, lo = _split3(x)
    return (_dot(c, hi) + _dot(c, mid)) + _dot(c, lo)


_SEG_WIDTHS = (("q", ATT_WIDTH), ("k", ATT_WIDTH), ("v", ATT_WIDTH), ("z", SSM_D_INNER),
               ("xbc", CONV_DIM), ("ga", D_MODEL), ("gs", D_MODEL))


def _segments():
    segs, first = {}, 0
    for name, width in _SEG_WIDTHS:
        assert width % INPROJ_TN == 0
        segs[name] = (first, width // INPROJ_TN)
        first += width // INPROJ_TN
    return segs, first


def _norm_kernel(x_ref, g_ref, wdt_ref, hn_ref, dt_ref):
    hn = (_rms(x_ref[...]) * g_ref[...]).astype(BF16)
    hn_ref[...] = hn
    dt_ref[...] = _dot(hn, wdt_ref[...])


def _proj_kernel(hn_ref, w_ref, *refs, mode, n_steps):
    acc = _dot(hn_ref[...], w_ref[...])
    if mode == "bf16":
        refs[0][...] = acc.astype(BF16)
    elif mode == "f32":
        refs[0][...] = acc
    else:
        heads_ref, stage_ref, sem = refs[0], refs[-2], refs[-1]
        refs[1][...] = acc.astype(BF16)
        if mode == "heads_bf16_t":
            refs[2][...] = acc.T.astype(BF16)
        i, j = pl.program_id(0), pl.program_id(1)
        step = i * pl.num_programs(1) + j
        slot = step % 2
        tm, tn = acc.shape
        heads_per_tile = tn // HEAD_W

        def copies(row0, head0, into):
            return [pltpu.make_async_copy(stage_ref.at[into, :, hh * HEAD_W:(hh + 1) * HEAD_W],
                                          heads_ref.at[pl.ds(row0, tm), head0 + hh, :], sem.at[into])
                    for hh in range(heads_per_tile)]

        stage_ref[slot] = acc
        for cp in copies(pl.multiple_of(i * tm, tm), j * heads_per_tile, slot):
            cp.start()

        @pl.when(step > 0)
        def _():
            for cp in copies(0, 0, 1 - slot):
                cp.wait()

        @pl.when(step == n_steps - 1)
        def _():
            for cp in copies(0, 0, slot):
                cp.wait()


def _proj(hn, w_main, name, mode):
    t = hn.shape[0]
    tm, tn = min(INPROJ_TM, t), INPROJ_TN
    lo, n = _segments()[0][name]
    assert t % tm == 0 and tn % HEAD_W == 0
    row_major = pl.BlockSpec((tm, tn), lambda i, j: (i, j))
    scratch = []
    if mode in ("bf16", "f32"):
        out_shape = [jax.ShapeDtypeStruct((t, n * tn), BF16 if mode == "bf16" else F32)]
        out_specs = [row_major]
    else:
        out_shape = [jax.ShapeDtypeStruct((t, n * tn // HEAD_W, HEAD_W), F32),
                     jax.ShapeDtypeStruct((t, n * tn), BF16)]
        out_specs = [pl.BlockSpec(memory_space=pl.ANY), row_major]
        scratch = [pltpu.VMEM((2, tm, tn), F32), pltpu.SemaphoreType.DMA((2,))]
        if mode == "heads_bf16_t":
            out_shape.append(jax.ShapeDtypeStruct((n * tn, t), BF16))
            out_specs.append(pl.BlockSpec((tn, tm), lambda i, j: (j, i)))
    return pl.pallas_call(
        functools.partial(_proj_kernel, mode=mode, n_steps=(t // tm) * n),
        out_shape=tuple(out_shape),
        grid=(t // tm, n),
        in_specs=[pl.BlockSpec((tm, D_MODEL), lambda i, j: (i, 0)),
                  pl.BlockSpec((D_MODEL, tn), lambda i, j: (0, lo + j))],
        out_specs=tuple(out_specs),
        scratch_shapes=scratch,
        compiler_params=pltpu.CompilerParams(dimension_semantics=("arbitrary", "arbitrary"),
                                             vmem_limit_bytes=40 * MIB),
        name="proj_" + name,
    )(hn, w_main)


def _inproj(x, g_mix, w_main, w_dt):
    t = x.shape[0]
    tm = TOK_TM
    assert t % tm == 0 and w_main.shape[1] == _segments()[1] * INPROJ_TN
    hn, dt = pl.pallas_call(
        _norm_kernel,
        out_shape=(jax.ShapeDtypeStruct((t, D_MODEL), BF16), jax.ShapeDtypeStruct((t, LANES), F32)),
        grid=(t // tm,),
        in_specs=[pl.BlockSpec((tm, D_MODEL), lambda i: (i, 0)),
                  pl.BlockSpec((1, D_MODEL), lambda i: (0, 0)),
                  pl.BlockSpec((D_MODEL, LANES), lambda i: (0, 0))],
        out_specs=(pl.BlockSpec((tm, D_MODEL), lambda i: (i, 0)), pl.BlockSpec((tm, LANES), lambda i: (i, 0))),
        compiler_params=pltpu.CompilerParams(dimension_semantics=("arbitrary",), vmem_limit_bytes=32 * MIB),
        name="norm_dt",
    )(x, g_mix, w_dt)
    (q,) = _proj(hn, w_main, "q", "bf16")
    k, kb = _proj(hn, w_main, "k", "heads_bf16")
    v, vb, vt = _proj(hn, w_main, "v", "heads_bf16_t")
    (z,) = _proj(hn, w_main, "z", "f32")
    (xbc,) = _proj(hn, w_main, "xbc", "f32")
    (ga,) = _proj(hn, w_main, "ga", "f32")
    (gs,) = _proj(hn, w_main, "gs", "f32")
    return q, k, v, kb, vb, vt, z, xbc, ga, gs, dt


def _t5_bucket(rel):
    nb = N_BUCKETS // 2
    max_exact = nb // 2
    ret = jnp.where(rel > 0, nb, 0)
    n = jnp.abs(rel)
    large = max_exact + (jnp.log(jnp.maximum(n, 1).astype(jnp.float32) / max_exact)
                         / math.log(MAX_DISTANCE / max_exact) * (nb - max_exact)).astype(jnp.int32)
    large = jnp.minimum(large, nb - 1)
    return ret + jnp.where(n < max_exact, n, large)


def _rel_bias_table(rel_bias, q_pos, k_pos):
    bucket = _t5_bucket(k_pos[None, :] - q_pos[:, None])
    return jnp.transpose(_lookup(rel_bias.astype(F32).T, bucket), (2, 0, 1))


def _toeplitz_bias(rel_bias, tq, d):
    n = 2 * tq
    rel = jnp.arange(n, dtype=jnp.int32) - (tq - 1) - d * tq
    w = _lookup(rel_bias.astype(F32).T, _t5_bucket(rel)).T
    skew = jnp.tile(w, (1, tq))[:, :tq * (n - 1)].reshape(w.shape[0], tq, n - 1)
    return skew[:, :, tq - 1:]


def _lookup(table, idx):
    n = table.shape[-1]
    hit = idx[..., None] == jnp.arange(n, dtype=idx.dtype)
    hit = hit.reshape(idx.shape + (1,) * (table.ndim - 1) + (n,))
    return jnp.sum(jnp.where(hit, table, jnp.zeros((), table.dtype)), axis=-1)


def _split_maps(qh):
    lane = lax.broadcasted_iota(jnp.int32, qh.shape, 1)
    q1 = jnp.where(lane < ATT_HEAD_DIM, qh, 0.0) * ATT_SCALE
    q2 = jnp.where(lane >= ATT_HEAD_DIM, qh, 0.0) * ATT_SCALE
    return q1.astype(BF16), q2.astype(BF16)


def _subln(o, lam_unused, sub):
    return (_rms(o) * sub) * (1.0 - LAM_INIT)


def _attn_prompt_kernel(qi_ref, kj_ref, q_ref, k_ref, vt_ref, bias_ref, far_ref, lam_ref, subt_ref, o_ref,
                        qs_ref, m_ref, l_ref, acc_ref):
    s = pl.program_id(0)
    qi = qi_ref[s]
    kj = kj_ref[s]
    tq = q_ref.shape[0]
    tk = k_ref.shape[0]

    @pl.when(kj == 0)
    def _():
        for h in range(N_ATT_HEADS):
            q1, q2 = _split_maps(q_ref[:, h * HEAD_W:(h + 1) * HEAD_W].astype(F32))
            qs_ref[2 * h] = q1
            qs_ref[2 * h + 1] = q2
        m_ref[...] = jnp.full(m_ref.shape, NEG_INF, F32)
        l_ref[...] = jnp.zeros(l_ref.shape, F32)
        acc_ref[...] = jnp.zeros(acc_ref.shape, F32)

    def step(mode):
        shift = CHUNK.bit_length() - 1
        key_chunk = jnp.right_shift(lax.broadcasted_iota(jnp.int32, (ATT_KB, ATT_QB), 0), shift)
        qry_chunk = jnp.right_shift(lax.broadcasted_iota(jnp.int32, (ATT_KB, ATT_QB), 1), shift)
        units = [(h, kb, c, qb) for h in range(N_ATT_HEADS) for kb in range(tk // ATT_KB)
                 for c in range(2) for qb in range(tq // ATT_QB)]
        for h, kb, c, qb in units:
                    key0, qry0 = kb * ATT_KB, qb * ATT_QB
                    if mode == "diag" and key0 >= qry0 + ATT_QB:
                        continue
                    kl = slice(key0, key0 + ATT_KB)
                    ql = slice(qry0, qry0 + ATT_QB)
                    idx = 2 * h + c
                    kh = k_ref[kl, h * HEAD_W:(h + 1) * HEAD_W]
                    vth = vt_ref[h * HEAD_W:(h + 1) * HEAD_W, kl]
                    sc = _dot_nt(kh, qs_ref[idx, ql, :])
                    if mode != "far":
                        sc = sc + bias_ref[0, h, kl, ql]
                    if mode == "diag" and key0 + ATT_KB > qry0:
                        visible = key_chunk + (key0 >> shift) <= qry_chunk + (qry0 >> shift)
                        sc = jnp.where(visible, sc, NEG_INF)
                    m_old = m_ref[idx, :, ql]
                    col_max = jnp.max(sc, axis=0, keepdims=True)
                    if mode == "far":
                        m_new = jnp.maximum(m_old, col_max + far_ref[h])
                        offset = m_new - far_ref[h]
                    else:
                        m_new = jnp.maximum(m_old, col_max)
                        offset = m_new
                    alpha = jnp.exp(m_old - m_new)
                    p = jnp.exp(sc - offset)
                    l_ref[idx, :, ql] = alpha * l_ref[idx, :, ql] + jnp.sum(p, axis=0, keepdims=True)
                    acc_ref[idx, :, ql] = alpha * acc_ref[idx, :, ql] + _dot(vth, p.astype(BF16))
                    m_ref[idx, :, ql] = m_new

    @pl.when(kj == qi)
    def _():
        step("diag")

    @pl.when(kj == qi - 1)
    def _():
        step("near")

    @pl.when(kj < qi - 1)
    def _():
        step("far")

    @pl.when(kj == qi)
    def _():
        lam = lam_ref[:, :1]
        subt = subt_ref[...]
        for h in range(N_ATT_HEADS):
            ot = acc_ref[2 * h] / l_ref[2 * h] - lam * (acc_ref[2 * h + 1] / l_ref[2 * h + 1])
            ms = jnp.mean(ot * ot, axis=0, keepdims=True)
            ont = ((ot * lax.rsqrt(ms + EPS)) * subt) * (1.0 - LAM_INIT)
            o_ref[:, h * HEAD_W:(h + 1) * HEAD_W] = ont.T.astype(BF16)


def _attn_prompt(q, k, vt, rel_bias, lam_vec, sub):
    t = q.shape[0]
    tq = ATT_TQ
    assert t % tq == 0 and tq % CHUNK == 0 and tq >= MAX_DISTANCE
    nq = t // tq
    qi = np.concatenate([np.full(i + 1, i, np.int32) for i in range(nq)])
    kj = np.concatenate([np.arange(i + 1, dtype=np.int32) for i in range(nq)])
    bias = jnp.stack([_toeplitz_bias(rel_bias, tq, d) for d in range(2)])
    bias = jnp.transpose(bias, (0, 1, 3, 2))
    far = _lookup(rel_bias.astype(F32).T, _t5_bucket(jnp.full((1,), -(tq + 1), jnp.int32)))[0]
    sub = sub.reshape(HEAD_W, 1)
    v = vt

    grid_spec = pltpu.PrefetchScalarGridSpec(
        num_scalar_prefetch=2,
        grid=(qi.shape[0],),
        in_specs=[pl.BlockSpec((tq, ATT_WIDTH), lambda s, qi, kj: (qi[s], 0)),
                  pl.BlockSpec((tq, ATT_WIDTH), lambda s, qi, kj: (kj[s], 0)),
                  pl.BlockSpec((ATT_WIDTH, tq), lambda s, qi, kj: (0, kj[s])),
                  pl.BlockSpec((1, N_ATT_HEADS, tq, tq),
                               lambda s, qi, kj: (jnp.minimum(qi[s] - kj[s], 1), 0, 0, 0)),
                  pl.BlockSpec(memory_space=pltpu.SMEM),
                  pl.BlockSpec((1, HEAD_W), lambda s, qi, kj: (0, 0)),
                  pl.BlockSpec((HEAD_W, 1), lambda s, qi, kj: (0, 0))],
        out_specs=pl.BlockSpec((tq, ATT_WIDTH), lambda s, qi, kj: (qi[s], 0)),
        scratch_shapes=[pltpu.VMEM((2 * N_ATT_HEADS, tq, HEAD_W), BF16),
                        pltpu.VMEM((2 * N_ATT_HEADS, 1, tq), F32),
                        pltpu.VMEM((2 * N_ATT_HEADS, 1, tq), F32),
                        pltpu.VMEM((2 * N_ATT_HEADS, HEAD_W, tq), F32)])
    return pl.pallas_call(
        _attn_prompt_kernel,
        out_shape=jax.ShapeDtypeStruct((t, ATT_WIDTH), BF16),
        grid_spec=grid_spec,
        compiler_params=pltpu.CompilerParams(dimension_semantics=("arbitrary",), vmem_limit_bytes=40 * MIB),
        name="attn_prompt",
    )(jnp.asarray(qi), jnp.asarray(kj), q, k, v, bias, far, lam_vec, sub)


def _attn_sample_kernel(q_ref, kn_ref, vn_ref, ck_hbm, cv_hbm, bc_ref, bn_ref, mc_ref, mn_ref, lam_ref,
                        sub_ref, o_ref, kbuf_ref, vbuf_ref, sem, *, n_batch):
    b = pl.program_id(0)
    slot = b % 2

    def cache_copies(bb, into):
        copies = []
        for h in range(N_ATT_HEADS):
            copies.append(pltpu.make_async_copy(ck_hbm.at[bb, :, h, :], kbuf_ref.at[into, h], sem.at[0, into]))
            copies.append(pltpu.make_async_copy(cv_hbm.at[bb, :, h, :], vbuf_ref.at[into, h], sem.at[1, into]))
        return copies

    @pl.when(b == 0)
    def _():
        for cp in cache_copies(0, 0):
            cp.start()

    @pl.when(b + 1 < n_batch)
    def _():
        for cp in cache_copies(b + 1, 1 - slot):
            cp.start()

    for cp in cache_copies(b, slot):
        cp.wait()

    lam = lam_ref[...]
    sub = sub_ref[...]
    vis_c = mc_ref[...] > 0.5
    vis_n = mn_ref[...] > 0.5
    for h in range(N_ATT_HEADS):
        hs = slice(h * HEAD_W, (h + 1) * HEAD_W)
        qmaps = _split_maps(q_ref[:, hs].astype(F32))
        kc = kbuf_ref[slot, h].astype(BF16)
        vc = vbuf_ref[slot, h].astype(BF16)
        kn = kn_ref[:, hs]
        vn = vn_ref[:, hs]
        probs = []
        for c in range(2):
            sc = jnp.where(vis_c, _dot_nt(qmaps[c], kc) + bc_ref[h], NEG_INF)
            sn = jnp.where(vis_n, _dot_nt(qmaps[c], kn) + bn_ref[h], NEG_INF)
            m = jnp.maximum(jnp.max(sc, axis=-1, keepdims=True), jnp.max(sn, axis=-1, keepdims=True))
            pc = jnp.exp(sc - m)
            pn = jnp.exp(sn - m)
            den = jnp.sum(pc, axis=-1, keepdims=True) + jnp.sum(pn, axis=-1, keepdims=True)
            probs.append((pc / den, pn / den))
        wc = probs[0][0] - lam[:, :1] * probs[1][0]
        wn = probs[0][1] - lam[:, :1] * probs[1][1]
        o = _dot(wc.astype(BF16), vc) + _dot(wn.astype(BF16), vn)
        o_ref[:, hs] = _subln(o, None, sub).astype(BF16)


def _attn_sample(q, kn, vn, cache_k, cache_v, rel_bias, lam_vec, sub):
    bsz, past = cache_k.shape[:2]
    seq = q.shape[0] // bsz
    q_pos = past + jnp.arange(seq, dtype=jnp.int32)
    k_pos = jnp.arange(past + seq, dtype=jnp.int32)
    bias = _rel_bias_table(rel_bias, q_pos, k_pos)
    visible = ((k_pos[None, :] // CHUNK) <= (q_pos[:, None] // CHUNK)).astype(F32)
    const = lambda *shape: pl.BlockSpec(shape, lambda b: (0,) * len(shape))
    row = pl.BlockSpec((seq, ATT_WIDTH), lambda b: (b, 0))
    cache = pl.BlockSpec(memory_space=pl.ANY)
    head_major = pltpu.VMEM((2, N_ATT_HEADS, past, HEAD_W), F32)
    return pl.pallas_call(
        functools.partial(_attn_sample_kernel, n_batch=bsz),
        out_shape=jax.ShapeDtypeStruct(q.shape, BF16),
        grid=(bsz,),
        in_specs=[row, row, row, cache, cache, const(N_ATT_HEADS, seq, past), const(N_ATT_HEADS, seq, seq),
                  const(seq, past), const(seq, seq), const(1, HEAD_W), const(1, HEAD_W)],
        out_specs=row,
        scratch_shapes=[head_major, head_major, pltpu.SemaphoreType.DMA((2, 2))],
        compiler_params=pltpu.CompilerParams(dimension_semantics=("arbitrary",), vmem_limit_bytes=40 * MIB),
        name="attn_sample",
    )(q, kn, vn, cache_k, cache_v, bias[:, :, :past], bias[:, :, past:], visible[:, :past], visible[:, past:],
      lam_vec, sub)


def _ssd_constants(l):
    hl = HEADS_PER_GROUP * l
    lp = SSD_LP
    e_head = np.zeros((LANES, SSM_D_INNER), np.float32)
    for h in range(N_SSM_HEADS):
        e_head[h, h * SSM_HEAD_DIM:(h + 1) * SSM_HEAD_DIM] = 1.0
    e_grp = np.zeros((N_SSM_GROUPS, LANES, hl), np.float32)
    for g in range(N_SSM_GROUPS):
        for r in range(HEADS_PER_GROUP):
            e_grp[g, g * HEADS_PER_GROUP + r, r * l:(r + 1) * l] = 1.0
    tile8 = np.zeros((lp, hl), np.float32)
    for r in range(HEADS_PER_GROUP):
        tile8[np.arange(l), r * l + np.arange(l)] = 1.0
    causal = np.zeros((l, hl), np.float32)
    for r in range(HEADS_PER_GROUP):
        causal[:, r * l:(r + 1) * l] = np.tril(np.ones((l, l), np.float32))
    tri = np.tril(np.ones((l, l), np.float32))
    ones = np.ones((l, l), np.float32)
    bmask = np.zeros((hl, GROUP_W), np.float32)
    for r in range(HEADS_PER_GROUP):
        bmask[r * l:(r + 1) * l, r * SSM_HEAD_DIM:(r + 1) * SSM_HEAD_DIM] = 1.0
    as_bf = lambda a: jnp.asarray(a, BF16)
    return (as_bf(e_head), as_bf(e_grp), as_bf(tile8), jnp.asarray(causal), as_bf(tri), as_bf(ones),
            as_bf(bmask))


def _ssd_kernel(xbc_ref, z_ref, dt_ref, hist_ref, st0_ref, cw_ref, cb_ref, dtb_ref, alog_ref, dskip_ref,
                norm_ref, eh_ref, eg_ref, t8_ref, caus_ref, tri_ref, ones_ref, bmask_ref,
                yz_ref, st_ref, buf_ref, state_ref):
    c = pl.program_id(1)
    l = xbc_ref.shape[0]
    lp = SSD_LP
    hist_rows = hist_ref.shape[0]

    @pl.when(c == 0)
    def _():
        buf_ref[0:hist_rows, :] = hist_ref[...]
        state_ref[...] = st0_ref[...]

    u = xbc_ref[...]
    buf_ref[hist_rows:hist_rows + l, :] = u
    conv = cb_ref[...] + cw_ref[CONV_WIDTH - 1:CONV_WIDTH, :] * u
    for w in range(CONV_WIDTH - 1):
        shift = CONV_WIDTH - 1 - w
        conv = conv + cw_ref[w:w + 1, :] * buf_ref[hist_rows - shift:hist_rows - shift + l, :]
    buf_ref[0:hist_rows, :] = buf_ref[l:l + hist_rows, :]
    xc = conv * _sigmoid(conv)
    xs = xc[:, :SSM_D_INNER]
    bm = xc[:, SSM_D_INNER:SSM_D_INNER + N_SSM_GROUPS * D_STATE]
    cm = xc[:, SSM_D_INNER + N_SSM_GROUPS * D_STATE:]

    dt_in = dt_ref[...] + dtb_ref[...]
    dt = jnp.maximum(dt_in, 0.0) + jnp.log(1.0 + jnp.exp(-jnp.abs(dt_in)))
    a = -jnp.exp(alog_ref[...])
    acum = _xdot_l(tri_ref[...], dt * a)
    acum_t = acum.T if l == LANES else None
    eh = eh_ref[...]
    dt_e = _xdot_r(dt, eh)
    ac_e = _xdot_r(acum, eh)
    a_last = ac_e[l - 1:l, :]
    ecum = jnp.exp(ac_e)
    xdt = xs * dt_e
    xdtw_b = (xdt * jnp.exp(a_last - ac_e)).astype(BF16)
    xdt_b = xdt.astype(BF16)
    drow = jnp.exp(a_last)
    z = z_ref[...]
    caus = caus_ref[...] > 0.5
    t8 = t8_ref[...]
    t8_mask = t8[0:l, :] > 0
    bmask = bmask_ref[...] > 0
    row_pad = lp - l

    for g in range(N_SSM_GROUPS):
        gs = slice(g * GROUP_W, (g + 1) * GROUP_W)
        ns = slice(g * D_STATE, (g + 1) * D_STATE)
        bm_g = bm[:, ns]
        cm_b = cm[:, ns].astype(BF16)
        xw_g = xdtw_b[:, gs]
        if row_pad:
            bm_g = jnp.concatenate([bm_g, jnp.zeros((row_pad, D_STATE), F32)], axis=0)
            xw_g = jnp.concatenate([xw_g, jnp.zeros((row_pad, GROUP_W), BF16)], axis=0)
        bmt_b = bm_g.T.astype(BF16)
        cb8 = _dot(cm_b, _dot(bmt_b, t8).astype(BF16))
        a1 = _xdot_r(acum, eg_ref[g])
        if l == LANES:
            heads = range(g * HEADS_PER_GROUP, (g + 1) * HEADS_PER_GROUP)
            a2 = jnp.concatenate([acum_t[hd:hd + 1, :] for hd in heads], axis=1)
        else:
            a2 = _xdot_l(ones_ref[...], jnp.where(t8_mask, a1, 0.0))
        decay = jnp.where(caus, jnp.exp(jnp.where(caus, a1 - a2, 0.0)), 0.0)
        m_b = (cb8 * decay).astype(BF16)
        xg = xdt_b[:, gs]
        bd = jnp.concatenate([xg] * HEADS_PER_GROUP, axis=0)
        bd = jnp.where(bmask, bd, jnp.zeros_like(bd))
        y = _dot(m_b, bd)
        st_g = state_ref[g]
        y = y + _dot(cm_b, st_g.astype(BF16)) * ecum[:, gs]
        y = y + dskip_ref[:, gs] * xs[:, gs]
        state_ref[g] = st_g * drow[:, gs] + _dot(bmt_b, xw_g)
        zg = z[:, gs]
        yz = y * (zg * _sigmoid(zg))
        yz_ref[:, gs] = (_rms(yz) * norm_ref[:, gs]).astype(BF16)

    @pl.when(c == pl.num_programs(1) - 1)
    def _():
        st_ref[...] = state_ref[...]


def _ssd(xbc, z, dt, hist8, st0, conv_w, conv_b, dt_bias, a_log, dskip_e, ssm_norm, bsz, l):
    rows = xbc.shape[0]
    seq = rows // bsz
    assert seq % l == 0 and l % SUBLANES == 0 and l <= SSD_LP and l >= SUBLANES
    nc = seq // l
    consts = _ssd_constants(l)
    rowblk = lambda width: pl.BlockSpec((l, width), lambda b, c: (b * nc + c, 0))
    const = lambda arr: pl.BlockSpec(arr.shape, lambda b, c: (0,) * arr.ndim)
    params = (conv_w, conv_b, dt_bias, a_log, dskip_e, ssm_norm)
    return pl.pallas_call(
        _ssd_kernel,
        out_shape=(jax.ShapeDtypeStruct((rows, SSM_D_INNER), BF16),
                   jax.ShapeDtypeStruct(st0.shape, F32)),
        grid=(bsz, nc),
        in_specs=[rowblk(CONV_DIM), rowblk(SSM_D_INNER), rowblk(LANES),
                  pl.BlockSpec((None,) + hist8.shape[1:], lambda b, c: (b, 0, 0)),
                  pl.BlockSpec((None,) + st0.shape[1:], lambda b, c: (b, 0, 0, 0))]
                 + [const(p) for p in params] + [const(k) for k in consts],
        out_specs=(rowblk(SSM_D_INNER), pl.BlockSpec((None,) + st0.shape[1:], lambda b, c: (b, 0, 0, 0))),
        scratch_shapes=[pltpu.VMEM((SUBLANES + l, CONV_DIM), F32),
                        pltpu.VMEM(st0.shape[1:], F32)],
        compiler_params=pltpu.CompilerParams(dimension_semantics=("arbitrary", "arbitrary"),
                                             vmem_limit_bytes=48 * MIB),
        name="ssd",
    )(xbc, z, dt, hist8, st0, *params, *consts)


def _mix_kernel(on_ref, yz_ref, ga_ref, gs_ref, wa_ref, ws_ref, o_ref):
    att = _dot(on_ref[...], wa_ref[...])
    ssm = _dot(yz_ref[...], ws_ref[...])
    o_ref[...] = (_sigmoid(ga_ref[...]) * att + _sigmoid(gs_ref[...]) * ssm).astype(BF16)


def _mix(on, yz, ga, gs, wa, ws):
    t = on.shape[0]
    tm = TOK_TM
    rowblk = lambda width: pl.BlockSpec((tm, width), lambda i: (i, 0))
    const = lambda arr: pl.BlockSpec(arr.shape, lambda i: (0,) * arr.ndim)
    return pl.pallas_call(
        _mix_kernel,
        out_shape=jax.ShapeDtypeStruct((t, D_MODEL), BF16),
        grid=(t // tm,),
        in_specs=[rowblk(ATT_WIDTH), rowblk(SSM_D_INNER), rowblk(D_MODEL), rowblk(D_MODEL), const(wa), const(ws)],
        out_specs=rowblk(D_MODEL),
        compiler_params=pltpu.CompilerParams(dimension_semantics=("arbitrary",), vmem_limit_bytes=48 * MIB),
        name="mix",
    )(on, yz, ga, gs, wa, ws)


def _pack_bf16_pairs(x):
    w = x.shape[1] // 2
    lo = lax.bitcast_convert_type(x[:, :w].astype(BF16).astype(F32), U32)
    hi = lax.bitcast_convert_type(x[:, w:].astype(BF16).astype(F32), U32)
    return hi | (lo >> 16)


def _unpack_bf16_pairs(words):
    lo = lax.bitcast_convert_type(words << 16, F32)
    hi = lax.bitcast_convert_type(words & jnp.uint32(0xFFFF0000), F32)
    return jnp.concatenate([lo, hi], axis=1).astype(BF16)


def _resid_kernel(x_ref, mixed_ref, wo_ref, g_ref, wr_ref, br_ref, lt_ref, cin_ref,
                  h_ref, xp_ref, route_ref, cnt_ref, carry_ref):
    i = pl.program_id(0)

    @pl.when(i == 0)
    def _():
        carry_ref[...] = cin_ref[...]

    h = x_ref[...] + _dot(mixed_ref[...], wo_ref[...])
    h_ref[...] = h
    xn = _rms(h) * g_ref[...]
    xp_ref[...] = _pack_bf16_pairs(xn)
    logits = _dot(xn.astype(BF16), wr_ref[...]) + br_ref[...]

    lane = lax.broadcasted_iota(jnp.int32, logits.shape, 1)
    lane_f = lane.astype(F32)
    rest = logits
    vals, idxs, sels = [], [], []
    for _ in range(TOP_K):
        m = jnp.max(rest, axis=-1, keepdims=True)
        idx = jnp.min(jnp.where(rest == m, lane_f, float(LANES)), axis=-1, keepdims=True)
        sel = lane_f == idx
        rest = jnp.where(sel, -jnp.inf, rest)
        vals.append(m)
        idxs.append(idx)
        sels.append(sel)
    exps = [jnp.exp(v - vals[0]) for v in vals]
    den = exps[0] + exps[1] + exps[2] + exps[3]
    onehot = jnp.zeros(logits.shape, F32)
    for sel in sels:
        onehot = onehot + jnp.where(sel, 1.0, 0.0)
    before = _dot(lt_ref[...], onehot.astype(BF16)) + carry_ref[...]
    route = jnp.zeros(logits.shape, F32)
    for k in range(TOP_K):
        pos = jnp.sum(jnp.where(sels[k], before, 0.0), axis=-1, keepdims=True)
        route = route + jnp.where(lane == k, idxs[k], 0.0)
        route = route + jnp.where(lane == TOP_K + k, exps[k] / den, 0.0)
        route = route + jnp.where(lane == 2 * TOP_K + k, pos, 0.0)
    route_ref[...] = route
    carry_ref[...] = carry_ref[...] + jnp.sum(onehot, axis=0, keepdims=True)
    cnt_ref[...] = carry_ref[...]


def _resid(x, mixed, wo, g_ffn, wr, br, counts_in):
    t = x.shape[0]
    tm = TOK_TM
    lt = jnp.asarray(np.tril(np.ones((tm, tm), np.float32), -1), BF16)
    rowblk = lambda width: pl.BlockSpec((tm, width), lambda i: (i, 0))
    const = lambda arr: pl.BlockSpec(arr.shape, lambda i: (0,) * arr.ndim)
    return pl.pallas_call(
        _resid_kernel,
        out_shape=(jax.ShapeDtypeStruct((t, D_MODEL), F32),
                   jax.ShapeDtypeStruct((t, D_MODEL // 2), U32),
                   jax.ShapeDtypeStruct((t, LANES), F32),
                   jax.ShapeDtypeStruct((1, LANES), F32)),
        grid=(t // tm,),
        in_specs=[rowblk(D_MODEL), rowblk(D_MODEL), const(wo), const(g_ffn), const(wr), const(br), const(lt),
                  const(counts_in)],
        out_specs=(rowblk(D_MODEL), rowblk(D_MODEL // 2), rowblk(LANES), pl.BlockSpec((1, LANES), lambda i: (0, 0))),
        scratch_shapes=[pltpu.VMEM((1, LANES), F32)],
        compiler_params=pltpu.CompilerParams(dimension_semantics=("arbitrary",), vmem_limit_bytes=48 * MIB),
        name="resid_route",
    )(x, mixed, wo, g_ffn, wr, br, lt, counts_in)


def _row_copy(src, src_row, dst, dst_row, sem):
    return pltpu.make_async_copy(src.at[pl.ds(src_row, 1)], dst.at[pl.ds(dst_row, 1)], sem)


def _dispatch_kernel(dest_ref, xp_ref, *rest, first):
    if first:
        xs_ref, sem, zero_ref = rest

        @pl.when(pl.program_id(0) == 0)
        def _():
            zero_ref[...] = jnp.zeros(zero_ref.shape, zero_ref.dtype)
            blk = zero_ref.shape[0]

            def fill(b):
                return pltpu.make_async_copy(zero_ref, xs_ref.at[pl.ds(pl.multiple_of(b * blk, blk), blk)], sem)

            def start_fill(b, carry):
                fill(b).start()
                return carry

            def wait_fill(b, carry):
                fill(b).wait()
                return carry

            lax.fori_loop(0, xs_ref.shape[0] // blk, start_fill, 0)
            lax.fori_loop(0, xs_ref.shape[0] // blk, wait_fill, 0)
    else:
        _, xs_ref, sem = rest

    def issue(t, carry):
        for k in range(TOP_K):
            _row_copy(xp_ref, t, xs_ref, dest_ref[t * TOP_K + k], sem).start()
        return carry

    lax.fori_loop(0, TOK_TM, issue, 0, unroll=DMA_UNROLL)

    def drain(t, carry):
        for k in range(TOP_K):
            _row_copy(xp_ref, 0, xs_ref, 0, sem).wait()
        return carry

    lax.fori_loop(0, TOK_TM, drain, 0, unroll=DMA_UNROLL)


def _dispatch(dest_flat, xp, xs, n_rows):
    t, width = xp.shape
    first = xs is None
    assert n_rows % TOK_TM == 0
    in_specs = [pl.BlockSpec((TOK_TM * TOP_K,), lambda i: (i,), memory_space=pltpu.SMEM),
                pl.BlockSpec((TOK_TM, width), lambda i: (i, 0))]
    scratch = [pltpu.SemaphoreType.DMA(())]
    operands = [dest_flat, xp]
    if first:
        scratch.append(pltpu.VMEM((TOK_TM, width), xp.dtype))
    else:
        in_specs.append(pl.BlockSpec(memory_space=pl.ANY))
        operands.append(xs)
    return pl.pallas_call(
        functools.partial(_dispatch_kernel, first=first),
        out_shape=jax.ShapeDtypeStruct((n_rows, width), xp.dtype),
        grid=(t // TOK_TM,),
        in_specs=in_specs,
        out_specs=pl.BlockSpec(memory_space=pl.ANY),
        scratch_shapes=scratch,
        input_output_aliases={} if first else {2: 0},
        compiler_params=pltpu.CompilerParams(dimension_semantics=("arbitrary",)),
        name="dispatch_first" if first else "dispatch",
    )(*operands)


def _experts_kernel(sbe_ref, sbs_ref, sbr_ref, nsb_ref, xs_ref, wg_ref, wu_ref, wd_ref, bg_ref, bu_ref,
                    bd_ref, y_ref, xw_ref, xb_ref, acc_ref, wgb_ref, wub_ref, wdb_ref, sem_in, sem_out):
    s = pl.program_id(0)
    f = pl.program_id(1)
    nf = pl.num_programs(1)
    n_sub = SUPER_ROWS // ROW_BLK

    @pl.when(s < nsb_ref[0])
    def _():
        start = pl.multiple_of(sbs_ref[s], ROW_BLK)
        rows = sbr_ref[s]

        def x_copy(sb, slot):
            src = xs_ref.at[pl.ds(pl.multiple_of(sbs_ref[sb], ROW_BLK), SUPER_ROWS)]
            return pltpu.make_async_copy(src, xw_ref.at[slot], sem_in.at[slot])

        @pl.when(f == 0)
        def _():
            slot = s % 2

            @pl.when(s == 0)
            def _():
                x_copy(0, 0).start()

            x_copy(s, slot).wait()

            @pl.when(s + 1 < nsb_ref[0])
            def _():
                x_copy(s + 1, 1 - slot).start()

            xb_ref[...] = _unpack_bf16_pairs(xw_ref[slot])
            acc_ref[...] = jnp.zeros(acc_ref.shape, F32)

        wgb_ref[...] = wg_ref[0].astype(BF16)
        wub_ref[...] = wu_ref[0].astype(BF16)
        wdb_ref[...] = wd_ref[0].astype(BF16)
        half_rows = SUPER_ROWS // 2
        for half in range(2):
            rs = slice(half * half_rows, (half + 1) * half_rows)
            x = xb_ref[rs, :]
            gate = _dot(x, wgb_ref[...]) + bg_ref[0]
            up = _dot(x, wub_ref[...]) + bu_ref[0]
            gate = jnp.minimum(gate, SWIGLU_LIMIT)
            up = jnp.clip(up, -SWIGLU_LIMIT, SWIGLU_LIMIT)
            act = (up + 1.0) * gate * _sigmoid(SWIGLU_ALPHA * gate)
            acc_ref[rs, :] += _dot(act.astype(BF16), wdb_ref[...])

        @pl.when(f == nf - 1)
        def _():
            def out_copy(sub):
                rs = pl.ds(sub * ROW_BLK, ROW_BLK)
                return pltpu.make_async_copy(acc_ref.at[rs], y_ref.at[pl.ds(start + sub * ROW_BLK, ROW_BLK)],
                                             sem_out)

            for sub in range(n_sub):
                @pl.when(sub * ROW_BLK < rows)
                def _():
                    rs = slice(sub * ROW_BLK, (sub + 1) * ROW_BLK)
                    acc_ref[rs, :] += bd_ref[0]
                    out_copy(sub).start()

            for sub in range(n_sub):
                @pl.when(sub * ROW_BLK < rows)
                def _():
                    out_copy(sub).wait()

    @pl.when((s == pl.num_programs(0) - 1) & (f == nf - 1))
    def _():
        zero_ref = acc_ref.at[pl.ds(0, ROW_BLK)]
        zero_ref[...] = jnp.zeros(zero_ref.shape, F32)
        n_blocks = y_ref.shape[0] // ROW_BLK

        def tail_copy(b):
            return pltpu.make_async_copy(zero_ref, y_ref.at[pl.ds(pl.multiple_of(b * ROW_BLK, ROW_BLK), ROW_BLK)],
                                         sem_out)

        def issue(b, carry):
            tail_copy(b).start()
            return carry

        def drain(b, carry):
            tail_copy(b).wait()
            return carry

        lax.fori_loop(nsb_ref[1], n_blocks, issue, 0)
        lax.fori_loop(nsb_ref[1], n_blocks, drain, 0)


def _experts(sb_expert, sb_start, sb_rows, n_sb, xs, w_gate_up, b_gate_up, w_down, b_down, n_rows):
    n_super = sb_expert.shape[0]
    nf = D_FF // FF_TILE

    def widx(s, f, sbe, sbs, sbr, nsb):
        live = s < nsb[0]
        return sbe[s], jnp.where(live, f, nf - 1)

    def gate_map(s, f, *pref):
        e, ff = widx(s, f, *pref)
        return (e, 0, ff)

    def up_map(s, f, *pref):
        e, ff = widx(s, f, *pref)
        return (e, 0, nf + ff)

    def down_map(s, f, *pref):
        e, ff = widx(s, f, *pref)
        return (e, ff, 0)

    def bias_map(s, f, *pref):
        return (widx(s, f, *pref)[0], 0, 0)

    b_gu3 = b_gate_up.reshape(N_EXPERTS, 1, 2 * D_FF)
    b_dn3 = b_down.reshape(N_EXPERTS, 1, D_MODEL)
    grid_spec = pltpu.PrefetchScalarGridSpec(
        num_scalar_prefetch=4,
        grid=(n_super, nf),
        in_specs=[pl.BlockSpec(memory_space=pl.ANY),
                  pl.BlockSpec((1, D_MODEL, FF_TILE), gate_map),
                  pl.BlockSpec((1, D_MODEL, FF_TILE), up_map),
                  pl.BlockSpec((1, FF_TILE, D_MODEL), down_map),
                  pl.BlockSpec((1, 1, FF_TILE), gate_map),
                  pl.BlockSpec((1, 1, FF_TILE), up_map),
                  pl.BlockSpec((1, 1, D_MODEL), bias_map)],
        out_specs=pl.BlockSpec(memory_space=pl.ANY),
        scratch_shapes=[pltpu.VMEM((2, SUPER_ROWS, D_MODEL // 2), U32),
                        pltpu.VMEM((SUPER_ROWS, D_MODEL), BF16),
                        pltpu.VMEM((SUPER_ROWS, D_MODEL), F32),
                        pltpu.VMEM((D_MODEL, FF_TILE), BF16),
                        pltpu.VMEM((D_MODEL, FF_TILE), BF16),
                        pltpu.VMEM((FF_TILE, D_MODEL), BF16),
                        pltpu.SemaphoreType.DMA((2,)),
                        pltpu.SemaphoreType.DMA(())])
    return pl.pallas_call(
        _experts_kernel,
        out_shape=jax.ShapeDtypeStruct((n_rows, D_MODEL), F32),
        grid_spec=grid_spec,
        compiler_params=pltpu.CompilerParams(dimension_semantics=("arbitrary", "arbitrary"),
                                             vmem_limit_bytes=56 * MIB),
        name="experts",
    )(sb_expert, sb_start, sb_rows, n_sb, xs, w_gate_up, w_gate_up, w_down, b_gu3, b_gu3, b_dn3)


def _combine_kernel(dest_ref, dest_next_ref, h_ref, route_ref, p_ref, y_ref, gple_ref, wpg_ref, wpp_ref,
                    gfin_ref, o_ref, gbuf_ref, sem, *, n_tiles):
    i = pl.program_id(0)
    slot = i % 2

    def gather_tile(d_ref, into):
        def issue(t, carry):
            for k in range(TOP_K):
                _row_copy(y_ref, d_ref[t * TOP_K + k], gbuf_ref.at[into].at[k], t, sem.at[into]).start()
            return carry

        lax.fori_loop(0, TOK_TM, issue, 0, unroll=DMA_UNROLL)

    @pl.when(i == 0)
    def _():
        gather_tile(dest_ref, 0)

    @pl.when(i + 1 < n_tiles)
    def _():
        gather_tile(dest_next_ref, 1 - slot)

    def drain(t, carry):
        for k in range(TOP_K):
            _row_copy(y_ref, 0, gbuf_ref.at[slot].at[k], 0, sem.at[slot]).wait()
        return carry

    lax.fori_loop(0, TOK_TM, drain, 0, unroll=DMA_UNROLL)

    route = route_ref[...]
    h = h_ref[...]
    for k in range(TOP_K):
        h = h + route[:, TOP_K + k:TOP_K + k + 1] * gbuf_ref[slot, k]
    xn = (_rms(h) * gple_ref[...]).astype(BF16)
    gate = _sigmoid(_dot(xn, wpg_ref[...]))
    h = h + gate * _dot(p_ref[...].astype(BF16), wpp_ref[...])
    o_ref[...] = _rms(h) * gfin_ref[...]


def _combine(dest_flat, h, route, p, y_sorted, g_ple, wpg, wpp, g_final):
    t = h.shape[0]
    tm = TOK_TM
    rowblk = lambda width: pl.BlockSpec((tm, width), lambda i: (i, 0))
    const = lambda arr: pl.BlockSpec(arr.shape, lambda i: (0,) * arr.ndim)
    return pl.pallas_call(
        functools.partial(_combine_kernel, n_tiles=t // tm),
        out_shape=jax.ShapeDtypeStruct((t, D_MODEL), F32),
        grid=(t // tm,),
        in_specs=[pl.BlockSpec((tm * TOP_K,), lambda i: (i,), memory_space=pltpu.SMEM),
                  pl.BlockSpec((tm * TOP_K,), lambda i: (jnp.minimum(i + 1, t // tm - 1),),
                               memory_space=pltpu.SMEM),
                  rowblk(D_MODEL), rowblk(LANES), rowblk(PLE_DIM),
                  pl.BlockSpec(memory_space=pl.ANY),
                  const(g_ple), const(wpg), const(wpp), const(g_final)],
        out_specs=rowblk(D_MODEL),
        scratch_shapes=[pltpu.VMEM((2, TOP_K, tm, D_MODEL), F32), pltpu.SemaphoreType.DMA((2,))],
        compiler_params=pltpu.CompilerParams(dimension_semantics=("arbitrary",), vmem_limit_bytes=52 * MIB),
        name="combine",
    )(dest_flat, dest_flat, h, route, p, y_sorted, g_ple, wpg, wpp, g_final)


def _super_blocks(counts, n_super):
    padded = (counts + ROW_BLK - 1) // ROW_BLK * ROW_BLK
    pad_start = jnp.cumsum(padded) - padded
    per_expert = (counts + SUPER_ROWS - 1) // SUPER_ROWS
    sb_end = jnp.cumsum(per_expert)
    n_sb = sb_end[-1]
    s = jnp.arange(n_super, dtype=jnp.int32)
    s_live = jnp.minimum(s, jnp.maximum(n_sb - 1, 0))
    expert = jnp.sum(sb_end[None, :] <= s_live[:, None], axis=1).astype(jnp.int32)
    expert = jnp.minimum(expert, N_EXPERTS - 1)
    within = s_live - _lookup(sb_end - per_expert, expert)
    start = _lookup(pad_start, expert) + within * SUPER_ROWS
    rows = jnp.where(s < n_sb, jnp.clip(_lookup(counts, expert) - within * SUPER_ROWS, 0, SUPER_ROWS), 0)
    first_unused_block = jnp.sum(padded) // ROW_BLK
    return (pad_start, expert, start.astype(jnp.int32), rows.astype(jnp.int32),
            jnp.stack([n_sb, first_unused_block]).astype(jnp.int32))


def _mixer(x, cache_k, cache_v, st0, conv0, prep, ssd_l):
    bsz, seq, _ = x.shape
    x2 = x.reshape(bsz * seq, D_MODEL)
    q, k, v, kb, vb, vt, z, xbc, ga, gs, dt = _inproj(x2, prep["g_mix"], prep["w_main"], prep["w_dt"])
    if cache_k is None:
        assert bsz == 1
        on = _attn_prompt(q, kb, vt, prep["rel_bias"], prep["lam"], prep["subln"])
    else:
        past = cache_k.shape[1]
        on = _attn_sample(q, kb, vb, cache_k, cache_v, prep["rel_bias"], prep["lam"], prep["subln"])
    hist8 = jnp.pad(conv0, ((0, 0), (SUBLANES - (CONV_WIDTH - 1), 0), (0, 0)))
    st0_t = st0.reshape(bsz, N_SSM_GROUPS, HEADS_PER_GROUP, SSM_HEAD_DIM, D_STATE)
    st0_t = jnp.transpose(st0_t, (0, 1, 4, 2, 3)).reshape(bsz, N_SSM_GROUPS, D_STATE, GROUP_W)
    yz, st_t = _ssd(xbc, z, dt, hist8, st0_t, prep["conv_w"], prep["conv_b"], prep["dt_bias"], prep["a_log"],
                    prep["dskip_e"], prep["ssm_norm"], bsz, ssd_l)
    st_new = st_t.reshape(bsz, N_SSM_GROUPS, D_STATE, HEADS_PER_GROUP, SSM_HEAD_DIM)
    st_new = jnp.transpose(st_new, (0, 1, 3, 4, 2)).reshape(1, bsz, N_SSM_HEADS, SSM_HEAD_DIM, D_STATE)
    mixed = _mix(on, yz, ga, gs, prep["w_attn_out"], prep["w_ssm_out"])
    k_rows = k.reshape(1, bsz, seq, N_ATT_HEADS, HEAD_W)
    v_rows = v.reshape(1, bsz, seq, N_ATT_HEADS, HEAD_W)
    conv_new = xbc.reshape(bsz, seq, CONV_DIM)[:, seq - (CONV_WIDTH - 1):].reshape(1, bsz, CONV_WIDTH - 1, CONV_DIM)
    return x2, mixed, k_rows, v_rows, st_new, conv_new


def kernel(x_prompt, x_sample, cache_k, cache_v, state_ssm, state_conv, p_prompt, p_sample, rel_bias, w_in,
           lambda_q1, lambda_k1, lambda_q2, lambda_k2, attn_subln, w_attn_out, conv_w, conv_b, dt_bias, a_log,
           d_skip, ssm_norm, w_ssm_out, w_o, g_mix, g_ffn, w_router, b_router, w_gate_up, b_gate_up, w_down,
           b_down, g_ple, w_ple_gate, w_ple_proj, g_final):
    w = w_in[0]
    c_dt = 3 * ATT_WIDTH + SSM_D_INNER + CONV_DIM
    w_main = jnp.concatenate([w[:, :c_dt], w[:, c_dt + N_SSM_HEADS:]], axis=1).astype(BF16)
    w_dt = jnp.pad(w[:, c_dt:c_dt + N_SSM_HEADS], ((0, 0), (0, LANES - N_SSM_HEADS))).astype(BF16)
    lam = (jnp.exp(jnp.sum(lambda_q1[0] * lambda_k1[0]).astype(F32))
           - jnp.exp(jnp.sum(lambda_q2[0] * lambda_k2[0]).astype(F32)) + LAM_INIT)
    pad_heads = lambda v: jnp.pad(v.reshape(1, N_SSM_HEADS), ((0, 0), (0, LANES - N_SSM_HEADS)))
    prep = dict(
        g_mix=g_mix, w_main=w_main, w_dt=w_dt, rel_bias=rel_bias,
        lam=jnp.full((1, HEAD_W), lam, F32), subln=attn_subln,
        conv_w=conv_w[0], conv_b=conv_b, dt_bias=pad_heads(dt_bias[0]), a_log=pad_heads(a_log[0]),
        dskip_e=jnp.repeat(d_skip[0], SSM_HEAD_DIM).reshape(1, SSM_D_INNER), ssm_norm=ssm_norm,
        w_attn_out=w_attn_out[0].astype(BF16), w_ssm_out=w_ssm_out[0].astype(BF16))
    wo = w_o[0].astype(BF16)
    wr = jnp.pad(w_router[0], ((0, 0), (0, LANES - N_EXPERTS))).astype(BF16)
    br = jnp.pad(b_router, ((0, 0), (0, LANES - N_EXPERTS)), constant_values=NEG_INF)
    wpg = w_ple_gate[0].astype(BF16)
    wpp = w_ple_proj[0].astype(BF16)

    bp, sp, _ = x_prompt.shape
    bs, ss, _ = x_sample.shape
    zeros_state = jnp.zeros((bp, N_SSM_HEADS, SSM_HEAD_DIM, D_STATE), F32)
    zeros_conv = jnp.zeros((bp, CONV_WIDTH - 1, CONV_DIM), F32)
    xp2, mixed_p, k_p, v_p, ssm_p, conv_p = _mixer(x_prompt, None, None, zeros_state, zeros_conv, prep, SSD_L_PROMPT)
    xs2, mixed_s, k_s, v_s, ssm_s, conv_s = _mixer(x_sample, cache_k[0], cache_v[0], state_ssm[0], state_conv[0],
                                                   prep, ss)

    zero_counts = jnp.zeros((1, LANES), F32)
    h_p, xpk_p, route_p, cnt_p = _resid(xp2, mixed_p, wo, g_ffn, wr, br, zero_counts)
    h_s, xpk_s, route_s, cnt = _resid(xs2, mixed_s, wo, g_ffn, wr, br, cnt_p)
    n_tok = xp2.shape[0] + xs2.shape[0]
    n_slots = n_tok * TOP_K
    n_super = -(-n_slots // SUPER_ROWS) + N_EXPERTS
    n_rows = n_slots + N_EXPERTS * ROW_BLK
    counts = cnt[0, :N_EXPERTS].astype(jnp.int32)
    pad_start, sb_expert, sb_start, sb_rows, n_sb = _super_blocks(counts, n_super)

    def dest_of(route):
        expert = route[:, :TOP_K].astype(jnp.int32)
        pos = route[:, 2 * TOP_K:3 * TOP_K].astype(jnp.int32)
        return (_lookup(pad_start, expert) + pos).astype(jnp.int32).reshape(-1)

    dest_p = dest_of(route_p)
    dest_s = dest_of(route_s)
    xs_sorted = _dispatch(dest_p, xpk_p, None, n_rows + SUPER_ROWS)
    xs_sorted = _dispatch(dest_s, xpk_s, xs_sorted, n_rows + SUPER_ROWS)
    y_sorted = _experts(sb_expert, sb_start, sb_rows, n_sb, xs_sorted, w_gate_up[0], b_gate_up[0], w_down[0],
                        b_down[0], n_rows)

    y_p = _combine(dest_p, h_p, route_p, p_prompt[0].reshape(-1, PLE_DIM), y_sorted, g_ple, wpg, wpp,
                   g_final.reshape(1, D_MODEL))
    y_s = _combine(dest_s, h_s, route_s, p_sample[0].reshape(-1, PLE_DIM), y_sorted, g_ple, wpg, wpp,
                   g_final.reshape(1, D_MODEL))
    return (y_p.reshape(bp, sp, D_MODEL), y_s.reshape(bs, ss, D_MODEL), k_p, v_p, ssm_p, conv_p,
            k_s, v_s, ssm_s, conv_s)
```

```python
import functools
import math

import numpy as np
import jax
import jax.numpy as jnp
from jax import lax
from jax.experimental import pallas as pl
from jax.experimental.pallas import tpu as pltpu

F32 = jnp.float32
BF16 = jnp.bfloat16
U32 = jnp.uint32

D_MODEL = 2048
CHUNK = 64
N_ATT_HEADS = 8
ATT_HEAD_DIM = 64
HEAD_W = 2 * ATT_HEAD_DIM
ATT_WIDTH = N_ATT_HEADS * HEAD_W
ATT_SCALE = ATT_HEAD_DIM ** -0.5
N_BUCKETS = 32
MAX_DISTANCE = 128
NEG_INF = -1e30
SSM_D_INNER = 2048
SSM_HEAD_DIM = 64
N_SSM_HEADS = SSM_D_INNER // SSM_HEAD_DIM
N_SSM_GROUPS = 4
HEADS_PER_GROUP = N_SSM_HEADS // N_SSM_GROUPS
GROUP_W = HEADS_PER_GROUP * SSM_HEAD_DIM
D_STATE = 128
CONV_WIDTH = 4
CONV_DIM = SSM_D_INNER + 2 * N_SSM_GROUPS * D_STATE
N_EXPERTS = 32
TOP_K = 4
D_FF = 2048
SWIGLU_LIMIT = 7.0
SWIGLU_ALPHA = 1.702
PLE_DIM = 256
EPS = 1e-6
LAM_INIT = 0.8 - 0.6 * math.exp(-0.3 * 0)

LANES = 128
SUBLANES = 8
MIB = 1024 * 1024

INPROJ_TM = 1024
INPROJ_TN = 512
ATT_TQ = 512
ATT_QB = 128
ATT_KB = 256
SSD_L_PROMPT = 128
SSD_LP = 128
TOK_TM = 256
ROW_BLK = 256
SUPER_ROWS = 1280
FF_TILE = 512
DMA_UNROLL = 8


def _dot(a, b):
    return jnp.dot(a, b, preferred_element_type=F32)


def _dot_nt(a, b):
    return lax.dot_general(a, b, (((1,), (1,)), ((), ())), preferred_element_type=F32)


def _rms(x):
    return x * lax.rsqrt(jnp.mean(x * x, axis=-1, keepdims=True) + EPS)


def _sigmoid(x):
    return 1.0 / (1.0 + jnp.exp(-x))


def _split3(x):
    hi = x.astype(BF16)
    r = x - hi.astype(F32)
    mid = r.astype(BF16)
    lo = (r - mid.astype(F32)).astype(BF16)
    return hi, mid, lo


def _xdot_r(x, c):
    hi, mid, lo = _split3(x)
    return (_dot(hi, c) + _dot(mid, c)) + _dot(lo, c)


def _xdot_l(c, x):
    hi, mid, lo = _split3(x)
    return (_dot(c, hi) + _dot(c, mid)) + _dot(c, lo)


_SEG_WIDTHS = (("q", ATT_WIDTH), ("k", ATT_WIDTH), ("v", ATT_WIDTH), ("z", SSM_D_INNER),
               ("xbc", CONV_DIM), ("ga", D_MODEL), ("gs", D_MODEL))


def _segments():
    segs, first = {}, 0
    for name, width in _SEG_WIDTHS:
        assert width % INPROJ_TN == 0
        segs[name] = (first, width // INPROJ_TN)
        first += width // INPROJ_TN
    return segs, first


def _norm_kernel(x_ref, g_ref, wdt_ref, hn_ref, dt_ref):
    hn = (_rms(x_ref[...]) * g_ref[...]).astype(BF16)
    hn_ref[...] = hn
    dt_ref[...] = _dot(hn, wdt_ref[...])


def _proj_kernel(hn_ref, w_ref, *refs, mode, n_steps):
    acc = _dot(hn_ref[...], w_ref[...])
    if mode == "bf16":
        refs[0][...] = acc.astype(BF16)
    elif mode == "f32":
        refs[0][...] = acc
    else:
        heads_ref, stage_ref, sem = refs[0], refs[-2], refs[-1]
        refs[1][...] = acc.astype(BF16)
        if mode == "heads_bf16_t":
            refs[2][...] = acc.T.astype(BF16)
        i, j = pl.program_id(0), pl.program_id(1)
        step = i * pl.num_programs(1) + j
        slot = step % 2
        tm, tn = acc.shape
        heads_per_tile = tn // HEAD_W

        def copies(row0, head0, into):
            return [pltpu.make_async_copy(stage_ref.at[into, :, hh * HEAD_W:(hh + 1) * HEAD_W],
                                          heads_ref.at[pl.ds(row0, tm), head0 + hh, :], sem.at[into])
                    for hh in range(heads_per_tile)]

        stage_ref[slot] = acc
        for cp in copies(pl.multiple_of(i * tm, tm), j * heads_per_tile, slot):
            cp.start()

        @pl.when(step > 0)
        def _():
            for cp in copies(0, 0, 1 - slot):
                cp.wait()

        @pl.when(step == n_steps - 1)
        def _():
            for cp in copies(0, 0, slot):
                cp.wait()


def _proj(hn, w_main, name, mode):
    t = hn.shape[0]
    tm, tn = min(INPROJ_TM, t), INPROJ_TN
    lo, n = _segments()[0][name]
    assert t % tm == 0 and tn % HEAD_W == 0
    row_major = pl.BlockSpec((tm, tn), lambda i, j: (i, j))
    scratch = []
    if mode in ("bf16", "f32"):
        out_shape = [jax.ShapeDtypeStruct((t, n * tn), BF16 if mode == "bf16" else F32)]
        out_specs = [row_major]
    else:
        out_shape = [jax.ShapeDtypeStruct((t, n * tn // HEAD_W, HEAD_W), F32),
                     jax.ShapeDtypeStruct((t, n * tn), BF16)]
        out_specs = [pl.BlockSpec(memory_space=pl.ANY), row_major]
        scratch = [pltpu.VMEM((2, tm, tn), F32), pltpu.SemaphoreType.DMA((2,))]
        if mode == "heads_bf16_t":
            out_shape.append(jax.ShapeDtypeStruct((n * tn, t), BF16))
            out_specs.append(pl.BlockSpec((tn, tm), lambda i, j: (j, i)))
    return pl.pallas_call(
        functools.partial(_proj_kernel, mode=mode, n_steps=(t // tm) * n),
        out_shape=tuple(out_shape),
        grid=(t // tm, n),
        in_specs=[pl.BlockSpec((tm, D_MODEL), lambda i, j: (i, 0)),
                  pl.BlockSpec((D_MODEL, tn), lambda i, j: (0, lo + j))],
        out_specs=tuple(out_specs),
        scratch_shapes=scratch,
        compiler_params=pltpu.CompilerParams(dimension_semantics=("arbitrary", "arbitrary"),
                                             vmem_limit_bytes=40 * MIB),
        name="proj_" + name,
    )(hn, w_main)


def _inproj(x, g_mix, w_main, w_dt):
    t = x.shape[0]
    tm = TOK_TM
    assert t % tm == 0 and w_main.shape[1] == _segments()[1] * INPROJ_TN
    hn, dt = pl.pallas_call(
        _norm_kernel,
        out_shape=(jax.ShapeDtypeStruct((t, D_MODEL), BF16), jax.ShapeDtypeStruct((t, LANES), F32)),
        grid=(t // tm,),
        in_specs=[pl.BlockSpec((tm, D_MODEL), lambda i: (i, 0)),
                  pl.BlockSpec((1, D_MODEL), lambda i: (0, 0)),
                  pl.BlockSpec((D_MODEL, LANES), lambda i: (0, 0))],
        out_specs=(pl.BlockSpec((tm, D_MODEL), lambda i: (i, 0)), pl.BlockSpec((tm, LANES), lambda i: (i, 0))),
        compiler_params=pltpu.CompilerParams(dimension_semantics=("arbitrary",), vmem_limit_bytes=32 * MIB),
        name="norm_dt",
    )(x, g_mix, w_dt)
    (q,) = _proj(hn, w_main, "q", "bf16")
    k, kb = _proj(hn, w_main, "k", "heads_bf16")
    v, vb, vt = _proj(hn, w_main, "v", "heads_bf16_t")
    (z,) = _proj(hn, w_main, "z", "f32")
    (xbc,) = _proj(hn, w_main, "xbc", "f32")
    (ga,) = _proj(hn, w_main, "ga", "f32")
    (gs,) = _proj(hn, w_main, "gs", "f32")
    return q, k, v, kb, vb, vt, z, xbc, ga, gs, dt


def _t5_bucket(rel):
    nb = N_BUCKETS // 2
    max_exact = nb // 2
    ret = jnp.where(rel > 0, nb, 0)
    n = jnp.abs(rel)
    large = max_exact + (jnp.log(jnp.maximum(n, 1).astype(jnp.float32) / max_exact)
                         / math.log(MAX_DISTANCE / max_exact) * (nb - max_exact)).astype(jnp.int32)
    large = jnp.minimum(large, nb - 1)
    return ret + jnp.where(n < max_exact, n, large)


def _rel_bias_table(rel_bias, q_pos, k_pos):
    bucket = _t5_bucket(k_pos[None, :] - q_pos[:, None])
    return jnp.transpose(_lookup(rel_bias.astype(F32).T, bucket), (2, 0, 1))


def _toeplitz_bias(rel_bias, tq, d):
    n = 2 * tq
    rel = jnp.arange(n, dtype=jnp.int32) - (tq - 1) - d * tq
    w = _lookup(rel_bias.astype(F32).T, _t5_bucket(rel)).T
    skew = jnp.tile(w, (1, tq))[:, :tq * (n - 1)].reshape(w.shape[0], tq, n - 1)
    return skew[:, :, tq - 1:]


def _lookup(table, idx):
    n = table.shape[-1]
    hit = idx[..., None] == jnp.arange(n, dtype=idx.dtype)
    hit = hit.reshape(idx.shape + (1,) * (table.ndim - 1) + (n,))
    return jnp.sum(jnp.where(hit, table, jnp.zeros((), table.dtype)), axis=-1)


def _split_maps(qh):
    lane = lax.broadcasted_iota(jnp.int32, qh.shape, 1)
    q1 = jnp.where(lane < ATT_HEAD_DIM, qh, 0.0) * ATT_SCALE
    q2 = jnp.where(lane >= ATT_HEAD_DIM, qh, 0.0) * ATT_SCALE
    return q1.astype(BF16), q2.astype(BF16)


def _subln(o, lam_unused, sub):
    return (_rms(o) * sub) * (1.0 - LAM_INIT)


def _attn_prompt_kernel(qi_ref, kj_ref, q_ref, k_ref, vt_ref, bias_ref, far_ref, lam_ref, subt_ref, o_ref,
                        qs_ref, m_ref, l_ref, acc_ref):
    s = pl.program_id(0)
    qi = qi_ref[s]
    kj = kj_ref[s]
    tq = q_ref.shape[0]
    tk = k_ref.shape[0]

    @pl.when(kj == 0)
    def _():
        for h in range(N_ATT_HEADS):
            q1, q2 = _split_maps(q_ref[:, h * HEAD_W:(h + 1) * HEAD_W].astype(F32))
            qs_ref[2 * h] = q1
            qs_ref[2 * h + 1] = q2
        m_ref[...] = jnp.full(m_ref.shape, NEG_INF, F32)
        l_ref[...] = jnp.zeros(l_ref.shape, F32)
        acc_ref[...] = jnp.zeros(acc_ref.shape, F32)

    def step(mode):
        shift = CHUNK.bit_length() - 1
        key_chunk = jnp.right_shift(lax.broadcasted_iota(jnp.int32, (ATT_KB, ATT_QB), 0), shift)
        qry_chunk = jnp.right_shift(lax.broadcasted_iota(jnp.int32, (ATT_KB, ATT_QB), 1), shift)
        units = [(h, kb, c, qb) for h in range(N_ATT_HEADS) for kb in range(tk // ATT_KB)
                 for c in range(2) for qb in range(tq // ATT_QB)]
        for h, kb, c, qb in units:
                    key0, qry0 = kb * ATT_KB, qb * ATT_QB
                    if mode == "diag" and key0 >= qry0 + ATT_QB:
                        continue
                    kl = slice(key0, key0 + ATT_KB)
                    ql = slice(qry0, qry0 + ATT_QB)
                    idx = 2 * h + c
                    kh = k_ref[kl, h * HEAD_W:(h + 1) * HEAD_W]
                    vth = vt_ref[h * HEAD_W:(h + 1) * HEAD_W, kl]
                    sc = _dot_nt(kh, qs_ref[idx, ql, :])
                    if mode != "far":
                        sc = sc + bias_ref[0, h, kl, ql]
                    if mode == "diag" and key0 + ATT_KB > qry0:
                        visible = key_chunk + (key0 >> shift) <= qry_chunk + (qry0 >> shift)
                        sc = jnp.where(visible, sc, NEG_INF)
                    m_old = m_ref[idx, :, ql]
                    col_max = jnp.max(sc, axis=0, keepdims=True)
                    if mode == "far":
                        m_new = jnp.maximum(m_old, col_max + far_ref[h])
                        offset = m_new - far_ref[h]
                    else:
                        m_new = jnp.maximum(m_old, col_max)
                        offset = m_new
                    alpha = jnp.exp(m_old - m_new)
                    p = jnp.exp(sc - offset)
                    l_ref[idx, :, ql] = alpha * l_ref[idx, :, ql] + jnp.sum(p, axis=0, keepdims=True)
                    acc_ref[idx, :, ql] = alpha * acc_ref[idx, :, ql] + _dot(vth, p.astype(BF16))
                    m_ref[idx, :, ql] = m_new

    @pl.when(kj == qi)
    def _():
        step("diag")

    @pl.when(kj == qi - 1)
    def _():
        step("near")

    @pl.when(kj < qi - 1)
    def _():
        step("far")

    @pl.when(kj == qi)
    def _():
        lam = lam_ref[:, :1]
        subt = subt_ref[...]
        for h in range(N_ATT_HEADS):
            ot = acc_ref[2 * h] / l_ref[2 * h] - lam * (acc_ref[2 * h + 1] / l_ref[2 * h + 1])
            ms = jnp.mean(ot * ot, axis=0, keepdims=True)
            ont = ((ot * lax.rsqrt(ms + EPS)) * subt) * (1.0 - LAM_INIT)
            o_ref[:, h * HEAD_W:(h + 1) * HEAD_W] = ont.T.astype(BF16)


def _attn_prompt(q, k, vt, rel_bias, lam_vec, sub):
    t = q.shape[0]
    tq = ATT_TQ
    assert t % tq == 0 and tq % CHUNK == 0 and tq >= MAX_DISTANCE
    nq = t // tq
    qi = np.concatenate([np.full(i + 1, i, np.int32) for i in range(nq)])
    kj = np.concatenate([np.arange(i + 1, dtype=np.int32) for i in range(nq)])
    bias = jnp.stack([_toeplitz_bias(rel_bias, tq, d) for d in range(2)])
    bias = jnp.transpose(bias, (0, 1, 3, 2))
    far = _lookup(rel_bias.astype(F32).T, _t5_bucket(jnp.full((1,), -(tq + 1), jnp.int32)))[0]
    sub = sub.reshape(HEAD_W, 1)
    v = vt

    grid_spec = pltpu.PrefetchScalarGridSpec(
        num_scalar_prefetch=2,
        grid=(qi.shape[0],),
        in_specs=[pl.BlockSpec((tq, ATT_WIDTH), lambda s, qi, kj: (qi[s], 0)),
                  pl.BlockSpec((tq, ATT_WIDTH), lambda s, qi, kj: (kj[s], 0)),
                  pl.BlockSpec((ATT_WIDTH, tq), lambda s, qi, kj: (0, kj[s])),
                  pl.BlockSpec((1, N_ATT_HEADS, tq, tq),
                               lambda s, qi, kj: (jnp.minimum(qi[s] - kj[s], 1), 0, 0, 0)),
                  pl.BlockSpec(memory_space=pltpu.SMEM),
                  pl.BlockSpec((1, HEAD_W), lambda s, qi, kj: (0, 0)),
                  pl.BlockSpec((HEAD_W, 1), lambda s, qi, kj: (0, 0))],
        out_specs=pl.BlockSpec((tq, ATT_WIDTH), lambda s, qi, kj: (qi[s], 0)),
        scratch_shapes=[pltpu.VMEM((2 * N_ATT_HEADS, tq, HEAD_W), BF16),
                        pltpu.VMEM((2 * N_ATT_HEADS, 1, tq), F32),
                        pltpu.VMEM((2 * N_ATT_HEADS, 1, tq), F32),
                        pltpu.VMEM((2 * N_ATT_HEADS, HEAD_W, tq), F32)])
    return pl.pallas_call(
        _attn_prompt_kernel,
        out_shape=jax.ShapeDtypeStruct((t, ATT_WIDTH), BF16),
        grid_spec=grid_spec,
        compiler_params=pltpu.CompilerParams(dimension_semantics=("arbitrary",), vmem_limit_bytes=40 * MIB),
        name="attn_prompt",
    )(jnp.asarray(qi), jnp.asarray(kj), q, k, v, bias, far, lam_vec, sub)


def _attn_sample_kernel(q_ref, kn_ref, vn_ref, ck_hbm, cv_hbm, bc_ref, bn_ref, mc_ref, mn_ref, lam_ref,
                        sub_ref, o_ref, kbuf_ref, vbuf_ref, sem, *, n_batch):
    b = pl.program_id(0)
    slot = b % 2

    def cache_copies(bb, into):
        copies = []
        for h in range(N_ATT_HEADS):
            copies.append(pltpu.make_async_copy(ck_hbm.at[bb, :, h, :], kbuf_ref.at[into, h], sem.at[0, into]))
            copies.append(pltpu.make_async_copy(cv_hbm.at[bb, :, h, :], vbuf_ref.at[into, h], sem.at[1, into]))
        return copies

    @pl.when(b == 0)
    def _():
        for cp in cache_copies(0, 0):
            cp.start()

    @pl.when(b + 1 < n_batch)
    def _():
        for cp in cache_copies(b + 1, 1 - slot):
            cp.start()

    for cp in cache_copies(b, slot):
        cp.wait()

    lam = lam_ref[...]
    sub = sub_ref[...]
    vis_c = mc_ref[...] > 0.5
    vis_n = mn_ref[...] > 0.5
    for h in range(N_ATT_HEADS):
        hs = slice(h * HEAD_W, (h + 1) * HEAD_W)
        qmaps = _split_maps(q_ref[:, hs].astype(F32))
        kc = kbuf_ref[slot, h].astype(BF16)
        vc = vbuf_ref[slot, h].astype(BF16)
        kn = kn_ref[:, hs]
        vn = vn_ref[:, hs]
        probs = []
        for c in range(2):
            sc = jnp.where(vis_c, _dot_nt(qmaps[c], kc) + bc_ref[h], NEG_INF)
            sn = jnp.where(vis_n, _dot_nt(qmaps[c], kn) + bn_ref[h], NEG_INF)
            m = jnp.maximum(jnp.max(sc, axis=-1, keepdims=True), jnp.max(sn, axis=-1, keepdims=True))
            pc = jnp.exp(sc - m)
            pn = jnp.exp(sn - m)
            den = jnp.sum(pc, axis=-1, keepdims=True) + jnp.sum(pn, axis=-1, keepdims=True)
            probs.append((pc / den, pn / den))
        wc = probs[0][0] - lam[:, :1] * probs[1][0]
        wn = probs[0][1] - lam[:, :1] * probs[1][1]
        o = _dot(wc.astype(BF16), vc) + _dot(wn.astype(BF16), vn)
        o_ref[:, hs] = _subln(o, None, sub).astype(BF16)


def _attn_sample(q, kn, vn, cache_k, cache_v, rel_bias, lam_vec, sub):
    bsz, past = cache_k.shape[:2]
    seq = q.shape[0] // bsz
    q_pos = past + jnp.arange(seq, dtype=jnp.int32)
    k_pos = jnp.arange(past + seq, dtype=jnp.int32)
    bias = _rel_bias_table(rel_bias, q_pos, k_pos)
    visible = ((k_pos[None, :] // CHUNK) <= (q_pos[:, None] // CHUNK)).astype(F32)
    const = lambda *shape: pl.BlockSpec(shape, lambda b: (0,) * len(shape))
    row = pl.BlockSpec((seq, ATT_WIDTH), lambda b: (b, 0))
    cache = pl.BlockSpec(memory_space=pl.ANY)
    head_major = pltpu.VMEM((2, N_ATT_HEADS, past, HEAD_W), F32)
    return pl.pallas_call(
        functools.partial(_attn_sample_kernel, n_batch=bsz),
        out_shape=jax.ShapeDtypeStruct(q.shape, BF16),
        grid=(bsz,),
        in_specs=[row, row, row, cache, cache, const(N_ATT_HEADS, seq, past), const(N_ATT_HEADS, seq, seq),
                  const(seq, past), const(seq, seq), const(1, HEAD_W), const(1, HEAD_W)],
        out_specs=row,
        scratch_shapes=[head_major, head_major, pltpu.SemaphoreType.DMA((2, 2))],
        compiler_params=pltpu.CompilerParams(dimension_semantics=("arbitrary",), vmem_limit_bytes=40 * MIB),
        name="attn_sample",
    )(q, kn, vn, cache_k, cache_v, bias[:, :, :past], bias[:, :, past:], visible[:, :past], visible[:, past:],
      lam_vec, sub)


def _ssd_constants(l):
    hl = HEADS_PER_GROUP * l
    lp = SSD_LP
    e_head = np.zeros((LANES, SSM_D_INNER), np.float32)
    for h in range(N_SSM_HEADS):
        e_head[h, h * SSM_HEAD_DIM:(h + 1) * SSM_HEAD_DIM] = 1.0
    e_grp = np.zeros((N_SSM_GROUPS, LANES, hl), np.float32)
    for g in range(N_SSM_GROUPS):
        for r in range(HEADS_PER_GROUP):
            e_grp[g, g * HEADS_PER_GROUP + r, r * l:(r + 1) * l] = 1.0
    tile8 = np.zeros((lp, hl), np.float32)
    for r in range(HEADS_PER_GROUP):
        tile8[np.arange(l), r * l + np.arange(l)] = 1.0
    causal = np.zeros((l, hl), np.float32)
    for r in range(HEADS_PER_GROUP):
        causal[:, r * l:(r + 1) * l] = np.tril(np.ones((l, l), np.float32))
    tri = np.tril(np.ones((l, l), np.float32))
    ones = np.ones((l, l), np.float32)
    bmask = np.zeros((hl, GROUP_W), np.float32)
    for r in range(HEADS_PER_GROUP):
        bmask[r * l:(r + 1) * l, r * SSM_HEAD_DIM:(r + 1) * SSM_HEAD_DIM] = 1.0
    as_bf = lambda a: jnp.asarray(a, BF16)
    return (as_bf(e_head), as_bf(e_grp), as_bf(tile8), jnp.asarray(causal), as_bf(tri), as_bf(ones),
            as_bf(bmask))


def _ssd_kernel(xbc_ref, z_ref, dt_ref, hist_ref, st0_ref, cw_ref, cb_ref, dtb_ref, alog_ref, dskip_ref,
                norm_ref, eh_ref, eg_ref, t8_ref, caus_ref, tri_ref, ones_ref, bmask_ref,
                yz_ref, st_ref, buf_ref, state_ref):
    c = pl.program_id(1)
    l = xbc_ref.shape[0]
    lp = SSD_LP
    hist_rows = hist_ref.shape[0]

    @pl.when(c == 0)
    def _():
        buf_ref[0:hist_rows, :] = hist_ref[...]
        for g in range(N_SSM_GROUPS):
            state_ref[g] = st0_ref[g].T

    u = xbc_ref[...]
    buf_ref[hist_rows:hist_rows + l, :] = u
    conv = cb_ref[...] + cw_ref[CONV_WIDTH - 1:CONV_WIDTH, :] * u
    for w in range(CONV_WIDTH - 1):
        shift = CONV_WIDTH - 1 - w
        conv = conv + cw_ref[w:w + 1, :] * buf_ref[hist_rows - shift:hist_rows - shift + l, :]
    buf_ref[0:hist_rows, :] = buf_ref[l:l + hist_rows, :]
    xc = conv * _sigmoid(conv)
    xs = xc[:, :SSM_D_INNER]
    bm = xc[:, SSM_D_INNER:SSM_D_INNER + N_SSM_GROUPS * D_STATE]
    cm = xc[:, SSM_D_INNER + N_SSM_GROUPS * D_STATE:]

    dt_in = dt_ref[...] + dtb_ref[...]
    dt = jnp.maximum(dt_in, 0.0) + jnp.log(1.0 + jnp.exp(-jnp.abs(dt_in)))
    a = -jnp.exp(alog_ref[...])
    acum = _xdot_l(tri_ref[...], dt * a)
    acum_t = acum.T if l == LANES else None
    eh = eh_ref[...]
    dt_e = _xdot_r(dt, eh)
    ac_e = _xdot_r(acum, eh)
    a_last = ac_e[l - 1:l, :]
    ecum = jnp.exp(ac_e)
    xdt = xs * dt_e
    xdtw_b = (xdt * jnp.exp(a_last - ac_e)).astype(BF16)
    xdt_b = xdt.astype(BF16)
    drow = jnp.exp(a_last)
    z = z_ref[...]
    caus = caus_ref[...] > 0.5
    t8 = t8_ref[...]
    t8_mask = t8[0:l, :] > 0
    bmask = bmask_ref[...] > 0
    row_pad = lp - l

    for g in range(N_SSM_GROUPS):
        gs = slice(g * GROUP_W, (g + 1) * GROUP_W)
        ns = slice(g * D_STATE, (g + 1) * D_STATE)
        bm_g = bm[:, ns]
        cm_b = cm[:, ns].astype(BF16)
        xw_g = xdtw_b[:, gs]
        if row_pad:
            bm_g = jnp.concatenate([bm_g, jnp.zeros((row_pad, D_STATE), F32)], axis=0)
            xw_g = jnp.concatenate([xw_g, jnp.zeros((row_pad, GROUP_W), BF16)], axis=0)
        bmt_b = bm_g.T.astype(BF16)
        cb8 = _dot(cm_b, _dot(bmt_b, t8).astype(BF16))
        a1 = _xdot_r(acum, eg_ref[g])
        if l == LANES:
            heads = range(g * HEADS_PER_GROUP, (g + 1) * HEADS_PER_GROUP)
            a2 = jnp.concatenate([acum_t[hd:hd + 1, :] for hd in heads], axis=1)
        else:
            a2 = _xdot_l(ones_ref[...], jnp.where(t8_mask, a1, 0.0))
        decay = jnp.where(caus, jnp.exp(jnp.where(caus, a1 - a2, 0.0)), 0.0)
        m_b = (cb8 * decay).astype(BF16)
        xg = xdt_b[:, gs]
        bd = jnp.concatenate([xg] * HEADS_PER_GROUP, axis=0)
        bd = jnp.where(bmask, bd, jnp.zeros_like(bd))
        y = _dot(m_b, bd)
        st_g = state_ref[g]
        y = y + _dot(cm_b, st_g.astype(BF16)) * ecum[:, gs]
        y = y + dskip_ref[:, gs] * xs[:, gs]
        state_ref[g] = st_g * drow[:, gs] + _dot(bmt_b, xw_g)
        zg = z[:, gs]
        yz = y * (zg * _sigmoid(zg))
        yz_ref[:, gs] = (_rms(yz) * norm_ref[:, gs]).astype(BF16)

    @pl.when(c == pl.num_programs(1) - 1)
    def _():
        for g in range(N_SSM_GROUPS):
            st_ref[g] = state_ref[g].T


def _ssd(xbc, z, dt, hist8, st0, conv_w, conv_b, dt_bias, a_log, dskip_e, ssm_norm, bsz, l):
    rows = xbc.shape[0]
    seq = rows // bsz
    assert seq % l == 0 and l % SUBLANES == 0 and l <= SSD_LP and l >= SUBLANES
    nc = seq // l
    consts = _ssd_constants(l)
    rowblk = lambda width: pl.BlockSpec((l, width), lambda b, c: (b * nc + c, 0))
    const = lambda arr: pl.BlockSpec(arr.shape, lambda b, c: (0,) * arr.ndim)
    params = (conv_w, conv_b, dt_bias, a_log, dskip_e, ssm_norm)
    return pl.pallas_call(
        _ssd_kernel,
        out_shape=(jax.ShapeDtypeStruct((rows, SSM_D_INNER), BF16),
                   jax.ShapeDtypeStruct(st0.shape, F32)),
        grid=(bsz, nc),
        in_specs=[rowblk(CONV_DIM), rowblk(SSM_D_INNER), rowblk(LANES),
                  pl.BlockSpec((None,) + hist8.shape[1:], lambda b, c: (b, 0, 0)),
                  pl.BlockSpec((None,) + st0.shape[1:], lambda b, c: (b, 0, 0, 0))]
                 + [const(p) for p in params] + [const(k) for k in consts],
        out_specs=(rowblk(SSM_D_INNER), pl.BlockSpec((None,) + st0.shape[1:], lambda b, c: (b, 0, 0, 0))),
        scratch_shapes=[pltpu.VMEM((SUBLANES + l, CONV_DIM), F32),
                        pltpu.VMEM((N_SSM_GROUPS, D_STATE, GROUP_W), F32)],
        compiler_params=pltpu.CompilerParams(dimension_semantics=("arbitrary", "arbitrary"),
                                             vmem_limit_bytes=48 * MIB),
        name="ssd",
    )(xbc, z, dt, hist8, st0, *params, *consts)


def _mix_kernel(on_ref, yz_ref, ga_ref, gs_ref, wa_ref, ws_ref, o_ref):
    att = _dot(on_ref[...], wa_ref[...])
    ssm = _dot(yz_ref[...], ws_ref[...])
    o_ref[...] = (_sigmoid(ga_ref[...]) * att + _sigmoid(gs_ref[...]) * ssm).astype(BF16)


def _mix(on, yz, ga, gs, wa, ws):
    t = on.shape[0]
    tm = TOK_TM
    rowblk = lambda width: pl.BlockSpec((tm, width), lambda i: (i, 0))
    const = lambda arr: pl.BlockSpec(arr.shape, lambda i: (0,) * arr.ndim)
    return pl.pallas_call(
        _mix_kernel,
        out_shape=jax.ShapeDtypeStruct((t, D_MODEL), BF16),
        grid=(t // tm,),
        in_specs=[rowblk(ATT_WIDTH), rowblk(SSM_D_INNER), rowblk(D_MODEL), rowblk(D_MODEL), const(wa), const(ws)],
        out_specs=rowblk(D_MODEL),
        compiler_params=pltpu.CompilerParams(dimension_semantics=("arbitrary",), vmem_limit_bytes=48 * MIB),
        name="mix",
    )(on, yz, ga, gs, wa, ws)


def _pack_bf16_pairs(x):
    w = x.shape[1] // 2
    lo = lax.bitcast_convert_type(x[:, :w].astype(BF16).astype(F32), U32)
    hi = lax.bitcast_convert_type(x[:, w:].astype(BF16).astype(F32), U32)
    return hi | (lo >> 16)


def _unpack_bf16_pairs(words):
    lo = lax.bitcast_convert_type(words << 16, F32)
    hi = lax.bitcast_convert_type(words & jnp.uint32(0xFFFF0000), F32)
    return jnp.concatenate([lo, hi], axis=1).astype(BF16)


def _resid_kernel(x_ref, mixed_ref, wo_ref, g_ref, wr_ref, br_ref, lt_ref, cin_ref,
                  h_ref, xp_ref, route_ref, cnt_ref, carry_ref):
    i = pl.program_id(0)

    @pl.when(i == 0)
    def _():
        carry_ref[...] = cin_ref[...]

    h = x_ref[...] + _dot(mixed_ref[...], wo_ref[...])
    h_ref[...] = h
    xn = _rms(h) * g_ref[...]
    xp_ref[...] = _pack_bf16_pairs(xn)
    logits = _dot(xn.astype(BF16), wr_ref[...]) + br_ref[...]

    lane = lax.broadcasted_iota(jnp.int32, logits.shape, 1)
    lane_f = lane.astype(F32)
    rest = logits
    vals, idxs, sels = [], [], []
    for _ in range(TOP_K):
        m = jnp.max(rest, axis=-1, keepdims=True)
        idx = jnp.min(jnp.where(rest == m, lane_f, float(LANES)), axis=-1, keepdims=True)
        sel = lane_f == idx
        rest = jnp.where(sel, -jnp.inf, rest)
        vals.append(m)
        idxs.append(idx)
        sels.append(sel)
    exps = [jnp.exp(v - vals[0]) for v in vals]
    den = exps[0] + exps[1] + exps[2] + exps[3]
    onehot = jnp.zeros(logits.shape, F32)
    for sel in sels:
        onehot = onehot + jnp.where(sel, 1.0, 0.0)
    before = _dot(lt_ref[...], onehot.astype(BF16)) + carry_ref[...]
    route = jnp.zeros(logits.shape, F32)
    for k in range(TOP_K):
        pos = jnp.sum(jnp.where(sels[k], before, 0.0), axis=-1, keepdims=True)
        route = route + jnp.where(lane == k, idxs[k], 0.0)
        route = route + jnp.where(lane == TOP_K + k, exps[k] / den, 0.0)
        route = route + jnp.where(lane == 2 * TOP_K + k, pos, 0.0)
    route_ref[...] = route
    carry_ref[...] = carry_ref[...] + jnp.sum(onehot, axis=0, keepdims=True)
    cnt_ref[...] = carry_ref[...]


def _resid(x, mixed, wo, g_ffn, wr, br, counts_in):
    t = x.shape[0]
    tm = TOK_TM
    lt = jnp.asarray(np.tril(np.ones((tm, tm), np.float32), -1), BF16)
    rowblk = lambda width: pl.BlockSpec((tm, width), lambda i: (i, 0))
    const = lambda arr: pl.BlockSpec(arr.shape, lambda i: (0,) * arr.ndim)
    return pl.pallas_call(
        _resid_kernel,
        out_shape=(jax.ShapeDtypeStruct((t, D_MODEL), F32),
                   jax.ShapeDtypeStruct((t, D_MODEL // 2), U32),
                   jax.ShapeDtypeStruct((t, LANES), F32),
                   jax.ShapeDtypeStruct((1, LANES), F32)),
        grid=(t // tm,),
        in_specs=[rowblk(D_MODEL), rowblk(D_MODEL), const(wo), const(g_ffn), const(wr), const(br), const(lt),
                  const(counts_in)],
        out_specs=(rowblk(D_MODEL), rowblk(D_MODEL // 2), rowblk(LANES), pl.BlockSpec((1, LANES), lambda i: (0, 0))),
        scratch_shapes=[pltpu.VMEM((1, LANES), F32)],
        compiler_params=pltpu.CompilerParams(dimension_semantics=("arbitrary",), vmem_limit_bytes=48 * MIB),
        name="resid_route",
    )(x, mixed, wo, g_ffn, wr, br, lt, counts_in)


def _row_copy(src, src_row, dst, dst_row, sem):
    return pltpu.make_async_copy(src.at[pl.ds(src_row, 1)], dst.at[pl.ds(dst_row, 1)], sem)


def _dispatch_kernel(dest_ref, xp_ref, *rest, first):
    if first:
        xs_ref, sem, zero_ref = rest

        @pl.when(pl.program_id(0) == 0)
        def _():
            zero_ref[...] = jnp.zeros(zero_ref.shape, zero_ref.dtype)
            blk = zero_ref.shape[0]

            def fill(b):
                return pltpu.make_async_copy(zero_ref, xs_ref.at[pl.ds(pl.multiple_of(b * blk, blk), blk)], sem)

            def start_fill(b, carry):
                fill(b).start()
                return carry

            def wait_fill(b, carry):
                fill(b).wait()
                return carry

            lax.fori_loop(0, xs_ref.shape[0] // blk, start_fill, 0)
            lax.fori_loop(0, xs_ref.shape[0] // blk, wait_fill, 0)
    else:
        _, xs_ref, sem = rest

    def issue(t, carry):
        for k in range(TOP_K):
            _row_copy(xp_ref, t, xs_ref, dest_ref[t * TOP_K + k], sem).start()
        return carry

    lax.fori_loop(0, TOK_TM, issue, 0, unroll=DMA_UNROLL)

    def drain(t, carry):
        for k in range(TOP_K):
            _row_copy(xp_ref, 0, xs_ref, 0, sem).wait()
        return carry

    lax.fori_loop(0, TOK_TM, drain, 0, unroll=DMA_UNROLL)


def _dispatch(dest_flat, xp, xs, n_rows):
    t, width = xp.shape
    first = xs is None
    assert n_rows % TOK_TM == 0
    in_specs = [pl.BlockSpec((TOK_TM * TOP_K,), lambda i: (i,), memory_space=pltpu.SMEM),
                pl.BlockSpec((TOK_TM, width), lambda i: (i, 0))]
    scratch = [pltpu.SemaphoreType.DMA(())]
    operands = [dest_flat, xp]
    if first:
        scratch.append(pltpu.VMEM((TOK_TM, width), xp.dtype))
    else:
        in_specs.append(pl.BlockSpec(memory_space=pl.ANY))
        operands.append(xs)
    return pl.pallas_call(
        functools.partial(_dispatch_kernel, first=first),
        out_shape=jax.ShapeDtypeStruct((n_rows, width), xp.dtype),
        grid=(t // TOK_TM,),
        in_specs=in_specs,
        out_specs=pl.BlockSpec(memory_space=pl.ANY),
        scratch_shapes=scratch,
        input_output_aliases={} if first else {2: 0},
        compiler_params=pltpu.CompilerParams(dimension_semantics=("arbitrary",)),
        name="dispatch_first" if first else "dispatch",
    )(*operands)


def _experts_kernel(sbe_ref, sbs_ref, sbr_ref, nsb_ref, xs_ref, wg_ref, wu_ref, wd_ref, bg_ref, bu_ref,
                    bd_ref, y_ref, xw_ref, xb_ref, acc_ref, wgb_ref, wub_ref, wdb_ref, sem_in, sem_out):
    s = pl.program_id(0)
    f = pl.program_id(1)
    nf = pl.num_programs(1)
    n_sub = SUPER_ROWS // ROW_BLK

    @pl.when(s < nsb_ref[0])
    def _():
        start = pl.multiple_of(sbs_ref[s], ROW_BLK)
        rows = sbr_ref[s]

        def x_copy(sb, slot):
            src = xs_ref.at[pl.ds(pl.multiple_of(sbs_ref[sb], ROW_BLK), SUPER_ROWS)]
            return pltpu.make_async_copy(src, xw_ref.at[slot], sem_in.at[slot])

        @pl.when(f == 0)
        def _():
            slot = s % 2

            @pl.when(s == 0)
            def _():
                x_copy(0, 0).start()

            x_copy(s, slot).wait()

            @pl.when(s + 1 < nsb_ref[0])
            def _():
                x_copy(s + 1, 1 - slot).start()

            xb_ref[...] = _unpack_bf16_pairs(xw_ref[slot])
            acc_ref[...] = jnp.zeros(acc_ref.shape, F32)

        wgb_ref[...] = wg_ref[0].astype(BF16)
        wub_ref[...] = wu_ref[0].astype(BF16)
        wdb_ref[...] = wd_ref[0].astype(BF16)
        half_rows = SUPER_ROWS // 2
        for half in range(2):
            rs = slice(half * half_rows, (half + 1) * half_rows)
            x = xb_ref[rs, :]
            gate = _dot(x, wgb_ref[...]) + bg_ref[0]
            up = _dot(x, wub_ref[...]) + bu_ref[0]
            gate = jnp.minimum(gate, SWIGLU_LIMIT)
            up = jnp.clip(up, -SWIGLU_LIMIT, SWIGLU_LIMIT)
            act = (up + 1.0) * gate * _sigmoid(SWIGLU_ALPHA * gate)
            acc_ref[rs, :] += _dot(act.astype(BF16), wdb_ref[...])

        @pl.when(f == nf - 1)
        def _():
            def out_copy(sub):
                rs = pl.ds(sub * ROW_BLK, ROW_BLK)
                return pltpu.make_async_copy(acc_ref.at[rs], y_ref.at[pl.ds(start + sub * ROW_BLK, ROW_BLK)],
                                             sem_out)

            for sub in range(n_sub):
                @pl.when(sub * ROW_BLK < rows)
                def _():
                    rs = slice(sub * ROW_BLK, (sub + 1) * ROW_BLK)
                    acc_ref[rs, :] += bd_ref[0]
                    out_copy(sub).start()

            for sub in range(n_sub):
                @pl.when(sub * ROW_BLK < rows)
                def _():
                    out_copy(sub).wait()

    @pl.when((s == pl.num_programs(0) - 1) & (f == nf - 1))
    def _():
        zero_ref = acc_ref.at[pl.ds(0, ROW_BLK)]
        zero_ref[...] = jnp.zeros(zero_ref.shape, F32)
        n_blocks = y_ref.shape[0] // ROW_BLK

        def tail_copy(b):
            return pltpu.make_async_copy(zero_ref, y_ref.at[pl.ds(pl.multiple_of(b * ROW_BLK, ROW_BLK), ROW_BLK)],
                                         sem_out)

        def issue(b, carry):
            tail_copy(b).start()
            return carry

        def drain(b, carry):
            tail_copy(b).wait()
            return carry

        lax.fori_loop(nsb_ref[1], n_blocks, issue, 0)
        lax.fori_loop(nsb_ref[1], n_blocks, drain, 0)


def _experts(sb_expert, sb_start, sb_rows, n_sb, xs, w_gate_up, b_gate_up, w_down, b_down, n_rows):
    n_super = sb_expert.shape[0]
    nf = D_FF // FF_TILE

    def widx(s, f, sbe, sbs, sbr, nsb):
        live = s < nsb[0]
        return sbe[s], jnp.where(live, f, nf - 1)

    def gate_map(s, f, *pref):
        e, ff = widx(s, f, *pref)
        return (e, 0, ff)

    def up_map(s, f, *pref):
        e, ff = widx(s, f, *pref)
        return (e, 0, nf + ff)

    def down_map(s, f, *pref):
        e, ff = widx(s, f, *pref)
        return (e, ff, 0)

    def bias_map(s, f, *pref):
        return (widx(s, f, *pref)[0], 0, 0)

    b_gu3 = b_gate_up.reshape(N_EXPERTS, 1, 2 * D_FF)
    b_dn3 = b_down.reshape(N_EXPERTS, 1, D_MODEL)
    grid_spec = pltpu.PrefetchScalarGridSpec(
        num_scalar_prefetch=4,
        grid=(n_super, nf),
        in_specs=[pl.BlockSpec(memory_space=pl.ANY),
                  pl.BlockSpec((1, D_MODEL, FF_TILE), gate_map),
                  pl.BlockSpec((1, D_MODEL, FF_TILE), up_map),
                  pl.BlockSpec((1, FF_TILE, D_MODEL), down_map),
                  pl.BlockSpec((1, 1, FF_TILE), gate_map),
                  pl.BlockSpec((1, 1, FF_TILE), up_map),
                  pl.BlockSpec((1, 1, D_MODEL), bias_map)],
        out_specs=pl.BlockSpec(memory_space=pl.ANY),
        scratch_shapes=[pltpu.VMEM((2, SUPER_ROWS, D_MODEL // 2), U32),
                        pltpu.VMEM((SUPER_ROWS, D_MODEL), BF16),
                        pltpu.VMEM((SUPER_ROWS, D_MODEL), F32),
                        pltpu.VMEM((D_MODEL, FF_TILE), BF16),
                        pltpu.VMEM((D_MODEL, FF_TILE), BF16),
                        pltpu.VMEM((FF_TILE, D_MODEL), BF16),
                        pltpu.SemaphoreType.DMA((2,)),
                        pltpu.SemaphoreType.DMA(())])
    return pl.pallas_call(
        _experts_kernel,
        out_shape=jax.ShapeDtypeStruct((n_rows, D_MODEL), F32),
        grid_spec=grid_spec,
        compiler_params=pltpu.CompilerParams(dimension_semantics=("arbitrary", "arbitrary"),
                                             vmem_limit_bytes=56 * MIB),
        name="experts",
    )(sb_expert, sb_start, sb_rows, n_sb, xs, w_gate_up, w_gate_up, w_down, b_gu3, b_gu3, b_dn3)


def _combine_kernel(dest_ref, dest_next_ref, h_ref, route_ref, p_ref, y_ref, gple_ref, wpg_ref, wpp_ref,
                    gfin_ref, o_ref, gbuf_ref, sem, *, n_tiles):
    i = pl.program_id(0)
    slot = i % 2

    def gather_tile(d_ref, into):
        def issue(t, carry):
            for k in range(TOP_K):
                _row_copy(y_ref, d_ref[t * TOP_K + k], gbuf_ref.at[into].at[k], t, sem.at[into]).start()
            return carry

        lax.fori_loop(0, TOK_TM, issue, 0, unroll=DMA_UNROLL)

    @pl.when(i == 0)
    def _():
        gather_tile(dest_ref, 0)

    @pl.when(i + 1 < n_tiles)
    def _():
        gather_tile(dest_next_ref, 1 - slot)

    def drain(t, carry):
        for k in range(TOP_K):
            _row_copy(y_ref, 0, gbuf_ref.at[slot].at[k], 0, sem.at[slot]).wait()
        return carry

    lax.fori_loop(0, TOK_TM, drain, 0, unroll=DMA_UNROLL)

    route = route_ref[...]
    h = h_ref[...]
    for k in range(TOP_K):
        h = h + route[:, TOP_K + k:TOP_K + k + 1] * gbuf_ref[slot, k]
    xn = (_rms(h) * gple_ref[...]).astype(BF16)
    gate = _sigmoid(_dot(xn, wpg_ref[...]))
    h = h + gate * _dot(p_ref[...].astype(BF16), wpp_ref[...])
    o_ref[...] = _rms(h) * gfin_ref[...]


def _combine(dest_flat, h, route, p, y_sorted, g_ple, wpg, wpp, g_final):
    t = h.shape[0]
    tm = TOK_TM
    rowblk = lambda width: pl.BlockSpec((tm, width), lambda i: (i, 0))
    const = lambda arr: pl.BlockSpec(arr.shape, lambda i: (0,) * arr.ndim)
    return pl.pallas_call(
        functools.partial(_combine_kernel, n_tiles=t // tm),
        out_shape=jax.ShapeDtypeStruct((t, D_MODEL), F32),
        grid=(t // tm,),
        in_specs=[pl.BlockSpec((tm * TOP_K,), lambda i: (i,), memory_space=pltpu.SMEM),
                  pl.BlockSpec((tm * TOP_K,), lambda i: (jnp.minimum(i + 1, t // tm - 1),),
                               memory_space=pltpu.SMEM),
                  rowblk(D_MODEL), rowblk(LANES), rowblk(PLE_DIM),
                  pl.BlockSpec(memory_space=pl.ANY),
                  const(g_ple), const(wpg), const(wpp), const(g_final)],
        out_specs=rowblk(D_MODEL),
        scratch_shapes=[pltpu.VMEM((2, TOP_K, tm, D_MODEL), F32), pltpu.SemaphoreType.DMA((2,))],
        compiler_params=pltpu.CompilerParams(dimension_semantics=("arbitrary",), vmem_limit_bytes=52 * MIB),
        name="combine",
    )(dest_flat, dest_flat, h, route, p, y_sorted, g_ple, wpg, wpp, g_final)


def _super_blocks(counts, n_super):
    padded = (counts + ROW_BLK - 1) // ROW_BLK * ROW_BLK
    pad_start = jnp.cumsum(padded) - padded
    per_expert = (counts + SUPER_ROWS - 1) // SUPER_ROWS
    sb_end = jnp.cumsum(per_expert)
    n_sb = sb_end[-1]
    s = jnp.arange(n_super, dtype=jnp.int32)
    s_live = jnp.minimum(s, jnp.maximum(n_sb - 1, 0))
    expert = jnp.sum(sb_end[None, :] <= s_live[:, None], axis=1).astype(jnp.int32)
    expert = jnp.minimum(expert, N_EXPERTS - 1)
    within = s_live - _lookup(sb_end - per_expert, expert)
    start = _lookup(pad_start, expert) + within * SUPER_ROWS
    rows = jnp.where(s < n_sb, jnp.clip(_lookup(counts, expert) - within * SUPER_ROWS, 0, SUPER_ROWS), 0)
    first_unused_block = jnp.sum(padded) // ROW_BLK
    return (pad_start, expert, start.astype(jnp.int32), rows.astype(jnp.int32),
            jnp.stack([n_sb, first_unused_block]).astype(jnp.int32))


def _mixer(x, cache_k, cache_v, st0, conv0, prep, ssd_l):
    bsz, seq, _ = x.shape
    x2 = x.reshape(bsz * seq, D_MODEL)
    q, k, v, kb, vb, vt, z, xbc, ga, gs, dt = _inproj(x2, prep["g_mix"], prep["w_main"], prep["w_dt"])
    if cache_k is None:
        assert bsz == 1
        on = _attn_prompt(q, kb, vt, prep["rel_bias"], prep["lam"], prep["subln"])
    else:
        past = cache_k.shape[1]
        on = _attn_sample(q, kb, vb, cache_k, cache_v, prep["rel_bias"], prep["lam"], prep["subln"])
    hist8 = jnp.pad(conv0, ((0, 0), (SUBLANES - (CONV_WIDTH - 1), 0), (0, 0)))
    st0_t = st0.reshape(bsz, N_SSM_GROUPS, GROUP_W, D_STATE)
    yz, st_t = _ssd(xbc, z, dt, hist8, st0_t, prep["conv_w"], prep["conv_b"], prep["dt_bias"], prep["a_log"],
                    prep["dskip_e"], prep["ssm_norm"], bsz, ssd_l)
    st_new = st_t.reshape(1, bsz, N_SSM_HEADS, SSM_HEAD_DIM, D_STATE)
    mixed = _mix(on, yz, ga, gs, prep["w_attn_out"], prep["w_ssm_out"])
    k_rows = k.reshape(1, bsz, seq, N_ATT_HEADS, HEAD_W)
    v_rows = v.reshape(1, bsz, seq, N_ATT_HEADS, HEAD_W)
    conv_new = xbc.reshape(bsz, seq, CONV_DIM)[:, seq - (CONV_WIDTH - 1):].reshape(1, bsz, CONV_WIDTH - 1, CONV_DIM)
    return x2, mixed, k_rows, v_rows, st_new, conv_new


def kernel(x_prompt, x_sample, cache_k, cache_v, state_ssm, state_conv, p_prompt, p_sample, rel_bias, w_in,
           lambda_q1, lambda_k1, lambda_q2, lambda_k2, attn_subln, w_attn_out, conv_w, conv_b, dt_bias, a_log,
           d_skip, ssm_norm, w_ssm_out, w_o, g_mix, g_ffn, w_router, b_router, w_gate_up, b_gate_up, w_down,
           b_down, g_ple, w_ple_gate, w_ple_proj, g_final):
    w = w_in[0]
    c_dt = 3 * ATT_WIDTH + SSM_D_INNER + CONV_DIM
    w_main = jnp.concatenate([w[:, :c_dt], w[:, c_dt + N_SSM_HEADS:]], axis=1).astype(BF16)
    w_dt = jnp.pad(w[:, c_dt:c_dt + N_SSM_HEADS], ((0, 0), (0, LANES - N_SSM_HEADS))).astype(BF16)
    lam = (jnp.exp(jnp.sum(lambda_q1[0] * lambda_k1[0]).astype(F32))
           - jnp.exp(jnp.sum(lambda_q2[0] * lambda_k2[0]).astype(F32)) + LAM_INIT)
    pad_heads = lambda v: jnp.pad(v.reshape(1, N_SSM_HEADS), ((0, 0), (0, LANES - N_SSM_HEADS)))
    prep = dict(
        g_mix=g_mix, w_main=w_main, w_dt=w_dt, rel_bias=rel_bias,
        lam=jnp.full((1, HEAD_W), lam, F32), subln=attn_subln,
        conv_w=conv_w[0], conv_b=conv_b, dt_bias=pad_heads(dt_bias[0]), a_log=pad_heads(a_log[0]),
        dskip_e=jnp.repeat(d_skip[0], SSM_HEAD_DIM).reshape(1, SSM_D_INNER), ssm_norm=ssm_norm,
        w_attn_out=w_attn_out[0].astype(BF16), w_ssm_out=w_ssm_out[0].astype(BF16))
    wo = w_o[0].astype(BF16)
    wr = jnp.pad(w_router[0], ((0, 0), (0, LANES - N_EXPERTS))).astype(BF16)
    br = jnp.pad(b_router, ((0, 0), (0, LANES - N_EXPERTS)), constant_values=NEG_INF)
    wpg = w_ple_gate[0].astype(BF16)
    wpp = w_ple_proj[0].astype(BF16)

    bp, sp, _ = x_prompt.shape
    bs, ss, _ = x_sample.shape
    zeros_state = jnp.zeros((bp, N_SSM_HEADS, SSM_HEAD_DIM, D_STATE), F32)
    zeros_conv = jnp.zeros((bp, CONV_WIDTH - 1, CONV_DIM), F32)
    xp2, mixed_p, k_p, v_p, ssm_p, conv_p = _mixer(x_prompt, None, None, zeros_state, zeros_conv, prep, SSD_L_PROMPT)
    xs2, mixed_s, k_s, v_s, ssm_s, conv_s = _mixer(x_sample, cache_k[0], cache_v[0], state_ssm[0], state_conv[0],
                                                   prep, ss)

    zero_counts = jnp.zeros((1, LANES), F32)
    h_p, xpk_p, route_p, cnt_p = _resid(xp2, mixed_p, wo, g_ffn, wr, br, zero_counts)
    h_s, xpk_s, route_s, cnt = _resid(xs2, mixed_s, wo, g_ffn, wr, br, cnt_p)
    n_tok = xp2.shape[0] + xs2.shape[0]
    n_slots = n_tok * TOP_K
    n_super = -(-n_slots // SUPER_ROWS) + N_EXPERTS
    n_rows = n_slots + N_EXPERTS * ROW_BLK
    counts = cnt[0, :N_EXPERTS].astype(jnp.int32)
    pad_start, sb_expert, sb_start, sb_rows, n_sb = _super_blocks(counts, n_super)

    def dest_of(route):
        expert = route[:, :TOP_K].astype(jnp.int32)
        pos = route[:, 2 * TOP_K:3 * TOP_K].astype(jnp.int32)
        return (_lookup(pad_start, expert) + pos).astype(jnp.int32).reshape(-1)

    dest_p = dest_of(route_p)
    dest_s = dest_of(route_s)
    xs_sorted = _dispatch(dest_p, xpk_p, None, n_rows + SUPER_ROWS)
    xs_sorted = _dispatch(dest_s, xpk_s, xs_sorted, n_rows + SUPER_ROWS)
    y_sorted = _experts(sb_expert, sb_start, sb_rows, n_sb, xs_sorted, w_gate_up[0], b_gate_up[0], w_down[0],
                        b_down[0], n_rows)

    y_p = _combine(dest_p, h_p, route_p, p_prompt[0].reshape(-1, PLE_DIM), y_sorted, g_ple, wpg, wpp,
                   g_final.reshape(1, D_MODEL))
    y_s = _combine(dest_s, h_s, route_s, p_sample[0].reshape(-1, PLE_DIM), y_sorted, g_ple, wpg, wpp,
                   g_final.reshape(1, D_MODEL))
    return (y_p.reshape(bp, sp, D_MODEL), y_s.reshape(bs, ss, D_MODEL), k_p, v_p, ssm_p, conv_p,
            k_s, v_s, ssm_s, conv_s)
```

```python
import functools
import math

import numpy as np
import jax
import jax.numpy as jnp
from jax import lax
from jax.experimental import pallas as pl
from jax.experimental.pallas import tpu as pltpu

F32 = jnp.float32
BF16 = jnp.bfloat16
U32 = jnp.uint32

D_MODEL = 2048
CHUNK = 64
N_ATT_HEADS = 8
ATT_HEAD_DIM = 64
HEAD_W = 2 * ATT_HEAD_DIM
ATT_WIDTH = N_ATT_HEADS * HEAD_W
ATT_SCALE = ATT_HEAD_DIM ** -0.5
N_BUCKETS = 32
MAX_DISTANCE = 128
NEG_INF = -1e30
SSM_D_INNER = 2048
SSM_HEAD_DIM = 64
N_SSM_HEADS = SSM_D_INNER // SSM_HEAD_DIM
N_SSM_GROUPS = 4
HEADS_PER_GROUP = N_SSM_HEADS // N_SSM_GROUPS
GROUP_W = HEADS_PER_GROUP * SSM_HEAD_DIM
D_STATE = 128
CONV_WIDTH = 4
CONV_DIM = SSM_D_INNER + 2 * N_SSM_GROUPS * D_STATE
N_EXPERTS = 32
TOP_K = 4
D_FF = 2048
SWIGLU_LIMIT = 7.0
SWIGLU_ALPHA = 1.702
PLE_DIM = 256
EPS = 1e-6
LAM_INIT = 0.8 - 0.6 * math.exp(-0.3 * 0)

LANES = 128
SUBLANES = 8
MIB = 1024 * 1024

INPROJ_TM = 1024
INPROJ_TN = 512
ATT_TQ = 512
ATT_QB = 128
ATT_KB = 256
SSD_L_PROMPT = 128
SSD_LP = 128
TOK_TM = 256
ROW_BLK = 256
SUPER_ROWS = 1280
FF_TILE = 512
DMA_UNROLL = 8


def _dot(a, b):
    return jnp.dot(a, b, preferred_element_type=F32)


def _dot_nt(a, b):
    return lax.dot_general(a, b, (((1,), (1,)), ((), ())), preferred_element_type=F32)


def _rms(x):
    return x * lax.rsqrt(jnp.mean(x * x, axis=-1, keepdims=True) + EPS)


def _sigmoid(x):
    return 1.0 / (1.0 + jnp.exp(-x))


def _split3(x):
    hi = x.astype(BF16)
    r = x - hi.astype(F32)
    mid = r.astype(BF16)
    lo = (r - mid.astype(F32)).astype(BF16)
    return hi, mid, lo


def _xdot_r(x, c):
    hi, mid, lo = _split3(x)
    return (_dot(hi, c) + _dot(mid, c)) + _dot(lo, c)


def _xdot_l(c, x):
    hi, mid, lo = _split3(x)
    return (_dot(c, hi) + _dot(c, mid)) + _dot(c, lo)


_SEG_WIDTHS = (("q", ATT_WIDTH), ("k", ATT_WIDTH), ("v", ATT_WIDTH), ("z", SSM_D_INNER),
               ("xbc", CONV_DIM), ("ga", D_MODEL), ("gs", D_MODEL))


def _segments():
    segs, first = {}, 0
    for name, width in _SEG_WIDTHS:
        assert width % INPROJ_TN == 0
        segs[name] = (first, width // INPROJ_TN)
        first += width // INPROJ_TN
    return segs, first


def _norm_kernel(x_ref, g_ref, wdt_ref, hn_ref, dt_ref):
    hn = (_rms(x_ref[...]) * g_ref[...]).astype(BF16)
    hn_ref[...] = hn
    dt_ref[...] = _dot(hn, wdt_ref[...])


def _proj_kernel(hn_ref, w_ref, *refs, mode, n_steps):
    acc = _dot(hn_ref[...], w_ref[...])
    if mode == "bf16":
        refs[0][...] = acc.astype(BF16)
    elif mode == "f32":
        refs[0][...] = acc
    else:
        heads_ref, stage_ref, sem = refs[0], refs[-2], refs[-1]
        refs[1][...] = acc.astype(BF16)
        if mode == "heads_bf16_t":
            refs[2][...] = acc.T.astype(BF16)
        i, j = pl.program_id(0), pl.program_id(1)
        step = i * pl.num_programs(1) + j
        slot = step % 2
        tm, tn = acc.shape
        heads_per_tile = tn // HEAD_W

        def copies(row0, head0, into):
            return [pltpu.make_async_copy(stage_ref.at[into, :, hh * HEAD_W:(hh + 1) * HEAD_W],
                                          heads_ref.at[pl.ds(row0, tm), head0 + hh, :], sem.at[into])
                    for hh in range(heads_per_tile)]

        stage_ref[slot] = acc
        for cp in copies(pl.multiple_of(i * tm, tm), j * heads_per_tile, slot):
            cp.start()

        @pl.when(step > 0)
        def _():
            for cp in copies(0, 0, 1 - slot):
                cp.wait()

        @pl.when(step == n_steps - 1)
        def _():
            for cp in copies(0, 0, slot):
                cp.wait()


def _proj(hn, w_main, name, mode):
    t = hn.shape[0]
    tm, tn = min(INPROJ_TM, t), INPROJ_TN
    segs = _segments()[0]
    lo, n = segs[name]
    split = segs["ga"][0]
    w_main, lo = (w_main[0], lo) if lo < split else (w_main[1], lo - split)
    assert t % tm == 0 and tn % HEAD_W == 0 and (lo + n) * tn <= w_main.shape[1]
    row_major = pl.BlockSpec((tm, tn), lambda i, j: (i, j))
    scratch = []
    if mode in ("bf16", "f32"):
        out_shape = [jax.ShapeDtypeStruct((t, n * tn), BF16 if mode == "bf16" else F32)]
        out_specs = [row_major]
    else:
        out_shape = [jax.ShapeDtypeStruct((t, n * tn // HEAD_W, HEAD_W), F32),
                     jax.ShapeDtypeStruct((t, n * tn), BF16)]
        out_specs = [pl.BlockSpec(memory_space=pl.ANY), row_major]
        scratch = [pltpu.VMEM((2, tm, tn), F32), pltpu.SemaphoreType.DMA((2,))]
        if mode == "heads_bf16_t":
            out_shape.append(jax.ShapeDtypeStruct((n * tn, t), BF16))
            out_specs.append(pl.BlockSpec((tn, tm), lambda i, j: (j, i)))
    return pl.pallas_call(
        functools.partial(_proj_kernel, mode=mode, n_steps=(t // tm) * n),
        out_shape=tuple(out_shape),
        grid=(t // tm, n),
        in_specs=[pl.BlockSpec((tm, D_MODEL), lambda i, j: (i, 0)),
                  pl.BlockSpec((D_MODEL, tn), lambda i, j: (0, lo + j))],
        out_specs=tuple(out_specs),
        scratch_shapes=scratch,
        compiler_params=pltpu.CompilerParams(dimension_semantics=("arbitrary", "arbitrary"),
                                             vmem_limit_bytes=40 * MIB),
        name="proj_" + name,
    )(hn, w_main)


def _inproj(x, g_mix, w_main, w_dt):
    t = x.shape[0]
    tm = TOK_TM
    assert t % tm == 0 and sum(w.shape[1] for w in w_main) == _segments()[1] * INPROJ_TN
    hn, dt = pl.pallas_call(
        _norm_kernel,
        out_shape=(jax.ShapeDtypeStruct((t, D_MODEL), BF16), jax.ShapeDtypeStruct((t, LANES), F32)),
        grid=(t // tm,),
        in_specs=[pl.BlockSpec((tm, D_MODEL), lambda i: (i, 0)),
                  pl.BlockSpec((1, D_MODEL), lambda i: (0, 0)),
                  pl.BlockSpec((D_MODEL, LANES), lambda i: (0, 0))],
        out_specs=(pl.BlockSpec((tm, D_MODEL), lambda i: (i, 0)), pl.BlockSpec((tm, LANES), lambda i: (i, 0))),
        compiler_params=pltpu.CompilerParams(dimension_semantics=("arbitrary",), vmem_limit_bytes=32 * MIB),
        name="norm_dt",
    )(x, g_mix, w_dt)
    (q,) = _proj(hn, w_main, "q", "bf16")
    k, kb = _proj(hn, w_main, "k", "heads_bf16")
    v, vb, vt = _proj(hn, w_main, "v", "heads_bf16_t")
    (z,) = _proj(hn, w_main, "z", "f32")
    (xbc,) = _proj(hn, w_main, "xbc", "f32")
    (ga,) = _proj(hn, w_main, "ga", "f32")
    (gs,) = _proj(hn, w_main, "gs", "f32")
    return q, k, v, kb, vb, vt, z, xbc, ga, gs, dt


def _t5_bucket(rel):
    nb = N_BUCKETS // 2
    max_exact = nb // 2
    ret = jnp.where(rel > 0, nb, 0)
    n = jnp.abs(rel)
    large = max_exact + (jnp.log(jnp.maximum(n, 1).astype(jnp.float32) / max_exact)
                         / math.log(MAX_DISTANCE / max_exact) * (nb - max_exact)).astype(jnp.int32)
    large = jnp.minimum(large, nb - 1)
    return ret + jnp.where(n < max_exact, n, large)


def _rel_bias_table(rel_bias, q_pos, k_pos):
    bucket = _t5_bucket(k_pos[None, :] - q_pos[:, None])
    return jnp.transpose(_lookup(rel_bias.astype(F32).T, bucket), (2, 0, 1))


def _toeplitz_bias(rel_bias, tq, d):
    n = 2 * tq
    rel = jnp.arange(n, dtype=jnp.int32) - (tq - 1) - d * tq
    w = _lookup(rel_bias.astype(F32).T, _t5_bucket(rel)).T
    skew = jnp.tile(w, (1, tq))[:, :tq * (n - 1)].reshape(w.shape[0], tq, n - 1)
    return skew[:, :, tq - 1:]


def _lookup(table, idx):
    n = table.shape[-1]
    hit = idx[..., None] == jnp.arange(n, dtype=idx.dtype)
    hit = hit.reshape(idx.shape + (1,) * (table.ndim - 1) + (n,))
    return jnp.sum(jnp.where(hit, table, jnp.zeros((), table.dtype)), axis=-1)


def _split_maps(qh):
    lane = lax.broadcasted_iota(jnp.int32, qh.shape, 1)
    q1 = jnp.where(lane < ATT_HEAD_DIM, qh, 0.0) * ATT_SCALE
    q2 = jnp.where(lane >= ATT_HEAD_DIM, qh, 0.0) * ATT_SCALE
    return q1.astype(BF16), q2.astype(BF16)


def _subln(o, lam_unused, sub):
    return (_rms(o) * sub) * (1.0 - LAM_INIT)


def _attn_prompt_kernel(qi_ref, kj_ref, q_ref, k_ref, vt_ref, bias_ref, far_ref, lam_ref, subt_ref, o_ref,
                        qs_ref, m_ref, l_ref, acc_ref):
    s = pl.program_id(0)
    qi = qi_ref[s]
    kj = kj_ref[s]
    tq = q_ref.shape[0]
    tk = k_ref.shape[0]

    @pl.when(kj == 0)
    def _():
        for h in range(N_ATT_HEADS):
            q1, q2 = _split_maps(q_ref[:, h * HEAD_W:(h + 1) * HEAD_W].astype(F32))
            qs_ref[2 * h] = q1
            qs_ref[2 * h + 1] = q2
        m_ref[...] = jnp.full(m_ref.shape, NEG_INF, F32)
        l_ref[...] = jnp.zeros(l_ref.shape, F32)
        acc_ref[...] = jnp.zeros(acc_ref.shape, F32)

    def step(mode):
        shift = CHUNK.bit_length() - 1
        key_chunk = jnp.right_shift(lax.broadcasted_iota(jnp.int32, (ATT_KB, ATT_QB), 0), shift)
        qry_chunk = jnp.right_shift(lax.broadcasted_iota(jnp.int32, (ATT_KB, ATT_QB), 1), shift)
        units = [(h, kb, c, qb) for h in range(N_ATT_HEADS) for kb in range(tk // ATT_KB)
                 for c in range(2) for qb in range(tq // ATT_QB)]
        for h, kb, c, qb in units:
                    key0, qry0 = kb * ATT_KB, qb * ATT_QB
                    if mode == "diag" and key0 >= qry0 + ATT_QB:
                        continue
                    kl = slice(key0, key0 + ATT_KB)
                    ql = slice(qry0, qry0 + ATT_QB)
                    idx = 2 * h + c
                    kh = k_ref[kl, h * HEAD_W:(h + 1) * HEAD_W]
                    vth = vt_ref[h * HEAD_W:(h + 1) * HEAD_W, kl]
                    sc = _dot_nt(kh, qs_ref[idx, ql, :])
                    if mode != "far":
                        sc = sc + bias_ref[0, h, kl, ql]
                    if mode == "diag" and key0 + ATT_KB > qry0:
                        visible = key_chunk + (key0 >> shift) <= qry_chunk + (qry0 >> shift)
                        sc = jnp.where(visible, sc, NEG_INF)
                    m_old = m_ref[idx, :, ql]
                    col_max = jnp.max(sc, axis=0, keepdims=True)
                    if mode == "far":
                        m_new = jnp.maximum(m_old, col_max + far_ref[h])
                        offset = m_new - far_ref[h]
                    else:
                        m_new = jnp.maximum(m_old, col_max)
                        offset = m_new
                    alpha = jnp.exp(m_old - m_new)
                    p = jnp.exp(sc - offset)
                    l_ref[idx, :, ql] = alpha * l_ref[idx, :, ql] + jnp.sum(p, axis=0, keepdims=True)
                    acc_ref[idx, :, ql] = alpha * acc_ref[idx, :, ql] + _dot(vth, p.astype(BF16))
                    m_ref[idx, :, ql] = m_new

    @pl.when(kj == qi)
    def _():
        step("diag")

    @pl.when(kj == qi - 1)
    def _():
        step("near")

    @pl.when(kj < qi - 1)
    def _():
        step("far")

    @pl.when(kj == qi)
    def _():
        lam = lam_ref[:, :1]
        subt = subt_ref[...]
        for h in range(N_ATT_HEADS):
            ot = acc_ref[2 * h] / l_ref[2 * h] - lam * (acc_ref[2 * h + 1] / l_ref[2 * h + 1])
            ms = jnp.mean(ot * ot, axis=0, keepdims=True)
            ont = ((ot * lax.rsqrt(ms + EPS)) * subt) * (1.0 - LAM_INIT)
            o_ref[:, h * HEAD_W:(h + 1) * HEAD_W] = ont.T.astype(BF16)


def _attn_prompt(q, k, vt, rel_bias, lam_vec, sub):
    t = q.shape[0]
    tq = ATT_TQ
    assert t % tq == 0 and tq % CHUNK == 0 and tq >= MAX_DISTANCE
    nq = t // tq
    qi = np.concatenate([np.full(i + 1, i, np.int32) for i in range(nq)])
    kj = np.concatenate([np.arange(i + 1, dtype=np.int32) for i in range(nq)])
    bias = jnp.stack([_toeplitz_bias(rel_bias, tq, d) for d in range(2)])
    bias = jnp.transpose(bias, (0, 1, 3, 2))
    far = _lookup(rel_bias.astype(F32).T, _t5_bucket(jnp.full((1,), -(tq + 1), jnp.int32)))[0]
    sub = sub.reshape(HEAD_W, 1)
    v = vt

    grid_spec = pltpu.PrefetchScalarGridSpec(
        num_scalar_prefetch=2,
        grid=(qi.shape[0],),
        in_specs=[pl.BlockSpec((tq, ATT_WIDTH), lambda s, qi, kj: (qi[s], 0)),
                  pl.BlockSpec((tq, ATT_WIDTH), lambda s, qi, kj: (kj[s], 0)),
                  pl.BlockSpec((ATT_WIDTH, tq), lambda s, qi, kj: (0, kj[s])),
                  pl.BlockSpec((1, N_ATT_HEADS, tq, tq),
                               lambda s, qi, kj: (jnp.minimum(qi[s] - kj[s], 1), 0, 0, 0)),
                  pl.BlockSpec(memory_space=pltpu.SMEM),
                  pl.BlockSpec((1, HEAD_W), lambda s, qi, kj: (0, 0)),
                  pl.BlockSpec((HEAD_W, 1), lambda s, qi, kj: (0, 0))],
        out_specs=pl.BlockSpec((tq, ATT_WIDTH), lambda s, qi, kj: (qi[s], 0)),
        scratch_shapes=[pltpu.VMEM((2 * N_ATT_HEADS, tq, HEAD_W), BF16),
                        pltpu.VMEM((2 * N_ATT_HEADS, 1, tq), F32),
                        pltpu.VMEM((2 * N_ATT_HEADS, 1, tq), F32),
                        pltpu.VMEM((2 * N_ATT_HEADS, HEAD_W, tq), F32)])
    return pl.pallas_call(
        _attn_prompt_kernel,
        out_shape=jax.ShapeDtypeStruct((t, ATT_WIDTH), BF16),
        grid_spec=grid_spec,
        compiler_params=pltpu.CompilerParams(dimension_semantics=("arbitrary",), vmem_limit_bytes=40 * MIB),
        name="attn_prompt",
    )(jnp.asarray(qi), jnp.asarray(kj), q, k, v, bias, far, lam_vec, sub)


def _attn_sample_kernel(q_ref, kn_ref, vn_ref, ck_hbm, cv_hbm, bc_ref, bn_ref, mc_ref, mn_ref, lam_ref,
                        sub_ref, o_ref, kbuf_ref, vbuf_ref, sem, *, n_batch):
    b = pl.program_id(0)
    slot = b % 2

    def cache_copies(bb, into):
        copies = []
        for h in range(N_ATT_HEADS):
            copies.append(pltpu.make_async_copy(ck_hbm.at[bb, :, h, :], kbuf_ref.at[into, h], sem.at[0, into]))
            copies.append(pltpu.make_async_copy(cv_hbm.at[bb, :, h, :], vbuf_ref.at[into, h], sem.at[1, into]))
        return copies

    @pl.when(b == 0)
    def _():
        for cp in cache_copies(0, 0):
            cp.start()

    @pl.when(b + 1 < n_batch)
    def _():
        for cp in cache_copies(b + 1, 1 - slot):
            cp.start()

    for cp in cache_copies(b, slot):
        cp.wait()

    lam = lam_ref[...]
    sub = sub_ref[...]
    vis_c = mc_ref[...] > 0.5
    vis_n = mn_ref[...] > 0.5
    for h in range(N_ATT_HEADS):
        hs = slice(h * HEAD_W, (h + 1) * HEAD_W)
        qmaps = _split_maps(q_ref[:, hs].astype(F32))
        kc = kbuf_ref[slot, h].astype(BF16)
        vc = vbuf_ref[slot, h].astype(BF16)
        kn = kn_ref[:, hs]
        vn = vn_ref[:, hs]
        probs = []
        for c in range(2):
            sc = jnp.where(vis_c, _dot_nt(qmaps[c], kc) + bc_ref[h], NEG_INF)
            sn = jnp.where(vis_n, _dot_nt(qmaps[c], kn) + bn_ref[h], NEG_INF)
            m = jnp.maximum(jnp.max(sc, axis=-1, keepdims=True), jnp.max(sn, axis=-1, keepdims=True))
            pc = jnp.exp(sc - m)
            pn = jnp.exp(sn - m)
            den = jnp.sum(pc, axis=-1, keepdims=True) + jnp.sum(pn, axis=-1, keepdims=True)
            probs.append((pc / den, pn / den))
        wc = probs[0][0] - lam[:, :1] * probs[1][0]
        wn = probs[0][1] - lam[:, :1] * probs[1][1]
        o = _dot(wc.astype(BF16), vc) + _dot(wn.astype(BF16), vn)
        o_ref[:, hs] = _subln(o, None, sub).astype(BF16)


def _attn_sample(q, kn, vn, cache_k, cache_v, rel_bias, lam_vec, sub):
    bsz, past = cache_k.shape[:2]
    seq = q.shape[0] // bsz
    q_pos = past + jnp.arange(seq, dtype=jnp.int32)
    k_pos = jnp.arange(past + seq, dtype=jnp.int32)
    bias = _rel_bias_table(rel_bias, q_pos, k_pos)
    visible = ((k_pos[None, :] // CHUNK) <= (q_pos[:, None] // CHUNK)).astype(F32)
    const = lambda *shape: pl.BlockSpec(shape, lambda b: (0,) * len(shape))
    row = pl.BlockSpec((seq, ATT_WIDTH), lambda b: (b, 0))
    cache = pl.BlockSpec(memory_space=pl.ANY)
    head_major = pltpu.VMEM((2, N_ATT_HEADS, past, HEAD_W), F32)
    return pl.pallas_call(
        functools.partial(_attn_sample_kernel, n_batch=bsz),
        out_shape=jax.ShapeDtypeStruct(q.shape, BF16),
        grid=(bsz,),
        in_specs=[row, row, row, cache, cache, const(N_ATT_HEADS, seq, past), const(N_ATT_HEADS, seq, seq),
                  const(seq, past), const(seq, seq), const(1, HEAD_W), const(1, HEAD_W)],
        out_specs=row,
        scratch_shapes=[head_major, head_major, pltpu.SemaphoreType.DMA((2, 2))],
        compiler_params=pltpu.CompilerParams(dimension_semantics=("arbitrary",), vmem_limit_bytes=40 * MIB),
        name="attn_sample",
    )(q, kn, vn, cache_k, cache_v, bias[:, :, :past], bias[:, :, past:], visible[:, :past], visible[:, past:],
      lam_vec, sub)


def _ssd_constants(l):
    hl = HEADS_PER_GROUP * l
    lp = SSD_LP
    e_head = np.zeros((LANES, SSM_D_INNER), np.float32)
    for h in range(N_SSM_HEADS):
        e_head[h, h * SSM_HEAD_DIM:(h + 1) * SSM_HEAD_DIM] = 1.0
    e_grp = np.zeros((N_SSM_GROUPS, LANES, hl), np.float32)
    for g in range(N_SSM_GROUPS):
        for r in range(HEADS_PER_GROUP):
            e_grp[g, g * HEADS_PER_GROUP + r, r * l:(r + 1) * l] = 1.0
    tile8 = np.zeros((lp, hl), np.float32)
    for r in range(HEADS_PER_GROUP):
        tile8[np.arange(l), r * l + np.arange(l)] = 1.0
    causal = np.zeros((l, hl), np.float32)
    for r in range(HEADS_PER_GROUP):
        causal[:, r * l:(r + 1) * l] = np.tril(np.ones((l, l), np.float32))
    tri = np.tril(np.ones((l, l), np.float32))
    ones = np.ones((l, l), np.float32)
    bmask = np.zeros((hl, GROUP_W), np.float32)
    for r in range(HEADS_PER_GROUP):
        bmask[r * l:(r + 1) * l, r * SSM_HEAD_DIM:(r + 1) * SSM_HEAD_DIM] = 1.0
    as_bf = lambda a: jnp.asarray(a, BF16)
    return (as_bf(e_head), as_bf(e_grp), as_bf(tile8), jnp.asarray(causal), as_bf(tri), as_bf(ones),
            as_bf(bmask))


def _ssd_kernel(xbc_ref, z_ref, dt_ref, hist_ref, st0_ref, cw_ref, cb_ref, dtb_ref, alog_ref, dskip_ref,
                norm_ref, eh_ref, eg_ref, t8_ref, caus_ref, tri_ref, ones_ref, bmask_ref,
                yz_ref, st_ref, buf_ref, state_ref):
    c = pl.program_id(1)
    l = xbc_ref.shape[0]
    lp = SSD_LP
    hist_rows = hist_ref.shape[0]

    @pl.when(c == 0)
    def _():
        buf_ref[0:hist_rows, :] = hist_ref[...]
        for g in range(N_SSM_GROUPS):
            state_ref[g] = st0_ref[g].T

    u = xbc_ref[...]
    buf_ref[hist_rows:hist_rows + l, :] = u
    conv = cb_ref[...] + cw_ref[CONV_WIDTH - 1:CONV_WIDTH, :] * u
    for w in range(CONV_WIDTH - 1):
        shift = CONV_WIDTH - 1 - w
        conv = conv + cw_ref[w:w + 1, :] * buf_ref[hist_rows - shift:hist_rows - shift + l, :]
    buf_ref[0:hist_rows, :] = buf_ref[l:l + hist_rows, :]
    xc = conv * _sigmoid(conv)
    xs = xc[:, :SSM_D_INNER]
    bm = xc[:, SSM_D_INNER:SSM_D_INNER + N_SSM_GROUPS * D_STATE]
    cm = xc[:, SSM_D_INNER + N_SSM_GROUPS * D_STATE:]

    dt_in = dt_ref[...] + dtb_ref[...]
    dt = jnp.maximum(dt_in, 0.0) + jnp.log(1.0 + jnp.exp(-jnp.abs(dt_in)))
    a = -jnp.exp(alog_ref[...])
    acum = _xdot_l(tri_ref[...], dt * a)
    acum_t = acum.T if l == LANES else None
    eh = eh_ref[...]
    dt_e = _xdot_r(dt, eh)
    ac_e = _xdot_r(acum, eh)
    a_last = ac_e[l - 1:l, :]
    ecum = jnp.exp(ac_e)
    xdt = xs * dt_e
    xdtw_b = (xdt * jnp.exp(a_last - ac_e)).astype(BF16)
    xdt_b = xdt.astype(BF16)
    drow = jnp.exp(a_last)
    z = z_ref[...]
    caus = caus_ref[...] > 0.5
    t8 = t8_ref[...]
    t8_mask = t8[0:l, :] > 0
    bmask = bmask_ref[...] > 0
    row_pad = lp - l

    for g in range(N_SSM_GROUPS):
        gs = slice(g * GROUP_W, (g + 1) * GROUP_W)
        ns = slice(g * D_STATE, (g + 1) * D_STATE)
        bm_g = bm[:, ns]
        cm_b = cm[:, ns].astype(BF16)
        xw_g = xdtw_b[:, gs]
        if row_pad:
            bm_g = jnp.concatenate([bm_g, jnp.zeros((row_pad, D_STATE), F32)], axis=0)
            xw_g = jnp.concatenate([xw_g, jnp.zeros((row_pad, GROUP_W), BF16)], axis=0)
        bmt_b = bm_g.T.astype(BF16)
        cb8 = _dot(cm_b, _dot(bmt_b, t8).astype(BF16))
        a1 = _xdot_r(acum, eg_ref[g])
        if l == LANES:
            heads = range(g * HEADS_PER_GROUP, (g + 1) * HEADS_PER_GROUP)
            a2 = jnp.concatenate([acum_t[hd:hd + 1, :] for hd in heads], axis=1)
        else:
            a2 = _xdot_l(ones_ref[...], jnp.where(t8_mask, a1, 0.0))
        decay = jnp.where(caus, jnp.exp(jnp.where(caus, a1 - a2, 0.0)), 0.0)
        m_b = (cb8 * decay).astype(BF16)
        xg = xdt_b[:, gs]
        bd = jnp.concatenate([xg] * HEADS_PER_GROUP, axis=0)
        bd = jnp.where(bmask, bd, jnp.zeros_like(bd))
        y = _dot(m_b, bd)
        st_g = state_ref[g]
        y = y + _dot(cm_b, st_g.astype(BF16)) * ecum[:, gs]
        y = y + dskip_ref[:, gs] * xs[:, gs]
        state_ref[g] = st_g * drow[:, gs] + _dot(bmt_b, xw_g)
        zg = z[:, gs]
        yz = y * (zg * _sigmoid(zg))
        yz_ref[:, gs] = (_rms(yz) * norm_ref[:, gs]).astype(BF16)

    @pl.when(c == pl.num_programs(1) - 1)
    def _():
        for g in range(N_SSM_GROUPS):
            st_ref[g] = state_ref[g].T


def _ssd(xbc, z, dt, hist8, st0, conv_w, conv_b, dt_bias, a_log, dskip_e, ssm_norm, bsz, l):
    rows = xbc.shape[0]
    seq = rows // bsz
    assert seq % l == 0 and l % SUBLANES == 0 and l <= SSD_LP and l >= SUBLANES
    nc = seq // l
    consts = _ssd_constants(l)
    rowblk = lambda width: pl.BlockSpec((l, width), lambda b, c: (b * nc + c, 0))
    const = lambda arr: pl.BlockSpec(arr.shape, lambda b, c: (0,) * arr.ndim)
    params = (conv_w, conv_b, dt_bias, a_log, dskip_e, ssm_norm)
    return pl.pallas_call(
        _ssd_kernel,
        out_shape=(jax.ShapeDtypeStruct((rows, SSM_D_INNER), BF16),
                   jax.ShapeDtypeStruct(st0.shape, F32)),
        grid=(bsz, nc),
        in_specs=[rowblk(CONV_DIM), rowblk(SSM_D_INNER), rowblk(LANES),
                  pl.BlockSpec((None,) + hist8.shape[1:], lambda b, c: (b, 0, 0)),
                  pl.BlockSpec((None,) + st0.shape[1:], lambda b, c: (b, 0, 0, 0))]
                 + [const(p) for p in params] + [const(k) for k in consts],
        out_specs=(rowblk(SSM_D_INNER), pl.BlockSpec((None,) + st0.shape[1:], lambda b, c: (b, 0, 0, 0))),
        scratch_shapes=[pltpu.VMEM((SUBLANES + l, CONV_DIM), F32),
                        pltpu.VMEM((N_SSM_GROUPS, D_STATE, GROUP_W), F32)],
        compiler_params=pltpu.CompilerParams(dimension_semantics=("arbitrary", "arbitrary"),
                                             vmem_limit_bytes=48 * MIB),
        name="ssd",
    )(xbc, z, dt, hist8, st0, *params, *consts)


def _mix_kernel(on_ref, yz_ref, ga_ref, gs_ref, wa_ref, ws_ref, o_ref):
    att = _dot(on_ref[...], wa_ref[...])
    ssm = _dot(yz_ref[...], ws_ref[...])
    o_ref[...] = (_sigmoid(ga_ref[...]) * att + _sigmoid(gs_ref[...]) * ssm).astype(BF16)


def _mix(on, yz, ga, gs, wa, ws):
    t = on.shape[0]
    tm = TOK_TM
    rowblk = lambda width: pl.BlockSpec((tm, width), lambda i: (i, 0))
    const = lambda arr: pl.BlockSpec(arr.shape, lambda i: (0,) * arr.ndim)
    return pl.pallas_call(
        _mix_kernel,
        out_shape=jax.ShapeDtypeStruct((t, D_MODEL), BF16),
        grid=(t // tm,),
        in_specs=[rowblk(ATT_WIDTH), rowblk(SSM_D_INNER), rowblk(D_MODEL), rowblk(D_MODEL), const(wa), const(ws)],
        out_specs=rowblk(D_MODEL),
        compiler_params=pltpu.CompilerParams(dimension_semantics=("arbitrary",), vmem_limit_bytes=48 * MIB),
        name="mix",
    )(on, yz, ga, gs, wa, ws)


def _pack_bf16_pairs(x):
    w = x.shape[1] // 2
    lo = lax.bitcast_convert_type(x[:, :w].astype(BF16).astype(F32), U32)
    hi = lax.bitcast_convert_type(x[:, w:].astype(BF16).astype(F32), U32)
    return hi | (lo >> 16)


def _unpack_bf16_pairs(words):
    lo = lax.bitcast_convert_type(words << 16, F32)
    hi = lax.bitcast_convert_type(words & jnp.uint32(0xFFFF0000), F32)
    return jnp.concatenate([lo, hi], axis=1).astype(BF16)


def _resid_kernel(x_ref, mixed_ref, wo_ref, g_ref, wr_ref, br_ref, lt_ref, cin_ref,
                  h_ref, xp_ref, route_ref, cnt_ref, carry_ref):
    i = pl.program_id(0)

    @pl.when(i == 0)
    def _():
        carry_ref[...] = cin_ref[...]

    h = x_ref[...] + _dot(mixed_ref[...], wo_ref[...])
    h_ref[...] = h
    xn = _rms(h) * g_ref[...]
    xp_ref[...] = _pack_bf16_pairs(xn)
    logits = _dot(xn.astype(BF16), wr_ref[...]) + br_ref[...]

    lane = lax.broadcasted_iota(jnp.int32, logits.shape, 1)
    lane_f = lane.astype(F32)
    rest = logits
    vals, idxs, sels = [], [], []
    for _ in range(TOP_K):
        m = jnp.max(rest, axis=-1, keepdims=True)
        idx = jnp.min(jnp.where(rest == m, lane_f, float(LANES)), axis=-1, keepdims=True)
        sel = lane_f == idx
        rest = jnp.where(sel, -jnp.inf, rest)
        vals.append(m)
        idxs.append(idx)
        sels.append(sel)
    exps = [jnp.exp(v - vals[0]) for v in vals]
    den = exps[0] + exps[1] + exps[2] + exps[3]
    onehot = jnp.zeros(logits.shape, F32)
    for sel in sels:
        onehot = onehot + jnp.where(sel, 1.0, 0.0)
    before = _dot(lt_ref[...], onehot.astype(BF16)) + carry_ref[...]
    route = jnp.zeros(logits.shape, F32)
    for k in range(TOP_K):
        pos = jnp.sum(jnp.where(sels[k], before, 0.0), axis=-1, keepdims=True)
        route = route + jnp.where(lane == k, idxs[k], 0.0)
        route = route + jnp.where(lane == TOP_K + k, exps[k] / den, 0.0)
        route = route + jnp.where(lane == 2 * TOP_K + k, pos, 0.0)
    route_ref[...] = route
    carry_ref[...] = carry_ref[...] + jnp.sum(onehot, axis=0, keepdims=True)
    cnt_ref[...] = carry_ref[...]


def _resid(x, mixed, wo, g_ffn, wr, br, counts_in):
    t = x.shape[0]
    tm = TOK_TM
    lt = jnp.asarray(np.tril(np.ones((tm, tm), np.float32), -1), BF16)
    rowblk = lambda width: pl.BlockSpec((tm, width), lambda i: (i, 0))
    const = lambda arr: pl.BlockSpec(arr.shape, lambda i: (0,) * arr.ndim)
    return pl.pallas_call(
        _resid_kernel,
        out_shape=(jax.ShapeDtypeStruct((t, D_MODEL), F32),
                   jax.ShapeDtypeStruct((t, D_MODEL // 2), U32),
                   jax.ShapeDtypeStruct((t, LANES), F32),
                   jax.ShapeDtypeStruct((1, LANES), F32)),
        grid=(t // tm,),
        in_specs=[rowblk(D_MODEL), rowblk(D_MODEL), const(wo), const(g_ffn), const(wr), const(br), const(lt),
                  const(counts_in)],
        out_specs=(rowblk(D_MODEL), rowblk(D_MODEL // 2), rowblk(LANES), pl.BlockSpec((1, LANES), lambda i: (0, 0))),
        scratch_shapes=[pltpu.VMEM((1, LANES), F32)],
        compiler_params=pltpu.CompilerParams(dimension_semantics=("arbitrary",), vmem_limit_bytes=48 * MIB),
        name="resid_route",
    )(x, mixed, wo, g_ffn, wr, br, lt, counts_in)


def _row_copy(src, src_row, dst, dst_row, sem):
    return pltpu.make_async_copy(src.at[pl.ds(src_row, 1)], dst.at[pl.ds(dst_row, 1)], sem)


def _dispatch_kernel(dest_ref, xp_ref, *rest, first):
    if first:
        xs_ref, sem, zero_ref = rest

        @pl.when(pl.program_id(0) == 0)
        def _():
            zero_ref[...] = jnp.zeros(zero_ref.shape, zero_ref.dtype)
            blk = zero_ref.shape[0]

            def fill(b):
                return pltpu.make_async_copy(zero_ref, xs_ref.at[pl.ds(pl.multiple_of(b * blk, blk), blk)], sem)

            def start_fill(b, carry):
                fill(b).start()
                return carry

            def wait_fill(b, carry):
                fill(b).wait()
                return carry

            lax.fori_loop(0, xs_ref.shape[0] // blk, start_fill, 0)
            lax.fori_loop(0, xs_ref.shape[0] // blk, wait_fill, 0)
    else:
        _, xs_ref, sem = rest

    def issue(t, carry):
        for k in range(TOP_K):
            _row_copy(xp_ref, t, xs_ref, dest_ref[t * TOP_K + k], sem).start()
        return carry

    lax.fori_loop(0, TOK_TM, issue, 0, unroll=DMA_UNROLL)

    def drain(t, carry):
        for k in range(TOP_K):
            _row_copy(xp_ref, 0, xs_ref, 0, sem).wait()
        return carry

    lax.fori_loop(0, TOK_TM, drain, 0, unroll=DMA_UNROLL)


def _dispatch(dest_flat, xp, xs, n_rows):
    t, width = xp.shape
    first = xs is None
    assert n_rows % TOK_TM == 0
    in_specs = [pl.BlockSpec((TOK_TM * TOP_K,), lambda i: (i,), memory_space=pltpu.SMEM),
                pl.BlockSpec((TOK_TM, width), lambda i: (i, 0))]
    scratch = [pltpu.SemaphoreType.DMA(())]
    operands = [dest_flat, xp]
    if first:
        scratch.append(pltpu.VMEM((TOK_TM, width), xp.dtype))
    else:
        in_specs.append(pl.BlockSpec(memory_space=pl.ANY))
        operands.append(xs)
    return pl.pallas_call(
        functools.partial(_dispatch_kernel, first=first),
        out_shape=jax.ShapeDtypeStruct((n_rows, width), xp.dtype),
        grid=(t // TOK_TM,),
        in_specs=in_specs,
        out_specs=pl.BlockSpec(memory_space=pl.ANY),
        scratch_shapes=scratch,
        input_output_aliases={} if first else {2: 0},
        compiler_params=pltpu.CompilerParams(dimension_semantics=("arbitrary",)),
        name="dispatch_first" if first else "dispatch",
    )(*operands)


def _experts_kernel(sbe_ref, sbs_ref, sbr_ref, nsb_ref, xs_ref, wg_ref, wu_ref, wd_ref, bg_ref, bu_ref,
                    bd_ref, y_ref, xw_ref, xb_ref, acc_ref, wgb_ref, wub_ref, wdb_ref, sem_in, sem_out):
    s = pl.program_id(0)
    f = pl.program_id(1)
    nf = pl.num_programs(1)
    n_sub = SUPER_ROWS // ROW_BLK

    @pl.when(s < nsb_ref[0])
    def _():
        start = pl.multiple_of(sbs_ref[s], ROW_BLK)
        rows = sbr_ref[s]

        def x_copy(sb, slot):
            src = xs_ref.at[pl.ds(pl.multiple_of(sbs_ref[sb], ROW_BLK), SUPER_ROWS)]
            return pltpu.make_async_copy(src, xw_ref.at[slot], sem_in.at[slot])

        @pl.when(f == 0)
        def _():
            slot = s % 2

            @pl.when(s == 0)
            def _():
                x_copy(0, 0).start()

            x_copy(s, slot).wait()

            @pl.when(s + 1 < nsb_ref[0])
            def _():
                x_copy(s + 1, 1 - slot).start()

            xb_ref[...] = _unpack_bf16_pairs(xw_ref[slot])
            acc_ref[...] = jnp.zeros(acc_ref.shape, F32)

        wgb_ref[...] = wg_ref[0].astype(BF16)
        wub_ref[...] = wu_ref[0].astype(BF16)
        wdb_ref[...] = wd_ref[0].astype(BF16)
        half_rows = SUPER_ROWS // 2
        for half in range(2):
            rs = slice(half * half_rows, (half + 1) * half_rows)
            x = xb_ref[rs, :]
            gate = _dot(x, wgb_ref[...]) + bg_ref[0]
            up = _dot(x, wub_ref[...]) + bu_ref[0]
            gate = jnp.minimum(gate, SWIGLU_LIMIT)
            up = jnp.clip(up, -SWIGLU_LIMIT, SWIGLU_LIMIT)
            act = (up + 1.0) * gate * _sigmoid(SWIGLU_ALPHA * gate)
            acc_ref[rs, :] += _dot(act.astype(BF16), wdb_ref[...])

        @pl.when(f == nf - 1)
        def _():
            def out_copy(sub):
                rs = pl.ds(sub * ROW_BLK, ROW_BLK)
                return pltpu.make_async_copy(acc_ref.at[rs], y_ref.at[pl.ds(start + sub * ROW_BLK, ROW_BLK)],
                                             sem_out)

            for sub in range(n_sub):
                @pl.when(sub * ROW_BLK < rows)
                def _():
                    rs = slice(sub * ROW_BLK, (sub + 1) * ROW_BLK)
                    acc_ref[rs, :] += bd_ref[0]
                    out_copy(sub).start()

            for sub in range(n_sub):
                @pl.when(sub * ROW_BLK < rows)
                def _():
                    out_copy(sub).wait()

    @pl.when((s == pl.num_programs(0) - 1) & (f == nf - 1))
    def _():
        zero_ref = acc_ref.at[pl.ds(0, ROW_BLK)]
        zero_ref[...] = jnp.zeros(zero_ref.shape, F32)
        n_blocks = y_ref.shape[0] // ROW_BLK

        def tail_copy(b):
            return pltpu.make_async_copy(zero_ref, y_ref.at[pl.ds(pl.multiple_of(b * ROW_BLK, ROW_BLK), ROW_BLK)],
                                         sem_out)

        def issue(b, carry):
            tail_copy(b).start()
            return carry

        def drain(b, carry):
            tail_copy(b).wait()
            return carry

        lax.fori_loop(nsb_ref[1], n_blocks, issue, 0)
        lax.fori_loop(nsb_ref[1], n_blocks, drain, 0)


def _experts(sb_expert, sb_start, sb_rows, n_sb, xs, w_gate_up, b_gate_up, w_down, b_down, n_rows):
    n_super = sb_expert.shape[0]
    nf = D_FF // FF_TILE

    def widx(s, f, sbe, sbs, sbr, nsb):
        live = s < nsb[0]
        return sbe[s], jnp.where(live, f, nf - 1)

    def gate_map(s, f, *pref):
        e, ff = widx(s, f, *pref)
        return (e, 0, ff)

    def up_map(s, f, *pref):
        e, ff = widx(s, f, *pref)
        return (e, 0, nf + ff)

    def down_map(s, f, *pref):
        e, ff = widx(s, f, *pref)
        return (e, ff, 0)

    def bias_map(s, f, *pref):
        return (widx(s, f, *pref)[0], 0, 0)

    b_gu3 = b_gate_up.reshape(N_EXPERTS, 1, 2 * D_FF)
    b_dn3 = b_down.reshape(N_EXPERTS, 1, D_MODEL)
    grid_spec = pltpu.PrefetchScalarGridSpec(
        num_scalar_prefetch=4,
        grid=(n_super, nf),
        in_specs=[pl.BlockSpec(memory_space=pl.ANY),
                  pl.BlockSpec((1, D_MODEL, FF_TILE), gate_map),
                  pl.BlockSpec((1, D_MODEL, FF_TILE), up_map),
                  pl.BlockSpec((1, FF_TILE, D_MODEL), down_map),
                  pl.BlockSpec((1, 1, FF_TILE), gate_map),
                  pl.BlockSpec((1, 1, FF_TILE), up_map),
                  pl.BlockSpec((1, 1, D_MODEL), bias_map)],
        out_specs=pl.BlockSpec(memory_space=pl.ANY),
        scratch_shapes=[pltpu.VMEM((2, SUPER_ROWS, D_MODEL // 2), U32),
                        pltpu.VMEM((SUPER_ROWS, D_MODEL), BF16),
                        pltpu.VMEM((SUPER_ROWS, D_MODEL), F32),
                        pltpu.VMEM((D_MODEL, FF_TILE), BF16),
                        pltpu.VMEM((D_MODEL, FF_TILE), BF16),
                        pltpu.VMEM((FF_TILE, D_MODEL), BF16),
                        pltpu.SemaphoreType.DMA((2,)),
                        pltpu.SemaphoreType.DMA(())])
    return pl.pallas_call(
        _experts_kernel,
        out_shape=jax.ShapeDtypeStruct((n_rows, D_MODEL), F32),
        grid_spec=grid_spec,
        compiler_params=pltpu.CompilerParams(dimension_semantics=("arbitrary", "arbitrary"),
                                             vmem_limit_bytes=56 * MIB),
        name="experts",
    )(sb_expert, sb_start, sb_rows, n_sb, xs, w_gate_up, w_gate_up, w_down, b_gu3, b_gu3, b_dn3)


def _combine_kernel(dest_ref, dest_next_ref, h_ref, route_ref, p_ref, y_ref, gple_ref, wpg_ref, wpp_ref,
                    gfin_ref, o_ref, gbuf_ref, sem, *, n_tiles):
    i = pl.program_id(0)
    slot = i % 2

    def gather_tile(d_ref, into):
        def issue(t, carry):
            for k in range(TOP_K):
                _row_copy(y_ref, d_ref[t * TOP_K + k], gbuf_ref.at[into].at[k], t, sem.at[into]).start()
            return carry

        lax.fori_loop(0, TOK_TM, issue, 0, unroll=DMA_UNROLL)

    @pl.when(i == 0)
    def _():
        gather_tile(dest_ref, 0)

    @pl.when(i + 1 < n_tiles)
    def _():
        gather_tile(dest_next_ref, 1 - slot)

    def drain(t, carry):
        for k in range(TOP_K):
            _row_copy(y_ref, 0, gbuf_ref.at[slot].at[k], 0, sem.at[slot]).wait()
        return carry

    lax.fori_loop(0, TOK_TM, drain, 0, unroll=DMA_UNROLL)

    route = route_ref[...]
    h = h_ref[...]
    for k in range(TOP_K):
        h = h + route[:, TOP_K + k:TOP_K + k + 1] * gbuf_ref[slot, k]
    xn = (_rms(h) * gple_ref[...]).astype(BF16)
    gate = _sigmoid(_dot(xn, wpg_ref[...]))
    h = h + gate * _dot(p_ref[...].astype(BF16), wpp_ref[...])
    o_ref[...] = _rms(h) * gfin_ref[...]


def _combine(dest_flat, h, route, p, y_sorted, g_ple, wpg, wpp, g_final):
    t = h.shape[0]
    tm = TOK_TM
    rowblk = lambda width: pl.BlockSpec((tm, width), lambda i: (i, 0))
    const = lambda arr: pl.BlockSpec(arr.shape, lambda i: (0,) * arr.ndim)
    return pl.pallas_call(
        functools.partial(_combine_kernel, n_tiles=t // tm),
        out_shape=jax.ShapeDtypeStruct((t, D_MODEL), F32),
        grid=(t // tm,),
        in_specs=[pl.BlockSpec((tm * TOP_K,), lambda i: (i,), memory_space=pltpu.SMEM),
                  pl.BlockSpec((tm * TOP_K,), lambda i: (jnp.minimum(i + 1, t // tm - 1),),
                               memory_space=pltpu.SMEM),
                  rowblk(D_MODEL), rowblk(LANES), rowblk(PLE_DIM),
                  pl.BlockSpec(memory_space=pl.ANY),
                  const(g_ple), const(wpg), const(wpp), const(g_final)],
        out_specs=rowblk(D_MODEL),
        scratch_shapes=[pltpu.VMEM((2, TOP_K, tm, D_MODEL), F32), pltpu.SemaphoreType.DMA((2,))],
        compiler_params=pltpu.CompilerParams(dimension_semantics=("arbitrary",), vmem_limit_bytes=52 * MIB),
        name="combine",
    )(dest_flat, dest_flat, h, route, p, y_sorted, g_ple, wpg, wpp, g_final)


def _super_blocks(counts, n_super):
    padded = (counts + ROW_BLK - 1) // ROW_BLK * ROW_BLK
    pad_start = jnp.cumsum(padded) - padded
    per_expert = (counts + SUPER_ROWS - 1) // SUPER_ROWS
    sb_end = jnp.cumsum(per_expert)
    n_sb = sb_end[-1]
    s = jnp.arange(n_super, dtype=jnp.int32)
    s_live = jnp.minimum(s, jnp.maximum(n_sb - 1, 0))
    expert = jnp.sum(sb_end[None, :] <= s_live[:, None], axis=1).astype(jnp.int32)
    expert = jnp.minimum(expert, N_EXPERTS - 1)
    within = s_live - _lookup(sb_end - per_expert, expert)
    start = _lookup(pad_start, expert) + within * SUPER_ROWS
    rows = jnp.where(s < n_sb, jnp.clip(_lookup(counts, expert) - within * SUPER_ROWS, 0, SUPER_ROWS), 0)
    first_unused_block = jnp.sum(padded) // ROW_BLK
    return (pad_start, expert, start.astype(jnp.int32), rows.astype(jnp.int32),
            jnp.stack([n_sb, first_unused_block]).astype(jnp.int32))


def _mixer(x, cache_k, cache_v, st0, conv0, prep, ssd_l):
    bsz, seq, _ = x.shape
    x2 = x.reshape(bsz * seq, D_MODEL)
    q, k, v, kb, vb, vt, z, xbc, ga, gs, dt = _inproj(x2, prep["g_mix"], prep["w_main"], prep["w_dt"])
    if cache_k is None:
        assert bsz == 1
        on = _attn_prompt(q, kb, vt, prep["rel_bias"], prep["lam"], prep["subln"])
    else:
        past = cache_k.shape[1]
        on = _attn_sample(q, kb, vb, cache_k, cache_v, prep["rel_bias"], prep["lam"], prep["subln"])
    hist8 = jnp.pad(conv0, ((0, 0), (SUBLANES - (CONV_WIDTH - 1), 0), (0, 0)))
    st0_t = st0.reshape(bsz, N_SSM_GROUPS, GROUP_W, D_STATE)
    yz, st_t = _ssd(xbc, z, dt, hist8, st0_t, prep["conv_w"], prep["conv_b"], prep["dt_bias"], prep["a_log"],
                    prep["dskip_e"], prep["ssm_norm"], bsz, ssd_l)
    st_new = st_t.reshape(1, bsz, N_SSM_HEADS, SSM_HEAD_DIM, D_STATE)
    mixed = _mix(on, yz, ga, gs, prep["w_attn_out"], prep["w_ssm_out"])
    k_rows = k.reshape(1, bsz, seq, N_ATT_HEADS, HEAD_W)
    v_rows = v.reshape(1, bsz, seq, N_ATT_HEADS, HEAD_W)
    conv_new = xbc.reshape(bsz, seq, CONV_DIM)[:, seq - (CONV_WIDTH - 1):].reshape(1, bsz, CONV_WIDTH - 1, CONV_DIM)
    return x2, mixed, k_rows, v_rows, st_new, conv_new


def kernel(x_prompt, x_sample, cache_k, cache_v, state_ssm, state_conv, p_prompt, p_sample, rel_bias, w_in,
           lambda_q1, lambda_k1, lambda_q2, lambda_k2, attn_subln, w_attn_out, conv_w, conv_b, dt_bias, a_log,
           d_skip, ssm_norm, w_ssm_out, w_o, g_mix, g_ffn, w_router, b_router, w_gate_up, b_gate_up, w_down,
           b_down, g_ple, w_ple_gate, w_ple_proj, g_final):
    w = w_in[0]
    c_dt = 3 * ATT_WIDTH + SSM_D_INNER + CONV_DIM
    w_main = (w[:, :c_dt].astype(BF16), w[:, c_dt + N_SSM_HEADS:].astype(BF16))
    w_dt = jnp.pad(w[:, c_dt:c_dt + N_SSM_HEADS], ((0, 0), (0, LANES - N_SSM_HEADS))).astype(BF16)
    lam = (jnp.exp(jnp.sum(lambda_q1[0] * lambda_k1[0]).astype(F32))
           - jnp.exp(jnp.sum(lambda_q2[0] * lambda_k2[0]).astype(F32)) + LAM_INIT)
    pad_heads = lambda v: jnp.pad(v.reshape(1, N_SSM_HEADS), ((0, 0), (0, LANES - N_SSM_HEADS)))
    prep = dict(
        g_mix=g_mix, w_main=w_main, w_dt=w_dt, rel_bias=rel_bias,
        lam=jnp.full((1, HEAD_W), lam, F32), subln=attn_subln,
        conv_w=conv_w[0], conv_b=conv_b, dt_bias=pad_heads(dt_bias[0]), a_log=pad_heads(a_log[0]),
        dskip_e=jnp.repeat(d_skip[0], SSM_HEAD_DIM).reshape(1, SSM_D_INNER), ssm_norm=ssm_norm,
        w_attn_out=w_attn_out[0].astype(BF16), w_ssm_out=w_ssm_out[0].astype(BF16))
    wo = w_o[0].astype(BF16)
    wr = jnp.pad(w_router[0], ((0, 0), (0, LANES - N_EXPERTS))).astype(BF16)
    br = jnp.pad(b_router, ((0, 0), (0, LANES - N_EXPERTS)), constant_values=NEG_INF)
    wpg = w_ple_gate[0].astype(BF16)
    wpp = w_ple_proj[0].astype(BF16)

    bp, sp, _ = x_prompt.shape
    bs, ss, _ = x_sample.shape
    zeros_state = jnp.zeros((bp, N_SSM_HEADS, SSM_HEAD_DIM, D_STATE), F32)
    zeros_conv = jnp.zeros((bp, CONV_WIDTH - 1, CONV_DIM), F32)
    xp2, mixed_p, k_p, v_p, ssm_p, conv_p = _mixer(x_prompt, None, None, zeros_state, zeros_conv, prep, SSD_L_PROMPT)
    xs2, mixed_s, k_s, v_s, ssm_s, conv_s = _mixer(x_sample, cache_k[0], cache_v[0], state_ssm[0], state_conv[0],
                                                   prep, ss)

    zero_counts = jnp.zeros((1, LANES), F32)
    h_p, xpk_p, route_p, cnt_p = _resid(xp2, mixed_p, wo, g_ffn, wr, br, zero_counts)
    h_s, xpk_s, route_s, cnt = _resid(xs2, mixed_s, wo, g_ffn, wr, br, cnt_p)
    n_tok = xp2.shape[0] + xs2.shape[0]
    n_slots = n_tok * TOP_K
    n_super = -(-n_slots // SUPER_ROWS) + N_EXPERTS
    n_rows = n_slots + N_EXPERTS * ROW_BLK
    counts = cnt[0, :N_EXPERTS].astype(jnp.int32)
    pad_start, sb_expert, sb_start, sb_rows, n_sb = _super_blocks(counts, n_super)

    def dest_of(route):
        expert = route[:, :TOP_K].astype(jnp.int32)
        pos = route[:, 2 * TOP_K:3 * TOP_K].astype(jnp.int32)
        return (_lookup(pad_start, expert) + pos).astype(jnp.int32).reshape(-1)

    dest_p = dest_of(route_p)
    dest_s = dest_of(route_s)
    xs_sorted = _dispatch(dest_p, xpk_p, None, n_rows + SUPER_ROWS)
    xs_sorted = _dispatch(dest_s, xpk_s, xs_sorted, n_rows + SUPER_ROWS)
    y_sorted = _experts(sb_expert, sb_start, sb_rows, n_sb, xs_sorted, w_gate_up[0], b_gate_up[0], w_down[0],
                        b_down[0], n_rows)

    y_p = _combine(dest_p, h_p, route_p, p_prompt[0].reshape(-1, PLE_DIM), y_sorted, g_ple, wpg, wpp,
                   g_final.reshape(1, D_MODEL))
    y_s = _combine(dest_s, h_s, route_s, p_sample[0].reshape(-1, PLE_DIM), y_sorted, g_ple, wpg, wpp,
                   g_final.reshape(1, D_MODEL))
    return (y_p.reshape(bp, sp, D_MODEL), y_s.reshape(bs, ss, D_MODEL), k_p, v_p, ssm_p, conv_p,
            k_s, v_s, ssm_s, conv_s)
```
